```python
import math
import jax, jax.numpy as jnp
from jax import lax
import numpy as np

D_MODEL = 1024
BATCH = 16
SEQ = 2048
DEPTH = 2
DEC_BATCH = 128
DEC_SEQ = 8
PAST_LEN = 16384
PAGE_SIZE = 128

D_MIX = D_MODEL
D_SSD = D_MIX // 2
SSD_HEAD_DIM = 64
N_SSD_HEADS = D_SSD // SSD_HEAD_DIM
N_SSD_GROUPS = 2
SSD_HEADS_PER_GROUP = N_SSD_HEADS // N_SSD_GROUPS
D_STATE = 128
CONV_W = 4
CONV_DIM = D_SSD + 2 * N_SSD_GROUPS * D_STATE
D_ATTN = D_MIX - D_SSD
HEAD_DIM = 64
N_Q_HEADS = D_ATTN // HEAD_DIM
N_KV_HEADS = 2
Q_PER_KV = N_Q_HEADS // N_KV_HEADS
KV_DIM = N_KV_HEADS * HEAD_DIM
WINDOW = 128
BLOCK = 128
ROPE_THETA = 10000.0
N_META = 16
META_PAD = (-N_META) % BLOCK
IN_DIM = D_SSD + CONV_DIM + N_SSD_HEADS + D_ATTN + 2 * KV_DIM
D_FF = 2816
N_EXPERTS = 8
TOP_K = 2
MOE_BLOCK = 128
N_DENSE = (DEPTH + 1) // 2
N_MOE = DEPTH // 2
EPS = 1e-6
NEG = -1e30
ATTN_SCALE = HEAD_DIM ** -0.5

kernel_name = 'hymba_ssd_swa_sink_moe_step'


def rmsnorm(x, w):
    xf = x.astype(jnp.float32)
    y = xf * lax.rsqrt(jnp.mean(xf * xf, axis=-1, keepdims=True) + EPS)
    return (y * w.astype(jnp.float32)).astype(x.dtype)


def rope(x, pos):
    half = HEAD_DIM // 2
    inv_freq = ROPE_THETA ** (-jnp.arange(half, dtype=jnp.float32) / half)
    ang = pos.astype(jnp.float32)[:, None] * inv_freq[None, :]
    cos = jnp.cos(ang)[:, None, :]
    sin = jnp.sin(ang)[:, None, :]
    xf = x.astype(jnp.float32)
    x1, x2 = xf[..., :half], xf[..., half:]
    return jnp.concatenate([x1 * cos - x2 * sin, x2 * cos + x1 * sin], -1).astype(x.dtype)


def split_in(p):
    idx = np.cumsum([D_SSD, CONV_DIM, N_SSD_HEADS, D_ATTN, KV_DIM]).tolist()
    return jnp.split(p, idx, axis=-1)


def causal_conv(u, prev, w, b):
    up = jnp.concatenate([prev, u], axis=1)
    t = u.shape[1]
    out = b + sum(up[:, k:k + t] * w[k] for k in range(CONV_W))
    return jax.nn.silu(out), up[:, up.shape[1] - (CONV_W - 1):]


def ssd_scan(x, dt, a, bm, cm, h0, chunk):
    bsz, L = x.shape[:2]
    nc = L // chunk
    G, R, P, N = N_SSD_GROUPS, SSD_HEADS_PER_GROUP, SSD_HEAD_DIM, D_STATE
    f32 = jnp.float32
    xf = x.astype(f32).reshape(bsz, nc, chunk, G, R, P)
    dtc = dt.reshape(bsz, nc, chunk, G, R)
    bc = bm.astype(f32).reshape(bsz, nc, chunk, G, N)
    cc = cm.astype(f32).reshape(bsz, nc, chunk, G, N)
    cs = jnp.cumsum(dtc * a.reshape(G, R), axis=2)
    xdt = xf * dtc[..., None]
    seg = cs[:, :, :, None] - cs[:, :, None, :]
    tril = jnp.tril(jnp.ones((chunk, chunk), bool))[:, :, None, None]
    decay = jnp.where(tril, jnp.exp(jnp.where(tril, seg, 0.0)), 0.0)
    cb = jnp.einsum('bcign,bcjgn->bcijg', cc, bc)
    y_diag = jnp.einsum('bcijgr,bcjgrp->bcigrp', decay * cb[..., None], xdt)
    last = cs[:, :, -1:]
    states = jnp.einsum('bcjgn,bcjgr,bcjgrp->bcgrpn', bc, jnp.exp(last - cs), xdt)
    chunk_dec = jnp.exp(last[:, :, 0])

    def step(h, inp):
        st, dc = inp
        return h * dc[..., None, None] + st, h

    h_fin, h_prev = lax.scan(step, h0.astype(f32).reshape(bsz, G, R, P, N),
                             (jnp.moveaxis(states, 1, 0), jnp.moveaxis(chunk_dec, 1, 0)))
    h_prev = jnp.moveaxis(h_prev, 0, 1)
    y_off = jnp.einsum('bcign,bcgrpn,bcigr->bcigrp', cc, h_prev, jnp.exp(cs))
    y = (y_diag + y_off).reshape(bsz, L, N_SSD_HEADS, P)
    return y, h_fin.reshape(bsz, N_SSD_HEADS, P, N)


def ssd_branch(z, xbc, dt_raw, conv_prev, h0, valid, chunk, lp):
    bsz, L = z.shape[:2]
    xbc_c, conv_new = causal_conv(xbc, conv_prev, lp['conv_w'], lp['conv_b'])
    xs, bm, cm = jnp.split(xbc_c, [D_SSD, D_SSD + N_SSD_GROUPS * D_STATE], axis=-1)
    xs = xs.reshape(bsz, L, N_SSD_HEADS, SSD_HEAD_DIM)
    bm = bm.reshape(bsz, L, N_SSD_GROUPS, D_STATE)
    cm = cm.reshape(bsz, L, N_SSD_GROUPS, D_STATE)
    dt = jax.nn.softplus(dt_raw.astype(jnp.float32) + lp['dt_bias'].astype(jnp.float32))
    if valid is not None:
        dt = dt * valid[None, :, None]
    a = -jnp.exp(lp['a_log'].astype(jnp.float32))
    y, h_new = ssd_scan(xs, dt, a, bm, cm, h0, chunk)
    y = y + lp['d_skip'].astype(jnp.float32)[:, None] * xs.astype(jnp.float32)
    y = y.reshape(bsz, L, D_SSD) * jax.nn.silu(z.astype(jnp.float32))
    gs = D_SSD // N_SSD_GROUPS
    y = rmsnorm(y.reshape(bsz, L, N_SSD_GROUPS, gs), lp['ssd_norm_w'].reshape(N_SSD_GROUPS, gs))
    return y.reshape(bsz, L, D_SSD).astype(z.dtype), h_new, conv_new


def attn_prep(q, k, v, pos, lp):
    bsz, t = q.shape[:2]
    q = rope(rmsnorm(q.reshape(bsz, t, N_Q_HEADS, HEAD_DIM), lp['q_norm_w']), pos)
    k = rope(rmsnorm(k.reshape(bsz, t, N_KV_HEADS, HEAD_DIM), lp['k_norm_w']), pos)
    return q, k, v.reshape(bsz, t, N_KV_HEADS, HEAD_DIM)


def visible(pos_q, pos_band):
    pos_meta = jnp.arange(N_META, dtype=jnp.int32)
    pq = pos_q[..., :, None]
    meta = pos_meta <= pq
    pk = pos_band[..., None, :]
    band = (pk >= N_META) & (pk <= pq) & (pq - pk < WINDOW)
    return jnp.concatenate([meta, band], axis=-1)


def sink_probs(s, sinks):
    sk = sinks.astype(jnp.float32).reshape(N_KV_HEADS, Q_PER_KV)[:, :, None, None]
    m = jnp.maximum(jnp.max(s, axis=-1, keepdims=True), sk)
    e = jnp.exp(s - m)
    return e / (jnp.sum(e, axis=-1, keepdims=True) + jnp.exp(sk - m))


def swa_prompt(q, k, v, pos, sinks):
    bsz, Lp = q.shape[:2]
    nb = Lp // BLOCK
    qb = q.reshape(bsz, nb, BLOCK, N_KV_HEADS, Q_PER_KV, HEAD_DIM)
    kb = k.reshape(bsz, nb, BLOCK, N_KV_HEADS, HEAD_DIM)
    vb = v.reshape(bsz, nb, BLOCK, N_KV_HEADS, HEAD_DIM)
    shift = lambda t: jnp.pad(t[:, :-1], ((0, 0), (1, 0), (0, 0), (0, 0), (0, 0)))
    k_band = jnp.concatenate([shift(kb), kb], axis=2)
    v_band = jnp.concatenate([shift(vb), vb], axis=2)
    pos_b = pos.reshape(nb, BLOCK)
    pos_band = jnp.concatenate([jnp.pad(pos_b[:-1], ((1, 0), (0, 0)), constant_values=-1), pos_b], axis=1)
    k_meta = k[:, META_PAD:META_PAD + N_META]
    v_meta = v[:, META_PAD:META_PAD + N_META]
    mask = visible(pos_b, pos_band)
    s_meta = jnp.einsum('bnqkgd,bjkd->bnkgqj', qb, k_meta, preferred_element_type=jnp.float32)
    s_band = jnp.einsum('bnqkgd,bnjkd->bnkgqj', qb, k_band, preferred_element_type=jnp.float32)
    s = jnp.concatenate([s_meta, s_band], axis=-1) * ATTN_SCALE
    s = jnp.where(mask[None, :, None, None], s, NEG)
    p = sink_probs(s, sinks).astype(v.dtype)
    o = (jnp.einsum('bnkgqj,bjkd->bnqkgd', p[..., :N_META], v_meta)
         + jnp.einsum('bnkgqj,bnjkd->bnqkgd', p[..., N_META:], v_band))
    return o.reshape(bsz, Lp, D_ATTN)


def swa_sample(q, k, v, k_meta, v_meta, k_win, v_win, sinks):
    bsz, t = q.shape[:2]
    pos_q = PAST_LEN + jnp.arange(t, dtype=jnp.int32)
    pos_band = jnp.concatenate([PAST_LEN - WINDOW + jnp.arange(WINDOW, dtype=jnp.int32), pos_q])
    k_band = jnp.concatenate([k_win.astype(k.dtype), k], axis=1)
    v_band = jnp.concatenate([v_win.astype(v.dtype), v], axis=1)
    mask = visible(pos_q, pos_band)
    qg = q.reshape(bsz, t, N_KV_HEADS, Q_PER_KV, HEAD_DIM)
    km = k_meta.astype(k.dtype)
    vm = v_meta.astype(v.dtype)
    s_meta = jnp.einsum('bqkgd,bjkd->bkgqj', qg, km, preferred_element_type=jnp.float32)
    s_band = jnp.einsum('bqkgd,bjkd->bkgqj', qg, k_band, preferred_element_type=jnp.float32)
    s = jnp.where(mask, jnp.concatenate([s_meta, s_band], axis=-1) * ATTN_SCALE, NEG)
    p = sink_probs(s, sinks).astype(v.dtype)
    o = (jnp.einsum('bkgqj,bjkd->bqkgd', p[..., :N_META], vm)
         + jnp.einsum('bkgqj,bjkd->bqkgd', p[..., N_META:], v_band))
    nw = k_band.shape[1] - WINDOW
    return o.reshape(bsz, t, D_ATTN), k_band[:, nw:], v_band[:, nw:]


def merge_out(y_ssd, y_att, lp):
    return jnp.concatenate([y_ssd, rmsnorm(y_att, lp['attn_norm_w'])], axis=-1) @ lp['w_out']


def mixer_prompt(h, lp):
    bsz, L = h.shape[:2]
    Lp = L + META_PAD
    proj = jnp.pad(h @ lp['w_in'], ((0, 0), (META_PAD, 0), (0, 0)))
    pos = jnp.arange(Lp, dtype=jnp.int32) - META_PAD
    valid = (pos >= 0).astype(jnp.float32)
    z, xbc, dt_raw, q, k, v = split_in(proj)
    conv0 = jnp.zeros((bsz, CONV_W - 1, CONV_DIM), proj.dtype)
    h0 = jnp.zeros((bsz, N_SSD_HEADS, SSD_HEAD_DIM, D_STATE), jnp.float32)
    y_ssd, ssm_new, conv_new = ssd_branch(z, xbc, dt_raw, conv0, h0, valid, BLOCK, lp)
    q, k, v = attn_prep(q, k, v, pos, lp)
    y_att = swa_prompt(q, k, v, pos, lp['sinks'])
    y = merge_out(y_ssd[:, META_PAD:], y_att[:, META_PAD:], lp)
    return y, ssm_new, conv_new, k[:, META_PAD:META_PAD + N_META], v[:, META_PAD:META_PAD + N_META], k[:, Lp - WINDOW:], v[:, Lp - WINDOW:]


def mixer_sample(h, st_ssm, st_conv, c_meta_k, c_meta_v, c_win_k, c_win_v, lp):
    bsz, t = h.shape[:2]
    proj = h @ lp['w_in']
    pos = PAST_LEN + jnp.arange(t, dtype=jnp.int32)
    z, xbc, dt_raw, q, k, v = split_in(proj)
    y_ssd, ssm_new, conv_new = ssd_branch(z, xbc, dt_raw, st_conv.astype(proj.dtype), st_ssm, None, t, lp)
    q, k, v = attn_prep(q, k, v, pos, lp)
    y_att, k_win_new, v_win_new = swa_sample(q, k, v, c_meta_k, c_meta_v, c_win_k, c_win_v, lp['sinks'])
    return merge_out(y_ssd, y_att, lp), ssm_new, conv_new, k_win_new, v_win_new


def swiglu(h, wg, wu, wd):
    return (jax.nn.silu(h @ wg) * (h @ wu)) @ wd


def moe_ffn(h, w_router, wg, wu, wd):
    shp = h.shape
    hf = h.reshape(-1, D_MODEL)
    n = hf.shape[0]
    nr = n * TOP_K
    logits = (hf @ w_router).astype(jnp.float32)
    top_v, top_i = lax.top_k(logits, TOP_K)
    gates = jax.nn.softmax(top_v, axis=-1)
    flat_e = top_i.reshape(-1)
    flat_t = jnp.repeat(jnp.arange(n, dtype=jnp.int32), TOP_K)
    flat_g = gates.reshape(-1)
    order = jnp.argsort(flat_e)
    se = flat_e[order]
    counts = jnp.bincount(flat_e, length=N_EXPERTS)
    starts = jnp.cumsum(counts) - counts
    padded = (counts + MOE_BLOCK - 1) // MOE_BLOCK * MOE_BLOCK
    pad_end = jnp.cumsum(padded)
    pad_start = pad_end - padded
    dest = pad_start[se] + (jnp.arange(nr, dtype=jnp.int32) - starts[se])
    n_blocks = -(-(nr + N_EXPERTS * (MOE_BLOCK - 1)) // MOE_BLOCK)
    m = n_blocks * MOE_BLOCK
    row_tok = jnp.full((m,), n, jnp.int32).at[dest].set(flat_t[order])
    row_gate = jnp.zeros((m,), jnp.float32).at[dest].set(flat_g[order])
    blk_e = jnp.minimum(jnp.searchsorted(pad_end, jnp.arange(n_blocks) * MOE_BLOCK, side='right'), N_EXPERTS - 1)
    h_ext = jnp.concatenate([hf, jnp.zeros((1, D_MODEL), hf.dtype)], axis=0)
    xb = h_ext[row_tok].reshape(n_blocks, MOE_BLOCK, D_MODEL)

    def one_block(args):
        xe, e = args
        return swiglu(xe, wg[e], wu[e], wd[e])

    yb = lax.map(one_block, (xb, blk_e)).reshape(m, D_MODEL)
    out = jax.ops.segment_sum(yb * row_gate[:, None].astype(yb.dtype), row_tok, num_segments=n + 1)[:n]
    return out.reshape(shp)


def setup_inputs(seed: int = 0) -> dict:
    key = jax.random.key(seed)
    ks = iter(jax.random.split(key, 48))
    f32 = jnp.float32

    def nrm(shape, scale):
        return jax.random.normal(next(ks), shape, f32) * scale

    def gain(shape):
        return 1.0 + nrm(shape, 0.02)

    dt0 = jnp.exp(jax.random.uniform(next(ks), (DEPTH, N_SSD_HEADS), f32, math.log(1e-3), math.log(1e-1)))
    return {
        'x_prompt': nrm((BATCH, SEQ, D_MODEL), 1.0),
        'x_sample': nrm((DEC_BATCH, DEC_SEQ, D_MODEL), 1.0),
        'state_ssm': nrm((DEPTH, DEC_BATCH, N_SSD_HEADS, SSD_HEAD_DIM, D_STATE), 0.5),
        'state_conv': nrm((DEPTH, DEC_BATCH, CONV_W - 1, CONV_DIM), 1.0),
        'cache_meta_k': nrm((DEPTH, DEC_BATCH, N_META, N_KV_HEADS, HEAD_DIM), 1.0),
        'cache_meta_v': nrm((DEPTH, DEC_BATCH, N_META, N_KV_HEADS, HEAD_DIM), 1.0),
        'cache_win_k': nrm((DEPTH, DEC_BATCH, WINDOW, N_KV_HEADS, HEAD_DIM), 1.0),
        'cache_win_v': nrm((DEPTH, DEC_BATCH, WINDOW, N_KV_HEADS, HEAD_DIM), 1.0),
        'meta_tokens': nrm((N_META, D_MODEL), 1.0),
        'norm_mix_w': gain((DEPTH, D_MODEL)),
        'w_in': nrm((DEPTH, D_MODEL, IN_DIM), D_MODEL ** -0.5),
        'conv_w': nrm((DEPTH, CONV_W, CONV_DIM), CONV_W ** -0.5),
        'conv_b': nrm((DEPTH, CONV_DIM), 0.02),
        'dt_bias': dt0 + jnp.log(-jnp.expm1(-dt0)),
        'a_log': jnp.log(jax.random.uniform(next(ks), (DEPTH, N_SSD_HEADS), f32, 1.0, 16.0)),
        'd_skip': gain((DEPTH, N_SSD_HEADS)),
        'ssd_norm_w': gain((DEPTH, D_SSD)),
        'q_norm_w': gain((DEPTH, HEAD_DIM)),
        'k_norm_w': gain((DEPTH, HEAD_DIM)),
        'sinks': nrm((DEPTH, N_Q_HEADS), 0.5),
        'attn_norm_w': gain((DEPTH, D_ATTN)),
        'w_out': nrm((DEPTH, D_MIX, D_MODEL), D_MIX ** -0.5),
        'norm_ffn_w': gain((DEPTH, D_MODEL)),
        'w_gate': nrm((N_DENSE, D_MODEL, D_FF), D_MODEL ** -0.5),
        'w_up': nrm((N_DENSE, D_MODEL, D_FF), D_MODEL ** -0.5),
        'w_down': nrm((N_DENSE, D_FF, D_MODEL), D_FF ** -0.5),
        'w_router': nrm((N_MOE, D_MODEL, N_EXPERTS), D_MODEL ** -0.5),
        'moe_w_gate': nrm((N_MOE, N_EXPERTS, D_MODEL, D_FF), D_MODEL ** -0.5),
        'moe_w_up': nrm((N_MOE, N_EXPERTS, D_MODEL, D_FF), D_MODEL ** -0.5),
        'moe_w_down': nrm((N_MOE, N_EXPERTS, D_FF, D_MODEL), D_FF ** -0.5),
    }


def reference(x_prompt, x_sample, state_ssm, state_conv, cache_meta_k, cache_meta_v, cache_win_k, cache_win_v,
              meta_tokens, norm_mix_w, w_in, conv_w, conv_b, dt_bias, a_log, d_skip, ssd_norm_w,
              q_norm_w, k_norm_w, sinks, attn_norm_w, w_out, norm_ffn_w, w_gate, w_up, w_down,
              w_router, moe_w_gate, moe_w_up, moe_w_down):
    bsz = x_prompt.shape[0]
    meta = jnp.broadcast_to(meta_tokens.astype(x_prompt.dtype)[None], (bsz, N_META, D_MODEL))
    xp = jnp.concatenate([meta, x_prompt], axis=1)
    xs = x_sample
    p_ssm, p_conv, p_mk, p_mv, p_wk, p_wv = [], [], [], [], [], []
    s_ssm, s_conv, s_wk, s_wv = [], [], [], []
    for l in range(DEPTH):
        lp = {'w_in': w_in[l], 'conv_w': conv_w[l], 'conv_b': conv_b[l], 'dt_bias': dt_bias[l],
              'a_log': a_log[l], 'd_skip': d_skip[l], 'ssd_norm_w': ssd_norm_w[l],
              'q_norm_w': q_norm_w[l], 'k_norm_w': k_norm_w[l], 'sinks': sinks[l],
              'attn_norm_w': attn_norm_w[l], 'w_out': w_out[l]}
        yp, ssm_p, conv_p, mk_p, mv_p, wk_p, wv_p = mixer_prompt(rmsnorm(xp, norm_mix_w[l]), lp)
        xp = xp + yp
        ys, ssm_s, conv_s, wk_s, wv_s = mixer_sample(rmsnorm(xs, norm_mix_w[l]), state_ssm[l], state_conv[l],
                                                      cache_meta_k[l], cache_meta_v[l],
                                                      cache_win_k[l], cache_win_v[l], lp)
        xs = xs + ys
        i = l // 2
        hp = rmsnorm(xp, norm_ffn_w[l])
        hs = rmsnorm(xs, norm_ffn_w[l])
        if l % 2 == 0:
            xp = xp + swiglu(hp, w_gate[i], w_up[i], w_down[i])
            xs = xs + swiglu(hs, w_gate[i], w_up[i], w_down[i])
        else:
            xp = xp + moe_ffn(hp, w_router[i], moe_w_gate[i], moe_w_up[i], moe_w_down[i])
            xs = xs + moe_ffn(hs, w_router[i], moe_w_gate[i], moe_w_up[i], moe_w_down[i])
        p_ssm.append(ssm_p); p_conv.append(conv_p); p_mk.append(mk_p); p_mv.append(mv_p)
        p_wk.append(wk_p); p_wv.append(wv_p)
        s_ssm.append(ssm_s); s_conv.append(conv_s); s_wk.append(wk_s); s_wv.append(wv_s)
    return (xp[:, N_META:], xs,
            jnp.stack(p_ssm), jnp.stack(p_conv), jnp.stack(p_mk), jnp.stack(p_mv),
            jnp.stack(p_wk), jnp.stack(p_wv),
            jnp.stack(s_ssm), jnp.stack(s_conv), jnp.stack(s_wk), jnp.stack(s_wv))
```

```python
import functools

import jax
import jax.numpy as jnp
from jax import lax
from jax.experimental import pallas as pl
from jax.experimental.pallas import tpu as pltpu

F32 = jnp.float32
BF16 = jnp.bfloat16

D_MODEL = 1024
D_SSD = 512
SSD_HEAD_DIM = 64
N_SSD_HEADS = 8
SSD_HEADS_PER_GROUP = 4
N_SSD_GROUPS = 2
D_STATE = 128
CONV_W = 4
CONV_DIM = 1024
D_ATTN = 512
HEAD_DIM = 64
N_Q_HEADS = 8
N_KV_HEADS = 2
Q_PER_KV = 4
KV_DIM = 128
WINDOW = 128
N_META = 16
D_FF = 2816
N_EXPERTS = 8
TOP_K = 2
EPS = 1e-6
NEG = -1e30
ATTN_SCALE = HEAD_DIM ** -0.5
PAST_LEN = 16384
ROPE_THETA = 10000.0

LANES = 128
SUBLANES = 8
CHUNK = 128
ROW_TILE = 512
FF_CHUNK = 256
VMEM_LIMIT = 60 * 1024 * 1024

COL_XBC = 0
COL_Z = 1024
COL_Q = 1536
COL_K = 2048
COL_V = 2176
COL_DT = 2304
PROJ_W = 2560


def _dot(a, b):
    return jnp.dot(a, b, preferred_element_type=F32)


def _dot_nt(a, b):
    return lax.dot_general(a, b, (((1,), (1,)), ((), ())), preferred_element_type=F32)


def _dot_tn(a, b):
    return lax.dot_general(a, b, (((0,), (0,)), ((), ())), preferred_element_type=F32)


def _rms(x, w):
    return x * lax.rsqrt(jnp.mean(x * x, axis=-1, keepdims=True) + EPS) * w


def _silu(x):
    return x * jax.nn.sigmoid(x)


def _split3(x):
    p1 = x.astype(BF16)
    r1 = x - p1.astype(F32)
    p2 = r1.astype(BF16)
    p3 = (r1 - p2.astype(F32)).astype(BF16)
    return p1, p2, p3


def _in_proj_kernel(x_ref, nw_ref, w_ref, o_ref):
    xn = _rms(x_ref[...], nw_ref[...]).astype(BF16)
    for j in range(PROJ_W // 512):
        o_ref[:, j * 512:(j + 1) * 512] = _dot(xn, w_ref[:, j * 512:(j + 1) * 512])


def _in_proj(x, nw, w):
    rows = x.shape[0]
    return pl.pallas_call(
        _in_proj_kernel,
        grid=(rows // ROW_TILE,),
        in_specs=[
            pl.BlockSpec((ROW_TILE, D_MODEL), lambda i: (i, 0)),
            pl.BlockSpec((1, D_MODEL), lambda i: (0, 0)),
            pl.BlockSpec((D_MODEL, PROJ_W), lambda i: (0, 0)),
        ],
        out_specs=pl.BlockSpec((ROW_TILE, PROJ_W), lambda i: (i, 0)),
        out_shape=jax.ShapeDtypeStruct((rows, PROJ_W), F32),
        compiler_params=pltpu.CompilerParams(dimension_semantics=("arbitrary",),
                                             vmem_limit_bytes=VMEM_LIMIT),
        name="in_proj",
    )(x, nw, w)


def _ssd_chunk(xbc, z, dt_raw, valid, cw_ref, cb_ref, dtb_ref, an_ref, dsk_ref, nw_ref,
               cbuf, hst, ybuf):
    q = CHUNK
    cbuf[SUBLANES:SUBLANES + q, :] = xbc
    acc = cb_ref[...]
    for k in range(CONV_W):
        off = SUBLANES - (CONV_W - 1) + k
        acc = acc + cbuf[off:off + q, :] * cw_ref[k:k + 1, :]
    cbuf[0:SUBLANES, :] = cbuf[q:q + SUBLANES, :]
    xc = _silu(acc)
    xs = xc[:, :D_SSD]
    bm = xc[:, D_SSD:D_SSD + N_SSD_GROUPS * D_STATE]
    cm = xc[:, D_SSD + N_SSD_GROUPS * D_STATE:]

    dt = jnp.where(valid, jax.nn.softplus(dt_raw + dtb_ref[...]), 0.0)
    da = dt * an_ref[...]
    row_i = lax.broadcasted_iota(jnp.int32, (q, q), 0)
    col_j = lax.broadcasted_iota(jnp.int32, (q, q), 1)
    tril = row_i >= col_j
    tril_b = jnp.where(tril, 1.0, 0.0).astype(BF16)
    p1, p2, p3 = _split3(da)
    cs = _dot(tril_b, p1) + _dot(tril_b, p2) + _dot(tril_b, p3)
    cs_t = cs.T
    last = cs[q - 1:q, :]
    ecs = jnp.exp(cs)
    dte = jnp.exp(last - cs)
    ecl = jnp.exp(last)

    for g in range(N_SSD_GROUPS):
        bg = bm[:, g * D_STATE:(g + 1) * D_STATE].astype(BF16)
        cg = cm[:, g * D_STATE:(g + 1) * D_STATE].astype(BF16)
        cb = _dot_nt(cg, bg)
        for r in range(SSD_HEADS_PER_GROUP):
            h = g * SSD_HEADS_PER_GROUP + r
            seg = cs[:, h:h + 1] - cs_t[h:h + 1, :]
            decay = jnp.where(tril, jnp.exp(jnp.where(tril, seg, 0.0)), 0.0)
            xs_h = xs[:, h * SSD_HEAD_DIM:(h + 1) * SSD_HEAD_DIM]
            xdt = xs_h * dt[:, h:h + 1]
            y_diag = _dot((decay * cb).astype(BF16), xdt.astype(BF16))
            h_prev = hst[h]
            y_off = _dot_nt(cg, h_prev.astype(BF16)) * ecs[:, h:h + 1]
            ybuf[:, h * SSD_HEAD_DIM:(h + 1) * SSD_HEAD_DIM] = y_diag + y_off + dsk_ref[:, h:h + 1] * xs_h
            xw = (xdt * dte[:, h:h + 1]).astype(BF16)
            hst[h] = h_prev * ecl[:, h:h + 1] + _dot_tn(xw, bg)

    y = ybuf[...] * _silu(z)
    gs = D_SSD // N_SSD_GROUPS
    outs = []
    for g in range(N_SSD_GROUPS):
        outs.append(_rms(y[:, g * gs:(g + 1) * gs], nw_ref[:, g * gs:(g + 1) * gs]))
    return jnp.concatenate(outs, axis=-1)


def _ssd_prompt_kernel(xbc_m, z_m, dt_m, xbc_t, z_t, dt_t, cw, cb, dtb, an, dsk, nw,
                       y_o, ym_o, hf_o, cbuf, hst, ybuf, ymbuf):
    c = pl.program_id(1)
    first = c == 0
    pad = CHUNK - N_META

    @pl.when(first)
    def _():
        cbuf[0:SUBLANES, :] = jnp.zeros((SUBLANES, CONV_DIM), F32)
        hst[...] = jnp.zeros(hst.shape, F32)

    def stage(main_ref, meta_ref):
        meta = jnp.concatenate([jnp.zeros((pad, meta_ref.shape[1]), F32), meta_ref[...]], axis=0)
        return jnp.where(first, meta, main_ref[...])

    row = lax.broadcasted_iota(jnp.int32, (CHUNK, 1), 0)
    valid = jnp.logical_or(c > 0, row >= pad)
    y = _ssd_chunk(stage(xbc_m, xbc_t), stage(z_m, z_t), stage(dt_m, dt_t), valid,
                   cw, cb, dtb, an, dsk, nw, cbuf, hst, ybuf).astype(BF16)
    y_o[...] = y

    @pl.when(first)
    def _():
        ymbuf[...] = y[pad:, :].astype(F32)

    ym_o[...] = ymbuf[...].astype(BF16)
    hf_o[...] = hst[...]


def _ssd_sample_kernel(xbc_s, z_s, dt_s, cprev, h0, cw, cb, dtb, an, dsk, nw,
                       y_o, hf_o, cbuf, hst, ybuf):
    t = xbc_s.shape[0]
    cbuf[0:SUBLANES, :] = cprev[...]
    hst[...] = h0[...]

    def stage(ref):
        return jnp.concatenate([ref[...], jnp.zeros((CHUNK - t, ref.shape[1]), F32)], axis=0)

    row = lax.broadcasted_iota(jnp.int32, (CHUNK, 1), 0)
    y = _ssd_chunk(stage(xbc_s), stage(z_s), stage(dt_s), row < t,
                   cw, cb, dtb, an, dsk, nw, cbuf, hst, ybuf)
    y_o[...] = y[0:t, :].astype(BF16)
    hf_o[...] = hst[...]


def _ssd_scratch():
    return [pltpu.VMEM((CHUNK + SUBLANES, CONV_DIM), F32),
            pltpu.VMEM((N_SSD_HEADS, SSD_HEAD_DIM, D_STATE), F32),
            pltpu.VMEM((CHUNK, D_SSD), F32)]


def _param_specs(params, nargs):
    zeros = (lambda *a: (0, 0))
    return [pl.BlockSpec(p.shape, zeros) for p in params]


def _ssd_prompt(proj, params, bsz, seq, meta_row0):
    nc = seq // CHUNK
    mb = meta_row0 // N_META

    def main(col):
        return lambda b, c: (b * nc + jnp.maximum(c - 1, 0), col)

    def meta(col):
        return lambda b, c: (mb, col)

    in_specs = [
        pl.BlockSpec((CHUNK, CONV_DIM), main(COL_XBC // CONV_DIM)),
        pl.BlockSpec((CHUNK, D_SSD), main(COL_Z // D_SSD)),
        pl.BlockSpec((CHUNK, LANES), main(COL_DT // LANES)),
        pl.BlockSpec((N_META, CONV_DIM), meta(COL_XBC // CONV_DIM)),
        pl.BlockSpec((N_META, D_SSD), meta(COL_Z // D_SSD)),
        pl.BlockSpec((N_META, LANES), meta(COL_DT // LANES)),
    ] + _param_specs(params, 2)
    return pl.pallas_call(
        _ssd_prompt_kernel,
        grid=(bsz, nc + 1),
        in_specs=in_specs,
        out_specs=[
            pl.BlockSpec((CHUNK, D_SSD), lambda b, c: (b * nc + jnp.maximum(c - 1, 0), 0)),
            pl.BlockSpec((None, N_META, D_SSD), lambda b, c: (b, 0, 0)),
            pl.BlockSpec((None, N_SSD_HEADS, SSD_HEAD_DIM, D_STATE), lambda b, c: (b, 0, 0, 0)),
        ],
        out_shape=[
            jax.ShapeDtypeStruct((bsz * seq, D_SSD), BF16),
            jax.ShapeDtypeStruct((bsz, N_META, D_SSD), BF16),
            jax.ShapeDtypeStruct((bsz, N_SSD_HEADS, SSD_HEAD_DIM, D_STATE), F32),
        ],
        scratch_shapes=_ssd_scratch() + [pltpu.VMEM((N_META, D_SSD), F32)],
        compiler_params=pltpu.CompilerParams(dimension_semantics=("arbitrary", "arbitrary"),
                                             vmem_limit_bytes=VMEM_LIMIT),
        name="ssd_prompt",
    )(proj, proj, proj, proj, proj, proj, *params)


def _ssd_sample(proj, conv_prev, h0, params, nseq, t, row0):
    rb = row0 // t

    def rows(col):
        return lambda b: (rb + b, col)

    in_specs = [
        pl.BlockSpec((t, CONV_DIM), rows(COL_XBC // CONV_DIM)),
        pl.BlockSpec((t, D_SSD), rows(COL_Z // D_SSD)),
        pl.BlockSpec((t, LANES), rows(COL_DT // LANES)),
        pl.BlockSpec((None, SUBLANES, CONV_DIM), lambda b: (b, 0, 0)),
        pl.BlockSpec((None, N_SSD_HEADS, SSD_HEAD_DIM, D_STATE), lambda b: (b, 0, 0, 0)),
    ] + _param_specs(params, 1)
    return pl.pallas_call(
        _ssd_sample_kernel,
        grid=(nseq,),
        in_specs=in_specs,
        out_specs=[
            pl.BlockSpec((t, D_SSD), lambda b: (b, 0)),
            pl.BlockSpec((None, N_SSD_HEADS, SSD_HEAD_DIM, D_STATE), lambda b: (b, 0, 0, 0)),
        ],
        out_shape=[
            jax.ShapeDtypeStruct((nseq * t, D_SSD), BF16),
            jax.ShapeDtypeStruct((nseq, N_SSD_HEADS, SSD_HEAD_DIM, D_STATE), F32),
        ],
        scratch_shapes=_ssd_scratch(),
        compiler_params=pltpu.CompilerParams(dimension_semantics=("arbitrary",),
                                             vmem_limit_bytes=VMEM_LIMIT),
        name="ssd_sample",
    )(proj, proj, proj, conv_prev, h0, *params)


def _qk_prep(x, w, cos, sin):
    lane = lax.broadcasted_iota(jnp.int32, x.shape, 1)
    lo_head = lane < HEAD_DIM
    sq = x * x
    s_lo = jnp.sum(jnp.where(lo_head, sq, 0.0), axis=-1, keepdims=True)
    s_all = jnp.sum(sq, axis=-1, keepdims=True)
    ms = jnp.where(lo_head, s_lo, s_all - s_lo) * (1.0 / HEAD_DIM)
    xn = x * lax.rsqrt(ms + EPS) * w
    half = HEAD_DIM // 2
    first_half = (lane % HEAD_DIM) < half
    partner = jnp.where(first_half, pltpu.roll(xn, LANES - half, 1), pltpu.roll(xn, half, 1))
    return xn * cos + partner * sin


def _attn_block(q, kp, v, cos, sin, qnw_ref, snk_ref, anw_ref,
                kmeta, vmeta, kprev, vprev, meta_mask, prev_mask, cur_mask, obuf):
    kc_b = kp.astype(BF16)
    vc_b = v.astype(BF16)
    km_b = kmeta.astype(BF16)
    vm_b = vmeta.astype(BF16)
    kp_b = kprev.astype(BF16)
    vp_b = vprev.astype(BF16)
    for g in range(D_ATTN // LANES):
        qg = _qk_prep(q[:, g * LANES:(g + 1) * LANES], qnw_ref[:, g * LANES:(g + 1) * LANES], cos, sin)
        qg = (qg * ATTN_SCALE).astype(BF16)
        for hh in range(LANES // HEAD_DIM):
            h = g * (LANES // HEAD_DIM) + hh
            kv = h // Q_PER_KV
            sl = slice(kv * HEAD_DIM, (kv + 1) * HEAD_DIM)
            qh = qg[:, hh * HEAD_DIM:(hh + 1) * HEAD_DIM]
            s_m = jnp.where(meta_mask, _dot_nt(qh, km_b[:, sl]), NEG)
            s_p = jnp.where(prev_mask, _dot_nt(qh, kp_b[:, sl]), NEG)
            s_c = jnp.where(cur_mask, _dot_nt(qh, kc_b[:, sl]), NEG)
            sk = snk_ref[:, h:h + 1]
            m = jnp.maximum(jnp.maximum(jnp.max(s_m, axis=-1, keepdims=True),
                                        jnp.max(s_p, axis=-1, keepdims=True)),
                            jnp.maximum(jnp.max(s_c, axis=-1, keepdims=True), sk))
            e_m = jnp.exp(s_m - m)
            e_p = jnp.exp(s_p - m)
            e_c = jnp.exp(s_c - m)
            den = (jnp.sum(e_m, axis=-1, keepdims=True) + jnp.sum(e_p, axis=-1, keepdims=True)
                   + jnp.sum(e_c, axis=-1, keepdims=True) + jnp.exp(sk - m))
            o = (_dot((e_m / den).astype(BF16), vm_b[:, sl])
                 + _dot((e_p / den).astype(BF16), vp_b[:, sl])
                 + _dot((e_c / den).astype(BF16), vc_b[:, sl]))
            obuf[:, h * HEAD_DIM:(h + 1) * HEAD_DIM] = o
    return _rms(obuf[...], anw_ref[...])


def _attn_prompt_kernel(q_m, k_m, v_m, q_t, k_t, v_t, cos, sin, qnw, knw, snk, anw,
                        o_o, om_o, kp_o, kpm_o, kmeta, vmeta, kprev, vprev, obuf, ombuf):
    c = pl.program_id(1)
    first = c == 0
    pad = CHUNK - N_META

    def stage(main_ref, meta_ref):
        meta = jnp.concatenate([jnp.zeros((pad, meta_ref.shape[1]), F32), meta_ref[...]], axis=0)
        return jnp.where(first, meta, main_ref[...])

    v = stage(v_m, v_t)
    kp = _qk_prep(stage(k_m, k_t), knw[...], cos[...], sin[...])

    @pl.when(first)
    def _():
        kmeta[...] = kp[pad:, :]
        vmeta[...] = v[pad:, :]
        kprev[...] = jnp.zeros(kprev.shape, F32)
        vprev[...] = jnp.zeros(vprev.shape, F32)

    i = lax.broadcasted_iota(jnp.int32, (CHUNK, 1), 0)
    jm = lax.broadcasted_iota(jnp.int32, (1, N_META), 1)
    j = lax.broadcasted_iota(jnp.int32, (1, CHUNK), 1)
    meta_mask = jnp.logical_or(c > 0, jm <= i - pad)
    prev_mask = jnp.logical_and(c >= 2, j > i)
    cur_mask = jnp.logical_and(c >= 1, j <= i)

    o = _attn_block(stage(q_m, q_t), kp, v, cos[...], sin[...], qnw, snk, anw,
                    kmeta[...], vmeta[...], kprev[...], vprev[...],
                    meta_mask, prev_mask, cur_mask, obuf)
    o = o.astype(BF16)
    o_o[...] = o
    kp_o[...] = kp
    kprev[...] = kp
    vprev[...] = v

    @pl.when(first)
    def _():
        ombuf[...] = o[pad:, :].astype(F32)

    om_o[...] = ombuf[...].astype(BF16)
    kpm_o[...] = kmeta[...]


def _attn_sample_kernel(q_s, k_s, v_s, mk, mv, wk, wv, cos, sin, qnw, knw, snk, anw,
                        o_o, kp_o, obuf):
    t = q_s.shape[0]

    def stage(ref):
        return jnp.concatenate([ref[...], jnp.zeros((CHUNK - t, ref.shape[1]), F32)], axis=0)

    i = lax.broadcasted_iota(jnp.int32, (CHUNK, 1), 0)
    jm = lax.broadcasted_iota(jnp.int32, (1, N_META), 1)
    j = lax.broadcasted_iota(jnp.int32, (1, CHUNK), 1)
    meta_mask = jm <= i + N_META
    prev_mask = j > i
    cur_mask = j <= i
    kp = _qk_prep(stage(k_s), knw[...], cos[...], sin[...])
    o = _attn_block(stage(q_s), kp, stage(v_s), cos[...], sin[...], qnw, snk, anw,
                    mk[...], mv[...], wk[...], wv[...], meta_mask, prev_mask, cur_mask, obuf)
    o_o[...] = o[0:t, :].astype(BF16)
    kp_o[...] = kp[0:t, :]


def _attn_prompt(proj, cos, sin, params, bsz, seq, meta_row0):
    nc = seq // CHUNK
    mb = meta_row0 // N_META

    def main(col):
        return lambda b, c: (b * nc + jnp.maximum(c - 1, 0), col)

    def meta(col):
        return lambda b, c: (mb, col)

    in_specs = [
        pl.BlockSpec((CHUNK, D_ATTN), main(COL_Q // D_ATTN)),
        pl.BlockSpec((CHUNK, KV_DIM), main(COL_K // KV_DIM)),
        pl.BlockSpec((CHUNK, KV_DIM), main(COL_V // KV_DIM)),
        pl.BlockSpec((N_META, D_ATTN), meta(COL_Q // D_ATTN)),
        pl.BlockSpec((N_META, KV_DIM), meta(COL_K // KV_DIM)),
        pl.BlockSpec((N_META, KV_DIM), meta(COL_V // KV_DIM)),
        pl.BlockSpec((CHUNK, LANES), lambda b, c: (c, 0)),
        pl.BlockSpec((CHUNK, LANES), lambda b, c: (c, 0)),
    ] + _param_specs(params, 2)
    return pl.pallas_call(
        _attn_prompt_kernel,
        grid=(bsz, nc + 1),
        in_specs=in_specs,
        out_specs=[
            pl.BlockSpec((CHUNK, D_ATTN), lambda b, c: (b * nc + jnp.maximum(c - 1, 0), 0)),
            pl.BlockSpec((None, N_META, D_ATTN), lambda b, c: (b, 0, 0)),
            pl.BlockSpec((CHUNK, KV_DIM), lambda b, c: (b * nc + jnp.maximum(c - 1, 0), 0)),
            pl.BlockSpec((None, N_META, KV_DIM), lambda b, c: (b, 0, 0)),
        ],
        out_shape=[
            jax.ShapeDtypeStruct((bsz * seq, D_ATTN), BF16),
            jax.ShapeDtypeStruct((bsz, N_META, D_ATTN), BF16),
            jax.ShapeDtypeStruct((bsz * seq, KV_DIM), F32),
            jax.ShapeDtypeStruct((bsz, N_META, KV_DIM), F32),
        ],
        scratch_shapes=[
            pltpu.VMEM((N_META, KV_DIM), F32), pltpu.VMEM((N_META, KV_DIM), F32),
            pltpu.VMEM((CHUNK, KV_DIM), F32), pltpu.VMEM((CHUNK, KV_DIM), F32),
            pltpu.VMEM((CHUNK, D_ATTN), F32), pltpu.VMEM((N_META, D_ATTN), F32),
        ],
        compiler_params=pltpu.CompilerParams(dimension_semantics=("arbitrary", "arbitrary"),
                                             vmem_limit_bytes=VMEM_LIMIT),
        name="attn_prompt",
    )(proj, proj, proj, proj, proj, proj, cos, sin, *params)


def _attn_sample(proj, mk, mv, wk, wv, cos, sin, params, nseq, t, row0):
    rb = row0 // t

    def rows(col):
        return lambda b: (rb + b, col)

    in_specs = [
        pl.BlockSpec((t, D_ATTN), rows(COL_Q // D_ATTN)),
        pl.BlockSpec((t, KV_DIM), rows(COL_K // KV_DIM)),
        pl.BlockSpec((t, KV_DIM), rows(COL_V // KV_DIM)),
        pl.BlockSpec((None, N_META, KV_DIM), lambda b: (b, 0, 0)),
        pl.BlockSpec((None, N_META, KV_DIM), lambda b: (b, 0, 0)),
        pl.BlockSpec((None, WINDOW, KV_DIM), lambda b: (b, 0, 0)),
        pl.BlockSpec((None, WINDOW, KV_DIM), lambda b: (b, 0, 0)),
        pl.BlockSpec((CHUNK, LANES), lambda b: (0, 0)),
        pl.BlockSpec((CHUNK, LANES), lambda b: (0, 0)),
    ] + _param_specs(params, 1)
    return pl.pallas_call(
        _attn_sample_kernel,
        grid=(nseq,),
        in_specs=in_specs,
        out_specs=[
            pl.BlockSpec((t, D_ATTN), lambda b: (b, 0)),
            pl.BlockSpec((t, KV_DIM), lambda b: (b, 0)),
        ],
        out_shape=[
            jax.ShapeDtypeStruct((nseq * t, D_ATTN), BF16),
            jax.ShapeDtypeStruct((nseq * t, KV_DIM), F32),
        ],
        scratch_shapes=[pltpu.VMEM((CHUNK, D_ATTN), F32)],
        compiler_params=pltpu.CompilerParams(dimension_semantics=("arbitrary",),
                                             vmem_limit_bytes=VMEM_LIMIT),
        name="attn_sample",
    )(proj, proj, proj, mk, mv, wk, wv, cos, sin, *params)


def _mix_out(x_ref, ys_ref, ya_ref, wo_ref):
    return (x_ref[...] + _dot(ys_ref[...], wo_ref[0:D_SSD, :])
            + _dot(ya_ref[...], wo_ref[D_SSD:D_SSD + D_ATTN, :]))


def _swiglu_acc(hn, wg_ref, wu_ref, wd_ref, acc_ref):
    for j in range(D_FF // FF_CHUNK):
        cols = slice(j * FF_CHUNK, (j + 1) * FF_CHUNK)
        a = (_silu(_dot(hn, wg_ref[:, cols])) * _dot(hn, wu_ref[:, cols])).astype(BF16)
        acc_ref[...] += _dot(a, wd_ref[cols, :])


def _out_ffn_kernel(x_ref, ys_ref, ya_ref, wo_ref, nw_ref, wg_ref, wu_ref, wd_ref, o_ref):
    xm = _mix_out(x_ref, ys_ref, ya_ref, wo_ref)
    o_ref[...] = xm
    _swiglu_acc(_rms(xm, nw_ref[...]).astype(BF16), wg_ref, wu_ref, wd_ref, o_ref)


def _resident(shape):
    nd = len(shape)
    return pl.BlockSpec(shape, lambda *a: (0,) * nd, pipeline_mode=pl.Buffered(1))


def _out_ffn(x, ys, ya, wo, nw, wg, wu, wd):
    rows = x.shape[0]
    return pl.pallas_call(
        _out_ffn_kernel,
        grid=(rows // ROW_TILE,),
        in_specs=[
            pl.BlockSpec((ROW_TILE, D_MODEL), lambda i: (i, 0)),
            pl.BlockSpec((ROW_TILE, D_SSD), lambda i: (i, 0)),
            pl.BlockSpec((ROW_TILE, D_ATTN), lambda i: (i, 0)),
            _resident(wo.shape), _resident(nw.shape),
            _resident(wg.shape), _resident(wu.shape), _resident(wd.shape),
        ],
        out_specs=pl.BlockSpec((ROW_TILE, D_MODEL), lambda i: (i, 0)),
        out_shape=jax.ShapeDtypeStruct((rows, D_MODEL), F32),
        compiler_params=pltpu.CompilerParams(dimension_semantics=("arbitrary",),
                                             vmem_limit_bytes=VMEM_LIMIT),
        name="out_ffn",
    )(x, ys, ya, wo, nw, wg, wu, wd)


def _out_router_kernel(x_ref, ys_ref, ya_ref, wo_ref, nw_ref, wr_hi_ref, wr_lo_ref,
                       xm_o, hn_o, rt_o):
    xm = _mix_out(x_ref, ys_ref, ya_ref, wo_ref)
    xm_o[...] = xm
    hn = _rms(xm, nw_ref[...])
    hi = hn.astype(BF16)
    hn_o[...] = hi
    lo = (hn - hi.astype(F32)).astype(BF16)
    logits = _dot(hi, wr_hi_ref[...]) + _dot(lo, wr_hi_ref[...]) + _dot(hi, wr_lo_ref[...])
    lane = lax.broadcasted_iota(jnp.int32, logits.shape, 1)
    logits = jnp.where(lane < N_EXPERTS, logits, -jnp.inf)
    v1 = jnp.max(logits, axis=-1, keepdims=True)
    i1 = jnp.min(jnp.where(logits == v1, lane, LANES), axis=-1, keepdims=True)
    rest = jnp.where(lane == i1, -jnp.inf, logits)
    v2 = jnp.max(rest, axis=-1, keepdims=True)
    i2 = jnp.min(jnp.where(rest == v2, lane, LANES), axis=-1, keepdims=True)
    e2 = jnp.exp(v2 - v1)
    g1 = 1.0 / (1.0 + e2)
    g2 = e2 / (1.0 + e2)
    rt_o[...] = jnp.where(lane == 0, i1.astype(F32),
                          jnp.where(lane == 1, i2.astype(F32),
                                    jnp.where(lane == 2, g1, jnp.where(lane == 3, g2, 0.0))))


def _out_router(x, ys, ya, wo, nw, wr_hi, wr_lo):
    rows = x.shape[0]
    return pl.pallas_call(
        _out_router_kernel,
        grid=(rows // ROW_TILE,),
        in_specs=[
            pl.BlockSpec((ROW_TILE, D_MODEL), lambda i: (i, 0)),
            pl.BlockSpec((ROW_TILE, D_SSD), lambda i: (i, 0)),
            pl.BlockSpec((ROW_TILE, D_ATTN), lambda i: (i, 0)),
            _resident(wo.shape), _resident(nw.shape), _resident(wr_hi.shape), _resident(wr_lo.shape),
        ],
        out_specs=[
            pl.BlockSpec((ROW_TILE, D_MODEL), lambda i: (i, 0)),
            pl.BlockSpec((ROW_TILE, D_MODEL), lambda i: (i, 0)),
            pl.BlockSpec((ROW_TILE, LANES), lambda i: (i, 0)),
        ],
        out_shape=[
            jax.ShapeDtypeStruct((rows, D_MODEL), F32),
            jax.ShapeDtypeStruct((rows, D_MODEL), BF16),
            jax.ShapeDtypeStruct((rows, LANES), F32),
        ],
        compiler_params=pltpu.CompilerParams(dimension_semantics=("arbitrary",),
                                             vmem_limit_bytes=VMEM_LIMIT),
        name="out_router",
    )(x, ys, ya, wo, nw, wr_hi, wr_lo)


def _moe_kernel(te_ref, nu_ref, x_ref, wg_ref, wu_ref, wd_ref, o_ref):
    i = pl.program_id(0)
    o_ref[...] = jnp.zeros(o_ref.shape, F32)

    @pl.when(i < nu_ref[0])
    def _():
        _swiglu_acc(x_ref[...], wg_ref, wu_ref, wd_ref, o_ref)


def _moe_experts(tile_e, n_used, xs, wg, wu, wd):
    rows = xs.shape[0]
    grid_spec = pltpu.PrefetchScalarGridSpec(
        num_scalar_prefetch=2,
        grid=(rows // ROW_TILE,),
        in_specs=[
            pl.BlockSpec((ROW_TILE, D_MODEL), lambda i, te, nu: (i, 0)),
            pl.BlockSpec((None, D_MODEL, D_FF), lambda i, te, nu: (te[i], 0, 0)),
            pl.BlockSpec((None, D_MODEL, D_FF), lambda i, te, nu: (te[i], 0, 0)),
            pl.BlockSpec((None, D_FF, D_MODEL), lambda i, te, nu: (te[i], 0, 0)),
        ],
        out_specs=pl.BlockSpec((ROW_TILE, D_MODEL), lambda i, te, nu: (i, 0)),
    )
    return pl.pallas_call(
        _moe_kernel,
        grid_spec=grid_spec,
        out_shape=jax.ShapeDtypeStruct((rows, D_MODEL), F32),
        compiler_params=pltpu.CompilerParams(dimension_semantics=("arbitrary",),
                                             vmem_limit_bytes=VMEM_LIMIT),
        name="moe_experts",
    )(tile_e, n_used, xs, wg, wu, wd)


def _moe_layer(xm, hn, route, n_tok, wg, wu, wd):
    n_assign = n_tok * TOP_K
    n_tiles = -(-(n_assign + N_EXPERTS * (ROW_TILE - 1)) // ROW_TILE)
    flat_e = route[:n_tok, 0:TOP_K].astype(jnp.int32).reshape(-1)
    gates = route[:n_tok, TOP_K:2 * TOP_K]
    flat_t = jnp.repeat(jnp.arange(n_tok, dtype=jnp.int32), TOP_K)
    onehot = (flat_e[:, None] == jnp.arange(N_EXPERTS, dtype=jnp.int32)[None, :]).astype(jnp.int32)
    csum = jnp.cumsum(onehot, axis=0)
    counts = csum[-1]
    rank = jnp.sum(csum * onehot, axis=1) - 1
    padded = (counts + ROW_TILE - 1) // ROW_TILE * ROW_TILE
    pad_end = jnp.cumsum(padded)
    pad_start = pad_end - padded
    dest = pad_start[flat_e] + rank
    row_tok = jnp.zeros((n_tiles * ROW_TILE,), jnp.int32).at[dest].set(flat_t)
    tile_e = jnp.minimum(jnp.searchsorted(pad_end, jnp.arange(n_tiles, dtype=jnp.int32) * ROW_TILE,
                                          side='right'), N_EXPERTS - 1).astype(jnp.int32)
    n_used = (pad_end[-1:] // ROW_TILE).astype(jnp.int32)
    xs = jnp.take(hn, row_tok, axis=0)
    yb = _moe_experts(tile_e, n_used, xs, wg, wu, wd)
    yk = jnp.take(yb, dest, axis=0).reshape(n_tok, TOP_K, D_MODEL)
    upd = xm[:n_tok] + jnp.sum(yk * gates[:, :, None], axis=1)
    return jnp.concatenate([upd, xm[n_tok:]], axis=0)


def _rope_tables(pos):
    half = HEAD_DIM // 2
    inv_freq = ROPE_THETA ** (-jnp.arange(half, dtype=F32) / half)
    ang = pos.astype(F32)[:, None] * inv_freq[None, :]
    cos = jnp.cos(ang)
    sin = jnp.sin(ang)
    reps = LANES // HEAD_DIM
    return (jnp.tile(jnp.concatenate([cos, cos], axis=-1), (1, reps)),
            jnp.tile(jnp.concatenate([-sin, sin], axis=-1), (1, reps)))


def _pad_lanes(v, width=LANES):
    v = v.astype(F32).reshape(1, -1)
    return jnp.pad(v, ((0, 0), (0, width - v.shape[1])))


def kernel(x_prompt, x_sample, state_ssm, state_conv, cache_meta_k, cache_meta_v, cache_win_k, cache_win_v, meta_tokens, norm_mix_w, w_in, conv_w, conv_b, dt_bias, a_log, d_skip, ssd_norm_w, q_norm_w, k_norm_w, sinks, attn_norm_w, w_out, norm_ffn_w, w_gate, w_up, w_down, w_router, moe_w_gate, moe_w_up, moe_w_down):
    bsz, seq, _ = x_prompt.shape
    nseq, t_s, _ = x_sample.shape
    depth = w_in.shape[0]
    assert seq % CHUNK == 0 and t_s == SUBLANES
    r_main = bsz * seq
    r_samp = nseq * t_s
    n_tok = r_main + r_samp + N_META
    rows = -(-n_tok // ROW_TILE) * ROW_TILE
    meta_row0 = r_main + r_samp

    x = jnp.concatenate([x_prompt.reshape(r_main, D_MODEL), x_sample.reshape(r_samp, D_MODEL),
                         meta_tokens.astype(F32), jnp.zeros((rows - n_tok, D_MODEL), F32)], axis=0)

    pos_p = jnp.arange(seq + CHUNK, dtype=jnp.int32) - (CHUNK - N_META)
    cos_p, sin_p = _rope_tables(pos_p)
    pos_s = PAST_LEN + jnp.arange(CHUNK, dtype=jnp.int32)
    cos_s, sin_s = _rope_tables(pos_s)

    o_z, o_xbc, o_dt, o_q, o_k, o_v = 0, 512, 1536, 1544, 2056, 2184

    outs = {k: [] for k in ('p_ssm', 'p_conv', 'p_mk', 'p_mv', 'p_wk', 'p_wv', 's_ssm', 's_conv', 's_wk', 's_wv')}
    for l in range(depth):
        wl = w_in[l]
        w_re = jnp.concatenate([
            wl[:, o_xbc:o_xbc + CONV_DIM], wl[:, o_z:o_z + D_SSD], wl[:, o_q:o_q + D_ATTN],
            wl[:, o_k:o_k + KV_DIM], wl[:, o_v:o_v + KV_DIM], wl[:, o_dt:o_dt + N_SSD_HEADS],
            jnp.zeros((D_MODEL, PROJ_W - COL_DT - N_SSD_HEADS), wl.dtype)], axis=1).astype(BF16)
        proj = _in_proj(x, norm_mix_w[l].reshape(1, D_MODEL), w_re)

        ssd_params = (conv_w[l].astype(F32), conv_b[l].reshape(1, CONV_DIM).astype(F32),
                      _pad_lanes(dt_bias[l]), _pad_lanes(-jnp.exp(a_log[l].astype(F32))),
                      _pad_lanes(d_skip[l]), ssd_norm_w[l].reshape(1, D_SSD).astype(F32))
        ys_p, ys_m, ssm_p = _ssd_prompt(proj, ssd_params, bsz, seq, meta_row0)
        conv_prev = jnp.pad(state_conv[l].astype(F32), ((0, 0), (SUBLANES - (CONV_W - 1), 0), (0, 0)))
        ys_s, ssm_s = _ssd_sample(proj, conv_prev, state_ssm[l].astype(F32), ssd_params, nseq, t_s, r_main)

        attn_params = (jnp.tile(q_norm_w[l].astype(F32), N_Q_HEADS).reshape(1, D_ATTN),
                       jnp.tile(k_norm_w[l].astype(F32), N_KV_HEADS).reshape(1, KV_DIM),
                       _pad_lanes(sinks[l]), attn_norm_w[l].reshape(1, D_ATTN).astype(F32))
        ya_p, ya_m, kp_p, kp_m = _attn_prompt(proj, cos_p, sin_p, attn_params, bsz, seq, meta_row0)
        ya_s, kp_s = _attn_sample(
            proj, cache_meta_k[l].reshape(nseq, N_META, KV_DIM).astype(F32),
            cache_meta_v[l].reshape(nseq, N_META, KV_DIM).astype(F32),
            cache_win_k[l].reshape(nseq, WINDOW, KV_DIM).astype(F32),
            cache_win_v[l].reshape(nseq, WINDOW, KV_DIM).astype(F32),
            cos_s, sin_s, attn_params, nseq, t_s, r_main)

        pad_rows = rows - n_tok
        ys = jnp.concatenate([ys_p, ys_s, ys_m[0], jnp.zeros((pad_rows, D_SSD), BF16)], axis=0)
        ya = jnp.concatenate([ya_p, ya_s, ya_m[0], jnp.zeros((pad_rows, D_ATTN), BF16)], axis=0)

        wo = w_out[l].astype(BF16)
        nfw = norm_ffn_w[l].reshape(1, D_MODEL).astype(F32)
        i = l // 2
        if l % 2 == 0:
            x = _out_ffn(x, ys, ya, wo, nfw, w_gate[i].astype(BF16), w_up[i].astype(BF16),
                         w_down[i].astype(BF16))
        else:
            wr = jnp.pad(w_router[i].astype(F32), ((0, 0), (0, LANES - N_EXPERTS)))
            wr_hi = wr.astype(BF16)
            wr_lo = (wr - wr_hi.astype(F32)).astype(BF16)
            xm, hn, route = _out_router(x, ys, ya, wo, nfw, wr_hi, wr_lo)
            x = _moe_layer(xm, hn, route, n_tok, moe_w_gate[i].astype(BF16), moe_w_up[i].astype(BF16),
                           moe_w_down[i].astype(BF16))

        xbc_p = proj[:r_main, COL_XBC:COL_XBC + CONV_DIM].reshape(bsz, seq, CONV_DIM)
        v_p = proj[:r_main, COL_V:COL_V + KV_DIM].reshape(bsz, seq, KV_DIM)
        kp_p3 = kp_p.reshape(bsz, seq, KV_DIM)
        v_meta = proj[meta_row0:meta_row0 + N_META, COL_V:COL_V + KV_DIM]
        outs['p_ssm'].append(ssm_p)
        outs['p_conv'].append(xbc_p[:, seq - (CONV_W - 1):])
        outs['p_mk'].append(jnp.broadcast_to(kp_m[0].reshape(1, N_META, N_KV_HEADS, HEAD_DIM),
                                             (bsz, N_META, N_KV_HEADS, HEAD_DIM)))
        outs['p_mv'].append(jnp.broadcast_to(v_meta.reshape(1, N_META, N_KV_HEADS, HEAD_DIM),
                                             (bsz, N_META, N_KV_HEADS, HEAD_DIM)))
        outs['p_wk'].append(kp_p3[:, seq - WINDOW:].reshape(bsz, WINDOW, N_KV_HEADS, HEAD_DIM))
        outs['p_wv'].append(v_p[:, seq - WINDOW:].reshape(bsz, WINDOW, N_KV_HEADS, HEAD_DIM))
        xbc_s = proj[r_main:r_main + r_samp, COL_XBC:COL_XBC + CONV_DIM].reshape(nseq, t_s, CONV_DIM)
        v_s = proj[r_main:r_main + r_samp, COL_V:COL_V + KV_DIM].reshape(nseq, t_s, N_KV_HEADS, HEAD_DIM)
        outs['s_ssm'].append(ssm_s)
        outs['s_conv'].append(jnp.concatenate([state_conv[l].astype(F32), xbc_s], axis=1)[:, t_s:])
        outs['s_wk'].append(jnp.concatenate([cache_win_k[l].astype(F32),
                                             kp_s.reshape(nseq, t_s, N_KV_HEADS, HEAD_DIM)], axis=1)[:, t_s:])
        outs['s_wv'].append(jnp.concatenate([cache_win_v[l].astype(F32), v_s], axis=1)[:, t_s:])

    y_prompt = x[:r_main].reshape(bsz, seq, D_MODEL)
    y_sample = x[r_main:r_main + r_samp].reshape(nseq, t_s, D_MODEL)
    st = lambda k: jnp.stack(outs[k])
    return (y_prompt, y_sample, st('p_ssm'), st('p_conv'), st('p_mk'), st('p_mv'), st('p_wk'), st('p_wv'),
            st('s_ssm'), st('s_conv'), st('s_wk'), st('s_wv'))
```

```python
import functools

import jax
import jax.numpy as jnp
from jax import lax
from jax.experimental import pallas as pl
from jax.experimental.pallas import tpu as pltpu

F32 = jnp.float32
BF16 = jnp.bfloat16

D_MODEL = 1024
D_SSD = 512
SSD_HEAD_DIM = 64
N_SSD_HEADS = 8
SSD_HEADS_PER_GROUP = 4
N_SSD_GROUPS = 2
D_STATE = 128
CONV_W = 4
CONV_DIM = 1024
D_ATTN = 512
HEAD_DIM = 64
N_Q_HEADS = 8
N_KV_HEADS = 2
Q_PER_KV = 4
KV_DIM = 128
WINDOW = 128
N_META = 16
D_FF = 2816
N_EXPERTS = 8
TOP_K = 2
EPS = 1e-6
NEG = -1e30
ATTN_SCALE = HEAD_DIM ** -0.5
PAST_LEN = 16384
ROPE_THETA = 10000.0

LANES = 128
SUBLANES = 8
CHUNK = 128
ROW_TILE = 512
FF_CHUNK = 256
VMEM_LIMIT = 60 * 1024 * 1024

COL_XBC = 0
COL_Z = 1024
COL_Q = 1536
COL_K = 2048
COL_V = 2176
COL_DT = 2304
PROJ_W = 2560


def _dot(a, b):
    return jnp.dot(a, b, preferred_element_type=F32)


def _dot_nt(a, b):
    return lax.dot_general(a, b, (((1,), (1,)), ((), ())), preferred_element_type=F32)


def _dot_tn(a, b):
    return lax.dot_general(a, b, (((0,), (0,)), ((), ())), preferred_element_type=F32)


def _rms(x, w):
    return x * lax.rsqrt(jnp.mean(x * x, axis=-1, keepdims=True) + EPS) * w


def _silu(x):
    return x * jax.nn.sigmoid(x)


def _split3(x):
    p1 = x.astype(BF16)
    r1 = x - p1.astype(F32)
    p2 = r1.astype(BF16)
    p3 = (r1 - p2.astype(F32)).astype(BF16)
    return p1, p2, p3


def _in_proj_kernel(x_ref, nw_ref, w_ref, o_ref):
    xn = _rms(x_ref[...], nw_ref[...]).astype(BF16)
    for j in range(PROJ_W // 512):
        o_ref[:, j * 512:(j + 1) * 512] = _dot(xn, w_ref[:, j * 512:(j + 1) * 512])


def _in_proj(x, nw, w):
    rows = x.shape[0]
    return pl.pallas_call(
        _in_proj_kernel,
        grid=(rows // ROW_TILE,),
        in_specs=[
            pl.BlockSpec((ROW_TILE, D_MODEL), lambda i: (i, 0)),
            pl.BlockSpec((1, D_MODEL), lambda i: (0, 0)),
            pl.BlockSpec((D_MODEL, PROJ_W), lambda i: (0, 0)),
        ],
        out_specs=pl.BlockSpec((ROW_TILE, PROJ_W), lambda i: (i, 0)),
        out_shape=jax.ShapeDtypeStruct((rows, PROJ_W), F32),
        compiler_params=pltpu.CompilerParams(dimension_semantics=("arbitrary",),
                                             vmem_limit_bytes=VMEM_LIMIT),
        name="in_proj",
    )(x, nw, w)


def _ssd_chunk(xbc, z, dt_raw, valid, cw_ref, cb_ref, dtb_ref, an_ref, dsk_ref, nw_ref,
               cbuf, hst, ybuf):
    q = CHUNK
    cbuf[SUBLANES:SUBLANES + q, :] = xbc
    acc = cb_ref[...]
    for k in range(CONV_W):
        off = SUBLANES - (CONV_W - 1) + k
        acc = acc + cbuf[off:off + q, :] * cw_ref[k:k + 1, :]
    cbuf[0:SUBLANES, :] = cbuf[q:q + SUBLANES, :]
    xc = _silu(acc)
    xs = xc[:, :D_SSD]
    bm = xc[:, D_SSD:D_SSD + N_SSD_GROUPS * D_STATE]
    cm = xc[:, D_SSD + N_SSD_GROUPS * D_STATE:]

    dt = jnp.where(valid, jax.nn.softplus(dt_raw + dtb_ref[...]), 0.0)
    da = dt * an_ref[...]
    row_i = lax.broadcasted_iota(jnp.int32, (q, q), 0)
    col_j = lax.broadcasted_iota(jnp.int32, (q, q), 1)
    tril = row_i >= col_j
    tril_b = jnp.where(tril, 1.0, 0.0).astype(BF16)
    p1, p2, p3 = _split3(da)
    cs = _dot(tril_b, p1) + _dot(tril_b, p2) + _dot(tril_b, p3)
    cs_t = cs.T
    last = cs[q - 1:q, :]
    ecs = jnp.exp(cs)
    dte = jnp.exp(last - cs)
    ecl = jnp.exp(last)

    for g in range(N_SSD_GROUPS):
        bg = bm[:, g * D_STATE:(g + 1) * D_STATE].astype(BF16)
        cg = cm[:, g * D_STATE:(g + 1) * D_STATE].astype(BF16)
        cb = _dot_nt(cg, bg)
        for r in range(SSD_HEADS_PER_GROUP):
            h = g * SSD_HEADS_PER_GROUP + r
            seg = cs[:, h:h + 1] - cs_t[h:h + 1, :]
            decay = jnp.where(tril, jnp.exp(jnp.where(tril, seg, 0.0)), 0.0)
            xs_h = xs[:, h * SSD_HEAD_DIM:(h + 1) * SSD_HEAD_DIM]
            xdt = xs_h * dt[:, h:h + 1]
            y_diag = _dot((decay * cb).astype(BF16), xdt.astype(BF16))
            h_prev = hst[h]
            y_off = _dot_nt(cg, h_prev.astype(BF16)) * ecs[:, h:h + 1]
            ybuf[:, h * SSD_HEAD_DIM:(h + 1) * SSD_HEAD_DIM] = y_diag + y_off + dsk_ref[:, h:h + 1] * xs_h
            xw = (xdt * dte[:, h:h + 1]).astype(BF16)
            hst[h] = h_prev * ecl[:, h:h + 1] + _dot_tn(xw, bg)

    y = ybuf[...] * _silu(z)
    gs = D_SSD // N_SSD_GROUPS
    outs = []
    for g in range(N_SSD_GROUPS):
        outs.append(_rms(y[:, g * gs:(g + 1) * gs], nw_ref[:, g * gs:(g + 1) * gs]))
    return jnp.concatenate(outs, axis=-1)


def _ssd_prompt_kernel(xbc_m, z_m, dt_m, xbc_t, z_t, dt_t, cw, cb, dtb, an, dsk, nw,
                       y_o, ym_o, hf_o, cbuf, hst, ybuf, ymbuf):
    c = pl.program_id(1)
    first = c == 0
    pad = CHUNK - N_META

    @pl.when(first)
    def _():
        cbuf[0:SUBLANES, :] = jnp.zeros((SUBLANES, CONV_DIM), F32)
        hst[...] = jnp.zeros(hst.shape, F32)

    def stage(main_ref, meta_ref):
        meta = jnp.concatenate([jnp.zeros((pad, meta_ref.shape[1]), F32), meta_ref[...]], axis=0)
        return jnp.where(first, meta, main_ref[...])

    row = lax.broadcasted_iota(jnp.int32, (CHUNK, 1), 0)
    valid = jnp.logical_or(c > 0, row >= pad)
    y = _ssd_chunk(stage(xbc_m, xbc_t), stage(z_m, z_t), stage(dt_m, dt_t), valid,
                   cw, cb, dtb, an, dsk, nw, cbuf, hst, ybuf).astype(BF16)
    y_o[...] = y

    @pl.when(first)
    def _():
        ymbuf[...] = y[pad:, :].astype(F32)

    ym_o[...] = ymbuf[...].astype(BF16)
    hf_o[...] = hst[...]


def _ssd_sample_kernel(xbc_s, z_s, dt_s, cprev, h0, cw, cb, dtb, an, dsk, nw,
                       y_o, hf_o, cbuf, hst, ybuf):
    t = xbc_s.shape[0]
    cbuf[0:SUBLANES, :] = cprev[...]
    hst[...] = h0[...]

    def stage(ref):
        return jnp.concatenate([ref[...], jnp.zeros((CHUNK - t, ref.shape[1]), F32)], axis=0)

    row = lax.broadcasted_iota(jnp.int32, (CHUNK, 1), 0)
    y = _ssd_chunk(stage(xbc_s), stage(z_s), stage(dt_s), row < t,
                   cw, cb, dtb, an, dsk, nw, cbuf, hst, ybuf)
    y_o[...] = y[0:t, :].astype(BF16)
    hf_o[...] = hst[...]


def _ssd_scratch():
    return [pltpu.VMEM((CHUNK + SUBLANES, CONV_DIM), F32),
            pltpu.VMEM((N_SSD_HEADS, SSD_HEAD_DIM, D_STATE), F32),
            pltpu.VMEM((CHUNK, D_SSD), F32)]


def _param_specs(params, nargs):
    zeros = (lambda *a: (0, 0))
    return [pl.BlockSpec(p.shape, zeros) for p in params]


def _ssd_prompt(proj, params, bsz, seq, meta_row0):
    nc = seq // CHUNK
    mb = meta_row0 // N_META

    def main(col):
        return lambda b, c: (b * nc + jnp.maximum(c - 1, 0), col)

    def meta(col):
        return lambda b, c: (mb, col)

    in_specs = [
        pl.BlockSpec((CHUNK, CONV_DIM), main(COL_XBC // CONV_DIM)),
        pl.BlockSpec((CHUNK, D_SSD), main(COL_Z // D_SSD)),
        pl.BlockSpec((CHUNK, LANES), main(COL_DT // LANES)),
        pl.BlockSpec((N_META, CONV_DIM), meta(COL_XBC // CONV_DIM)),
        pl.BlockSpec((N_META, D_SSD), meta(COL_Z // D_SSD)),
        pl.BlockSpec((N_META, LANES), meta(COL_DT // LANES)),
    ] + _param_specs(params, 2)
    return pl.pallas_call(
        _ssd_prompt_kernel,
        grid=(bsz, nc + 1),
        in_specs=in_specs,
        out_specs=[
            pl.BlockSpec((CHUNK, D_SSD), lambda b, c: (b * nc + jnp.maximum(c - 1, 0), 0)),
            pl.BlockSpec((None, N_META, D_SSD), lambda b, c: (b, 0, 0)),
            pl.BlockSpec((None, N_SSD_HEADS, SSD_HEAD_DIM, D_STATE), lambda b, c: (b, 0, 0, 0)),
        ],
        out_shape=[
            jax.ShapeDtypeStruct((bsz * seq, D_SSD), BF16),
            jax.ShapeDtypeStruct((bsz, N_META, D_SSD), BF16),
            jax.ShapeDtypeStruct((bsz, N_SSD_HEADS, SSD_HEAD_DIM, D_STATE), F32),
        ],
        scratch_shapes=_ssd_scratch() + [pltpu.VMEM((N_META, D_SSD), F32)],
        compiler_params=pltpu.CompilerParams(dimension_semantics=("arbitrary", "arbitrary"),
                                             vmem_limit_bytes=VMEM_LIMIT),
        name="ssd_prompt",
    )(proj, proj, proj, proj, proj, proj, *params)


def _ssd_sample(proj, conv_prev, h0, params, nseq, t, row0):
    rb = row0 // t

    def rows(col):
        return lambda b: (rb + b, col)

    in_specs = [
        pl.BlockSpec((t, CONV_DIM), rows(COL_XBC // CONV_DIM)),
        pl.BlockSpec((t, D_SSD), rows(COL_Z // D_SSD)),
        pl.BlockSpec((t, LANES), rows(COL_DT // LANES)),
        pl.BlockSpec((None, SUBLANES, CONV_DIM), lambda b: (b, 0, 0)),
        pl.BlockSpec((None, N_SSD_HEADS, SSD_HEAD_DIM, D_STATE), lambda b: (b, 0, 0, 0)),
    ] + _param_specs(params, 1)
    return pl.pallas_call(
        _ssd_sample_kernel,
        grid=(nseq,),
        in_specs=in_specs,
        out_specs=[
            pl.BlockSpec((t, D_SSD), lambda b: (b, 0)),
            pl.BlockSpec((None, N_SSD_HEADS, SSD_HEAD_DIM, D_STATE), lambda b: (b, 0, 0, 0)),
        ],
        out_shape=[
            jax.ShapeDtypeStruct((nseq * t, D_SSD), BF16),
            jax.ShapeDtypeStruct((nseq, N_SSD_HEADS, SSD_HEAD_DIM, D_STATE), F32),
        ],
        scratch_shapes=_ssd_scratch(),
        compiler_params=pltpu.CompilerParams(dimension_semantics=("arbitrary",),
                                             vmem_limit_bytes=VMEM_LIMIT),
        name="ssd_sample",
    )(proj, proj, proj, conv_prev, h0, *params)


def _qk_prep(x, w, cos, sin):
    lane = lax.broadcasted_iota(jnp.int32, x.shape, 1)
    lo_head = lane < HEAD_DIM
    sq = x * x
    s_lo = jnp.sum(jnp.where(lo_head, sq, 0.0), axis=-1, keepdims=True)
    s_all = jnp.sum(sq, axis=-1, keepdims=True)
    ms = jnp.where(lo_head, s_lo, s_all - s_lo) * (1.0 / HEAD_DIM)
    xn = x * lax.rsqrt(ms + EPS) * w
    half = HEAD_DIM // 2
    first_half = (lane % HEAD_DIM) < half
    partner = jnp.where(first_half, pltpu.roll(xn, LANES - half, 1), pltpu.roll(xn, half, 1))
    return xn * cos + partner * sin


def _q_stack(q, qnw_ref, cos, sin):
    groups = [_qk_prep(q[:, g * LANES:(g + 1) * LANES], qnw_ref[:, g * LANES:(g + 1) * LANES], cos, sin)
              for g in range(D_ATTN // LANES)]
    return groups


def _sink_column(snk_ref, kv, rows):
    return jnp.concatenate([jnp.broadcast_to(snk_ref[:, kv * Q_PER_KV + g:kv * Q_PER_KV + g + 1], (rows, 1))
                            for g in range(Q_PER_KV)], axis=0)


def _softmax_pv(scores, masks, sk, vals):
    s = [jnp.where(mk, sc, NEG) for sc, mk in zip(scores, masks)]
    top = s[0]
    for piece in s[1:]:
        top = jnp.maximum(top, piece)
    m = jnp.maximum(jnp.max(top, axis=-1, keepdims=True), sk)
    e = [jnp.exp(piece - m) for piece in s]
    tot = e[0]
    for piece in e[1:]:
        tot = tot + piece
    inv = 1.0 / (jnp.sum(tot, axis=-1, keepdims=True) + jnp.exp(sk - m))
    out = _dot((e[0] * inv).astype(BF16), vals[0])
    for piece, val in zip(e[1:], vals[1:]):
        out = out + _dot((piece * inv).astype(BF16), val)
    return out


def _attn_prompt_kernel(q_m, k_m, v_m, q_t, k_t, v_t, cos, sin, qnw, knw, snk, anw,
                        o_o, om_o, kp_o, kpm_o,
                        km_lo, km_hi, vm, kp_lo, kp_hi, vp, kmf, ombuf):
    c = pl.program_id(1)
    first = c == 0
    pad = CHUNK - N_META
    nst = Q_PER_KV * CHUNK

    def stage(main_ref, meta_ref):
        meta = jnp.concatenate([jnp.zeros((pad, meta_ref.shape[1]), F32), meta_ref[...]], axis=0)
        return jnp.where(first, meta, main_ref[...])

    cs, sn = cos[...], sin[...]
    v_b = stage(v_m, v_t).astype(BF16)
    kp = _qk_prep(stage(k_m, k_t), knw[...], cs, sn)
    lane = lax.broadcasted_iota(jnp.int32, (CHUNK, LANES), 1)
    k_lo = jnp.where(lane < HEAD_DIM, kp, 0.0).astype(BF16)
    k_hi = jnp.where(lane < HEAD_DIM, 0.0, kp).astype(BF16)

    @pl.when(first)
    def _():
        zpad = jnp.zeros((pad, LANES), BF16)
        km_lo[...] = jnp.concatenate([k_lo[pad:, :], zpad], axis=0)
        km_hi[...] = jnp.concatenate([k_hi[pad:, :], zpad], axis=0)
        vm[...] = jnp.concatenate([v_b[pad:, :], zpad], axis=0)
        kmf[...] = kp[pad:, :]
        kp_lo[...] = jnp.zeros(kp_lo.shape, BF16)
        kp_hi[...] = jnp.zeros(kp_hi.shape, BF16)
        vp[...] = jnp.zeros(vp.shape, BF16)

    qs = (jnp.concatenate(_q_stack(stage(q_m, q_t), qnw, cs, sn), axis=0) * ATTN_SCALE).astype(BF16)

    ri = lax.broadcasted_iota(jnp.int32, (nst, LANES), 0) & (CHUNK - 1)
    cj = lax.broadcasted_iota(jnp.int32, (nst, LANES), 1)
    meta_mask = jnp.logical_and(cj < N_META, jnp.logical_or(c > 0, cj <= ri - pad))
    prev_mask = jnp.logical_and(c >= 2, cj > ri)
    cur_mask = jnp.logical_and(c >= 1, cj <= ri)
    masks = [meta_mask, prev_mask, cur_mask]
    vals = [vm[...], vp[...], v_b]

    o_lo = _softmax_pv([_dot_nt(qs, km_lo[...]), _dot_nt(qs, kp_lo[...]), _dot_nt(qs, k_lo)],
                       masks, _sink_column(snk, 0, CHUNK), vals)
    o_hi = _softmax_pv([_dot_nt(qs, km_hi[...]), _dot_nt(qs, kp_hi[...]), _dot_nt(qs, k_hi)],
                       masks, _sink_column(snk, 1, CHUNK), vals)
    lane_st = lax.broadcasted_iota(jnp.int32, (nst, LANES), 1)
    o_st = jnp.where(lane_st < HEAD_DIM, o_lo, o_hi)
    o = jnp.concatenate([o_st[g * CHUNK:(g + 1) * CHUNK, :] for g in range(Q_PER_KV)], axis=1)
    o = _rms(o, anw[...]).astype(BF16)
    o_o[...] = o
    kp_o[...] = kp
    kp_lo[...] = k_lo
    kp_hi[...] = k_hi
    vp[...] = v_b

    @pl.when(first)
    def _():
        ombuf[...] = o[pad:, :].astype(F32)

    om_o[...] = ombuf[...].astype(BF16)
    kpm_o[...] = kmf[...]


def _attn_sample_kernel(q_s, k_s, v_s, mk, mv, wk, wv, cos, sin, qnw, knw, snk, anw,
                        o_o, kp_o, obuf):
    nsq = mk.shape[0]
    t = SUBLANES
    nst = Q_PER_KV * t
    cs, sn = cos[...], sin[...]
    kp = _qk_prep(k_s[...], knw[...], cs, sn)
    kp_o[...] = kp
    v = v_s[...]
    qg = [g * ATTN_SCALE for g in _q_stack(q_s[...], qnw, cs, sn)]

    lo = lax.broadcasted_iota(jnp.int32, (nst, LANES), 1) < HEAD_DIM
    i_q = lax.broadcasted_iota(jnp.int32, (2 * nst, LANES), 0) & (t - 1)
    cj = lax.broadcasted_iota(jnp.int32, (2 * nst, LANES), 1)
    mask_a = cj > i_q
    mask_b = jnp.logical_or(cj <= i_q, jnp.logical_and(cj >= t, cj < t + N_META))
    sk = jnp.concatenate([_sink_column(snk, kv, t) for kv in range(N_KV_HEADS)], axis=0)
    zpad = jnp.zeros((WINDOW - t - N_META, LANES), F32)
    for s in range(nsq):
        rows = slice(s * t, (s + 1) * t)
        q_st = jnp.concatenate([g[rows, :] for g in qg], axis=0)
        q2 = jnp.concatenate([jnp.where(lo, q_st, 0.0), jnp.where(lo, 0.0, q_st)], axis=0).astype(BF16)
        k_b = jnp.concatenate([kp[rows, :], mk[s], zpad], axis=0).astype(BF16)
        v_b = jnp.concatenate([v[rows, :], mv[s], zpad], axis=0).astype(BF16)
        o2 = _softmax_pv([_dot_nt(q2, wk[s].astype(BF16)), _dot_nt(q2, k_b)], [mask_a, mask_b], sk,
                         [wv[s].astype(BF16), v_b])
        o_st = jnp.where(lo, o2[0:nst, :], o2[nst:2 * nst, :])
        for g in range(Q_PER_KV):
            obuf[rows, g * LANES:(g + 1) * LANES] = o_st[g * t:(g + 1) * t, :]
    o_o[...] = _rms(obuf[...], anw[...]).astype(BF16)


def _attn_prompt(proj, cos, sin, params, bsz, seq, meta_row0):
    nc = seq // CHUNK
    mb = meta_row0 // N_META

    def main(col):
        return lambda b, c: (b * nc + jnp.maximum(c - 1, 0), col)

    def meta(col):
        return lambda b, c: (mb, col)

    in_specs = [
        pl.BlockSpec((CHUNK, D_ATTN), main(COL_Q // D_ATTN)),
        pl.BlockSpec((CHUNK, KV_DIM), main(COL_K // KV_DIM)),
        pl.BlockSpec((CHUNK, KV_DIM), main(COL_V // KV_DIM)),
        pl.BlockSpec((N_META, D_ATTN), meta(COL_Q // D_ATTN)),
        pl.BlockSpec((N_META, KV_DIM), meta(COL_K // KV_DIM)),
        pl.BlockSpec((N_META, KV_DIM), meta(COL_V // KV_DIM)),
        pl.BlockSpec((CHUNK, LANES), lambda b, c: (c, 0)),
        pl.BlockSpec((CHUNK, LANES), lambda b, c: (c, 0)),
    ] + _param_specs(params, 2)
    return pl.pallas_call(
        _attn_prompt_kernel,
        grid=(bsz, nc + 1),
        in_specs=in_specs,
        out_specs=[
            pl.BlockSpec((CHUNK, D_ATTN), lambda b, c: (b * nc + jnp.maximum(c - 1, 0), 0)),
            pl.BlockSpec((None, N_META, D_ATTN), lambda b, c: (b, 0, 0)),
            pl.BlockSpec((CHUNK, KV_DIM), lambda b, c: (b * nc + jnp.maximum(c - 1, 0), 0)),
            pl.BlockSpec((None, N_META, KV_DIM), lambda b, c: (b, 0, 0)),
        ],
        out_shape=[
            jax.ShapeDtypeStruct((bsz * seq, D_ATTN), BF16),
            jax.ShapeDtypeStruct((bsz, N_META, D_ATTN), BF16),
            jax.ShapeDtypeStruct((bsz * seq, KV_DIM), F32),
            jax.ShapeDtypeStruct((bsz, N_META, KV_DIM), F32),
        ],
        scratch_shapes=[pltpu.VMEM((CHUNK, KV_DIM), BF16) for _ in range(6)] + [
            pltpu.VMEM((N_META, KV_DIM), F32), pltpu.VMEM((N_META, D_ATTN), F32),
        ],
        compiler_params=pltpu.CompilerParams(dimension_semantics=("arbitrary", "arbitrary"),
                                             vmem_limit_bytes=VMEM_LIMIT),
        name="attn_prompt",
    )(proj, proj, proj, proj, proj, proj, cos, sin, *params)


def _attn_sample(proj, mk, mv, wk, wv, cos, sin, params, nseq, t, row0, nsq):
    blk = nsq * t
    rb = row0 // blk

    def rows(col):
        return lambda b: (rb + b, col)

    in_specs = [
        pl.BlockSpec((blk, D_ATTN), rows(COL_Q // D_ATTN)),
        pl.BlockSpec((blk, KV_DIM), rows(COL_K // KV_DIM)),
        pl.BlockSpec((blk, KV_DIM), rows(COL_V // KV_DIM)),
        pl.BlockSpec((nsq, N_META, KV_DIM), lambda b: (b, 0, 0)),
        pl.BlockSpec((nsq, N_META, KV_DIM), lambda b: (b, 0, 0)),
        pl.BlockSpec((nsq, WINDOW, KV_DIM), lambda b: (b, 0, 0)),
        pl.BlockSpec((nsq, WINDOW, KV_DIM), lambda b: (b, 0, 0)),
        pl.BlockSpec((blk, LANES), lambda b: (0, 0)),
        pl.BlockSpec((blk, LANES), lambda b: (0, 0)),
    ] + _param_specs(params, 1)
    return pl.pallas_call(
        _attn_sample_kernel,
        grid=(nseq // nsq,),
        in_specs=in_specs,
        out_specs=[
            pl.BlockSpec((blk, D_ATTN), lambda b: (b, 0)),
            pl.BlockSpec((blk, KV_DIM), lambda b: (b, 0)),
        ],
        out_shape=[
            jax.ShapeDtypeStruct((nseq * t, D_ATTN), BF16),
            jax.ShapeDtypeStruct((nseq * t, KV_DIM), F32),
        ],
        scratch_shapes=[pltpu.VMEM((blk, D_ATTN), F32)],
        compiler_params=pltpu.CompilerParams(dimension_semantics=("arbitrary",),
                                             vmem_limit_bytes=VMEM_LIMIT),
        name="attn_sample",
    )(proj, proj, proj, mk, mv, wk, wv, cos, sin, *params)


def _mix_out(x_ref, ys_ref, ya_ref, wo_ref):
    return (x_ref[...] + _dot(ys_ref[...], wo_ref[0:D_SSD, :])
            + _dot(ya_ref[...], wo_ref[D_SSD:D_SSD + D_ATTN, :]))


def _swiglu_acc(hn, wg_ref, wu_ref, wd_ref, acc_ref):
    for j in range(D_FF // FF_CHUNK):
        cols = slice(j * FF_CHUNK, (j + 1) * FF_CHUNK)
        a = (_silu(_dot(hn, wg_ref[:, cols])) * _dot(hn, wu_ref[:, cols])).astype(BF16)
        acc_ref[...] += _dot(a, wd_ref[cols, :])


def _out_ffn_kernel(x_ref, ys_ref, ya_ref, wo_ref, nw_ref, wg_ref, wu_ref, wd_ref, o_ref):
    xm = _mix_out(x_ref, ys_ref, ya_ref, wo_ref)
    o_ref[...] = xm
    _swiglu_acc(_rms(xm, nw_ref[...]).astype(BF16), wg_ref, wu_ref, wd_ref, o_ref)


def _resident(shape):
    nd = len(shape)
    return pl.BlockSpec(shape, lambda *a: (0,) * nd, pipeline_mode=pl.Buffered(1))


def _out_ffn(x, ys, ya, wo, nw, wg, wu, wd):
    rows = x.shape[0]
    return pl.pallas_call(
        _out_ffn_kernel,
        grid=(rows // ROW_TILE,),
        in_specs=[
            pl.BlockSpec((ROW_TILE, D_MODEL), lambda i: (i, 0)),
            pl.BlockSpec((ROW_TILE, D_SSD), lambda i: (i, 0)),
            pl.BlockSpec((ROW_TILE, D_ATTN), lambda i: (i, 0)),
            _resident(wo.shape), _resident(nw.shape),
            _resident(wg.shape), _resident(wu.shape), _resident(wd.shape),
        ],
        out_specs=pl.BlockSpec((ROW_TILE, D_MODEL), lambda i: (i, 0)),
        out_shape=jax.ShapeDtypeStruct((rows, D_MODEL), F32),
        compiler_params=pltpu.CompilerParams(dimension_semantics=("arbitrary",),
                                             vmem_limit_bytes=VMEM_LIMIT),
        name="out_ffn",
    )(x, ys, ya, wo, nw, wg, wu, wd)


def _out_router_kernel(n_tok, x_ref, ys_ref, ya_ref, wo_ref, nw_ref, wr_hi_ref, wr_lo_ref,
                       xm_o, rt_o, cnt_o):
    xm = _mix_out(x_ref, ys_ref, ya_ref, wo_ref)
    xm_o[...] = xm
    hn = _rms(xm, nw_ref[...])
    hi = hn.astype(BF16)
    lo = (hn - hi.astype(F32)).astype(BF16)
    logits = _dot(hi, wr_hi_ref[...]) + _dot(lo, wr_hi_ref[...]) + _dot(hi, wr_lo_ref[...])
    lane = lax.broadcasted_iota(jnp.int32, logits.shape, 1)
    logits = jnp.where(lane < N_EXPERTS, logits, -jnp.inf)
    v1 = jnp.max(logits, axis=-1, keepdims=True)
    i1 = jnp.min(jnp.where(logits == v1, lane, LANES), axis=-1, keepdims=True)
    rest = jnp.where(lane == i1, -jnp.inf, logits)
    v2 = jnp.max(rest, axis=-1, keepdims=True)
    i2 = jnp.min(jnp.where(rest == v2, lane, LANES), axis=-1, keepdims=True)
    e2 = jnp.exp(v2 - v1)
    g1 = 1.0 / (1.0 + e2)
    g2 = e2 / (1.0 + e2)

    row = pl.program_id(0) * ROW_TILE + lax.broadcasted_iota(jnp.int32, (ROW_TILE, 1), 0)
    valid = row < n_tok
    oh1 = jnp.where(jnp.logical_and(lane == i1, valid), 1.0, 0.0)
    oh2 = jnp.where(jnp.logical_and(lane == i2, valid), 1.0, 0.0)
    rr = lax.broadcasted_iota(jnp.int32, (ROW_TILE, ROW_TILE), 0)
    cc = lax.broadcasted_iota(jnp.int32, (ROW_TILE, ROW_TILE), 1)
    before = jnp.where(cc < rr, 1.0, 0.0).astype(BF16)
    c1 = _dot(before, oh1.astype(BF16))
    c2 = _dot(before, oh2.astype(BF16))
    tot1 = jnp.sum(oh1, axis=0, keepdims=True)
    tot2 = jnp.sum(oh2, axis=0, keepdims=True)
    rank1 = jnp.sum(jnp.where(lane == i1, c1, 0.0), axis=-1, keepdims=True)
    rank2 = jnp.sum(jnp.where(lane == i2, c2 + tot1, 0.0), axis=-1, keepdims=True)
    cnt_o[...] = jnp.broadcast_to(tot1 + tot2, cnt_o.shape)
    route = jnp.where(lane == 0, i1.astype(F32), 0.0)
    for k, val in enumerate((i2.astype(F32), g1, g2, rank1, rank2)):
        route = jnp.where(lane == k + 1, val, route)
    rt_o[...] = route


def _out_router(x, ys, ya, wo, nw, wr_hi, wr_lo, n_tok):
    rows = x.shape[0]
    return pl.pallas_call(
        functools.partial(_out_router_kernel, n_tok),
        grid=(rows // ROW_TILE,),
        in_specs=[
            pl.BlockSpec((ROW_TILE, D_MODEL), lambda i: (i, 0)),
            pl.BlockSpec((ROW_TILE, D_SSD), lambda i: (i, 0)),
            pl.BlockSpec((ROW_TILE, D_ATTN), lambda i: (i, 0)),
            _resident(wo.shape), _resident(nw.shape), _resident(wr_hi.shape), _resident(wr_lo.shape),
        ],
        out_specs=[
            pl.BlockSpec((ROW_TILE, D_MODEL), lambda i: (i, 0)),
            pl.BlockSpec((ROW_TILE, LANES), lambda i: (i, 0)),
            pl.BlockSpec((None, SUBLANES, LANES), lambda i: (i, 0, 0)),
        ],
        out_shape=[
            jax.ShapeDtypeStruct((rows, D_MODEL), F32),
            jax.ShapeDtypeStruct((rows, LANES), F32),
            jax.ShapeDtypeStruct((rows // ROW_TILE, SUBLANES, LANES), F32),
        ],
        compiler_params=pltpu.CompilerParams(dimension_semantics=("arbitrary",),
                                             vmem_limit_bytes=VMEM_LIMIT),
        name="out_router",
    )(x, ys, ya, wo, nw, wr_hi, wr_lo)


def _tile_rows(idx, n_tiles, tail, fn):
    if tail == ROW_TILE:
        fn(ROW_TILE)
    else:
        pl.when(idx < n_tiles - 1)(lambda: fn(ROW_TILE))
        pl.when(idx == n_tiles - 1)(lambda: fn(tail))


def _dispatch_kernel(n_tiles, tail, dest_ref, x_ref, xs_in_ref, xs_ref, sem):
    del xs_in_ref

    def run(nrows):
        def body(r, carry):
            for k in range(TOP_K):
                d = dest_ref[0, 0, TOP_K * r + k]
                pltpu.make_async_copy(x_ref.at[pl.ds(r, 1)], xs_ref.at[pl.ds(d, 1)], sem).start()
            return carry

        lax.fori_loop(0, nrows, body, 0)
        for k in range(TOP_K):
            pltpu.make_async_copy(x_ref.at[pl.ds(0, nrows)], xs_ref.at[pl.ds(0, nrows)], sem).wait()

    _tile_rows(pl.program_id(0), n_tiles, tail, run)


def _dispatch(dest, xm, n_tok, m_rows):
    rows = xm.shape[0]
    n_tiles = rows // ROW_TILE
    tail = n_tok - (n_tiles - 1) * ROW_TILE
    return pl.pallas_call(
        functools.partial(_dispatch_kernel, n_tiles, tail),
        grid=(n_tiles,),
        in_specs=[
            pl.BlockSpec((1, 1, TOP_K * ROW_TILE), lambda i: (i, 0, 0), memory_space=pltpu.SMEM),
            pl.BlockSpec((ROW_TILE, D_MODEL), lambda i: (i, 0)),
            pl.BlockSpec(memory_space=pl.ANY),
        ],
        out_specs=pl.BlockSpec(memory_space=pl.ANY),
        out_shape=jax.ShapeDtypeStruct((m_rows, D_MODEL), F32),
        scratch_shapes=[pltpu.SemaphoreType.DMA(())],
        input_output_aliases={2: 0},
        compiler_params=pltpu.CompilerParams(dimension_semantics=("arbitrary",),
                                             vmem_limit_bytes=VMEM_LIMIT, has_side_effects=True),
        name="moe_dispatch",
    )(dest, xm, jnp.zeros((m_rows, D_MODEL), F32))


def _moe_kernel(te_ref, nu_ref, x_ref, nw_ref, wg_ref, wu_ref, wd_ref, o_ref):
    i = pl.program_id(0)
    o_ref[...] = jnp.zeros(o_ref.shape, F32)

    @pl.when(i < nu_ref[0])
    def _():
        _swiglu_acc(_rms(x_ref[...], nw_ref[...]).astype(BF16), wg_ref, wu_ref, wd_ref, o_ref)


def _moe_experts(tile_e, n_used, xs, nw, wg, wu, wd):
    rows = xs.shape[0]
    grid_spec = pltpu.PrefetchScalarGridSpec(
        num_scalar_prefetch=2,
        grid=(rows // ROW_TILE,),
        in_specs=[
            pl.BlockSpec((ROW_TILE, D_MODEL), lambda i, te, nu: (i, 0)),
            pl.BlockSpec((1, D_MODEL), lambda i, te, nu: (0, 0)),
            pl.BlockSpec((None, D_MODEL, D_FF), lambda i, te, nu: (te[i], 0, 0)),
            pl.BlockSpec((None, D_MODEL, D_FF), lambda i, te, nu: (te[i], 0, 0)),
            pl.BlockSpec((None, D_FF, D_MODEL), lambda i, te, nu: (te[i], 0, 0)),
        ],
        out_specs=pl.BlockSpec((ROW_TILE, D_MODEL), lambda i, te, nu: (i, 0)),
    )
    return pl.pallas_call(
        _moe_kernel,
        grid_spec=grid_spec,
        out_shape=jax.ShapeDtypeStruct((rows, D_MODEL), F32),
        compiler_params=pltpu.CompilerParams(dimension_semantics=("arbitrary",),
                                             vmem_limit_bytes=VMEM_LIMIT),
        name="moe_experts",
    )(tile_e, n_used, xs, nw, wg, wu, wd)


def _combine_kernel(n_tiles, tail, dcur_ref, dnext_ref, xm_ref, rt_ref, yb_ref, o_ref, gbuf, sem):
    i = pl.program_id(0)

    def issue(dref, slot, nrows):
        def body(r, carry):
            for k in range(TOP_K):
                d = dref[0, 0, TOP_K * r + k]
                pltpu.make_async_copy(yb_ref.at[pl.ds(d, 1)], gbuf.at[slot, k, pl.ds(r, 1)],
                                      sem.at[slot]).start()
            return carry

        lax.fori_loop(0, nrows, body, 0)

    def wait(slot, nrows):
        for k in range(TOP_K):
            pltpu.make_async_copy(yb_ref.at[pl.ds(0, nrows)], gbuf.at[slot, k, pl.ds(0, nrows)],
                                  sem.at[slot]).wait()

    @pl.when(i == 0)
    def _():
        gbuf[...] = jnp.zeros(gbuf.shape, F32)
        _tile_rows(i, n_tiles, tail, lambda n: issue(dcur_ref, 0, n))

    @pl.when(i + 1 < n_tiles)
    def _():
        _tile_rows(i + 1, n_tiles, tail, lambda n: issue(dnext_ref, (i + 1) % 2, n))

    slot = i % 2
    _tile_rows(i, n_tiles, tail, lambda n: wait(slot, n))
    gates = rt_ref[...]
    o_ref[...] = (xm_ref[...] + gates[:, TOP_K:TOP_K + 1] * gbuf[slot, 0]
                  + gates[:, TOP_K + 1:TOP_K + 2] * gbuf[slot, 1])


def _combine(dest, xm, route, yb, n_tok):
    rows = xm.shape[0]
    n_tiles = rows // ROW_TILE
    tail = n_tok - (n_tiles - 1) * ROW_TILE
    dspec = lambda f: pl.BlockSpec((1, 1, TOP_K * ROW_TILE), f, memory_space=pltpu.SMEM)
    return pl.pallas_call(
        functools.partial(_combine_kernel, n_tiles, tail),
        grid=(n_tiles,),
        in_specs=[
            dspec(lambda i: (i, 0, 0)),
            dspec(lambda i: (jnp.minimum(i + 1, n_tiles - 1), 0, 0)),
            pl.BlockSpec((ROW_TILE, D_MODEL), lambda i: (i, 0)),
            pl.BlockSpec((ROW_TILE, LANES), lambda i: (i, 0)),
            pl.BlockSpec(memory_space=pl.ANY),
        ],
        out_specs=pl.BlockSpec((ROW_TILE, D_MODEL), lambda i: (i, 0)),
        out_shape=jax.ShapeDtypeStruct((rows, D_MODEL), F32),
        scratch_shapes=[pltpu.VMEM((2, TOP_K, ROW_TILE, D_MODEL), F32), pltpu.SemaphoreType.DMA((2,))],
        compiler_params=pltpu.CompilerParams(dimension_semantics=("arbitrary",),
                                             vmem_limit_bytes=VMEM_LIMIT),
        name="moe_combine",
    )(dest, dest, xm, route, yb)


def _moe_layer(xm, route, counts, n_tok, nw, wg, wu, wd):
    rows = xm.shape[0]
    n_tiles = rows // ROW_TILE
    m_tiles = -(-(n_tok * TOP_K + N_EXPERTS * (ROW_TILE - 1)) // ROW_TILE)
    cnt = counts[:, 0, :N_EXPERTS].astype(jnp.int32)
    total = jnp.sum(cnt, axis=0)
    padded = (total + ROW_TILE - 1) // ROW_TILE * ROW_TILE
    pad_end = jnp.cumsum(padded)
    base = (pad_end - padded)[None, :] + jnp.cumsum(cnt, axis=0) - cnt
    e = route[:, 0:TOP_K].astype(jnp.int32).reshape(n_tiles, ROW_TILE, TOP_K)
    rank = route[:, 2 * TOP_K:3 * TOP_K].astype(jnp.int32).reshape(n_tiles, ROW_TILE, TOP_K)
    onehot = e[..., None] == jnp.arange(N_EXPERTS, dtype=jnp.int32)
    dest = jnp.sum(jnp.where(onehot, base[:, None, None, :], 0), axis=-1) + rank
    dest = dest.reshape(n_tiles, 1, ROW_TILE * TOP_K)
    tile_e = jnp.minimum(jnp.searchsorted(pad_end, jnp.arange(m_tiles, dtype=jnp.int32) * ROW_TILE,
                                          side='right'), N_EXPERTS - 1).astype(jnp.int32)
    n_used = (pad_end[-1:] // ROW_TILE).astype(jnp.int32)
    xs = _dispatch(dest, xm, n_tok, m_tiles * ROW_TILE)
    yb = _moe_experts(tile_e, n_used, xs, nw, wg, wu, wd)
    return _combine(dest, xm, route, yb, n_tok)


def _rope_tables(pos):
    half = HEAD_DIM // 2
    inv_freq = ROPE_THETA ** (-jnp.arange(half, dtype=F32) / half)
    ang = pos.astype(F32)[:, None] * inv_freq[None, :]
    cos = jnp.cos(ang)
    sin = jnp.sin(ang)
    reps = LANES // HEAD_DIM
    return (jnp.tile(jnp.concatenate([cos, cos], axis=-1), (1, reps)),
            jnp.tile(jnp.concatenate([-sin, sin], axis=-1), (1, reps)))


def _pad_lanes(v, width=LANES):
    v = v.astype(F32).reshape(1, -1)
    return jnp.pad(v, ((0, 0), (0, width - v.shape[1])))


def kernel(x_prompt, x_sample, state_ssm, state_conv, cache_meta_k, cache_meta_v, cache_win_k, cache_win_v, meta_tokens, norm_mix_w, w_in, conv_w, conv_b, dt_bias, a_log, d_skip, ssd_norm_w, q_norm_w, k_norm_w, sinks, attn_norm_w, w_out, norm_ffn_w, w_gate, w_up, w_down, w_router, moe_w_gate, moe_w_up, moe_w_down):
    bsz, seq, _ = x_prompt.shape
    nseq, t_s, _ = x_sample.shape
    depth = w_in.shape[0]
    assert seq % CHUNK == 0 and t_s == SUBLANES
    r_main = bsz * seq
    r_samp = nseq * t_s
    n_tok = r_main + r_samp + N_META
    rows = -(-n_tok // ROW_TILE) * ROW_TILE
    meta_row0 = r_main + r_samp

    x = jnp.concatenate([x_prompt.reshape(r_main, D_MODEL), x_sample.reshape(r_samp, D_MODEL),
                         meta_tokens.astype(F32), jnp.zeros((rows - n_tok, D_MODEL), F32)], axis=0)

    pos_p = jnp.arange(seq + CHUNK, dtype=jnp.int32) - (CHUNK - N_META)
    cos_p, sin_p = _rope_tables(pos_p)
    nsq = 16 if nseq % 16 == 0 else nseq
    cos_s, sin_s = (jnp.tile(tab, (nsq, 1)) for tab in
                    _rope_tables(PAST_LEN + jnp.arange(t_s, dtype=jnp.int32)))

    o_z, o_xbc, o_dt, o_q, o_k, o_v = 0, 512, 1536, 1544, 2056, 2184
    col = jnp.arange(D_ATTN, dtype=jnp.int32)
    grp, lane = col // LANES, col % LANES
    head_perm = (grp + Q_PER_KV * (lane // HEAD_DIM)) * HEAD_DIM + lane % HEAD_DIM

    outs = {k: [] for k in ('p_ssm', 'p_conv', 'p_mk', 'p_mv', 'p_wk', 'p_wv', 's_ssm', 's_conv', 's_wk', 's_wv')}
    for l in range(depth):
        wl = w_in[l]
        w_re = jnp.concatenate([
            wl[:, o_xbc:o_xbc + CONV_DIM], wl[:, o_z:o_z + D_SSD], wl[:, o_q:o_q + D_ATTN][:, head_perm],
            wl[:, o_k:o_k + KV_DIM], wl[:, o_v:o_v + KV_DIM], wl[:, o_dt:o_dt + N_SSD_HEADS],
            jnp.zeros((D_MODEL, PROJ_W - COL_DT - N_SSD_HEADS), wl.dtype)], axis=1).astype(BF16)
        proj = _in_proj(x, norm_mix_w[l].reshape(1, D_MODEL), w_re)

        ssd_params = (conv_w[l].astype(F32), conv_b[l].reshape(1, CONV_DIM).astype(F32),
                      _pad_lanes(dt_bias[l]), _pad_lanes(-jnp.exp(a_log[l].astype(F32))),
                      _pad_lanes(d_skip[l]), ssd_norm_w[l].reshape(1, D_SSD).astype(F32))
        ys_p, ys_m, ssm_p = _ssd_prompt(proj, ssd_params, bsz, seq, meta_row0)
        conv_prev = jnp.pad(state_conv[l].astype(F32), ((0, 0), (SUBLANES - (CONV_W - 1), 0), (0, 0)))
        ys_s, ssm_s = _ssd_sample(proj, conv_prev, state_ssm[l].astype(F32), ssd_params, nseq, t_s, r_main)

        attn_params = (jnp.tile(q_norm_w[l].astype(F32), N_Q_HEADS).reshape(1, D_ATTN),
                       jnp.tile(k_norm_w[l].astype(F32), N_KV_HEADS).reshape(1, KV_DIM),
                       _pad_lanes(sinks[l]), attn_norm_w[l][head_perm].reshape(1, D_ATTN).astype(F32))
        ya_p, ya_m, kp_p, kp_m = _attn_prompt(proj, cos_p, sin_p, attn_params, bsz, seq, meta_row0)
        ya_s, kp_s = _attn_sample(
            proj, cache_meta_k[l].reshape(nseq, N_META, KV_DIM).astype(F32),
            cache_meta_v[l].reshape(nseq, N_META, KV_DIM).astype(F32),
            cache_win_k[l].reshape(nseq, WINDOW, KV_DIM).astype(F32),
            cache_win_v[l].reshape(nseq, WINDOW, KV_DIM).astype(F32),
            cos_s, sin_s, attn_params, nseq, t_s, r_main, nsq)

        pad_rows = rows - n_tok
        ys = jnp.concatenate([ys_p, ys_s, ys_m[0], jnp.zeros((pad_rows, D_SSD), BF16)], axis=0)
        ya = jnp.concatenate([ya_p, ya_s, ya_m[0], jnp.zeros((pad_rows, D_ATTN), BF16)], axis=0)

        wo = jnp.concatenate([w_out[l][:D_SSD], w_out[l][D_SSD:][head_perm]], axis=0).astype(BF16)
        nfw = norm_ffn_w[l].reshape(1, D_MODEL).astype(F32)
        i = l // 2
        if l % 2 == 0:
            x = _out_ffn(x, ys, ya, wo, nfw, w_gate[i].astype(BF16), w_up[i].astype(BF16),
                         w_down[i].astype(BF16))
        else:
            wr = jnp.pad(w_router[i].astype(F32), ((0, 0), (0, LANES - N_EXPERTS)))
            wr_hi = wr.astype(BF16)
            wr_lo = (wr - wr_hi.astype(F32)).astype(BF16)
            xm, route, counts = _out_router(x, ys, ya, wo, nfw, wr_hi, wr_lo, n_tok)
            x = _moe_layer(xm, route, counts, n_tok, nfw, moe_w_gate[i].astype(BF16),
                           moe_w_up[i].astype(BF16), moe_w_down[i].astype(BF16))

        proj_p = proj[:r_main].reshape(bsz, seq, PROJ_W)
        xbc_p = proj_p[:, seq - (CONV_W - 1):, COL_XBC:COL_XBC + CONV_DIM]
        v_p = proj_p[:, seq - WINDOW:, COL_V:COL_V + KV_DIM]
        kp_p3 = kp_p.reshape(bsz, seq, KV_DIM)
        v_meta = proj[meta_row0:meta_row0 + N_META, COL_V:COL_V + KV_DIM]
        outs['p_ssm'].append(ssm_p)
        outs['p_conv'].append(xbc_p)
        outs['p_mk'].append(jnp.broadcast_to(kp_m[0].reshape(1, N_META, N_KV_HEADS, HEAD_DIM),
                                             (bsz, N_META, N_KV_HEADS, HEAD_DIM)))
        outs['p_mv'].append(jnp.broadcast_to(v_meta.reshape(1, N_META, N_KV_HEADS, HEAD_DIM),
                                             (bsz, N_META, N_KV_HEADS, HEAD_DIM)))
        outs['p_wk'].append(kp_p3[:, seq - WINDOW:].reshape(bsz, WINDOW, N_KV_HEADS, HEAD_DIM))
        outs['p_wv'].append(v_p.reshape(bsz, WINDOW, N_KV_HEADS, HEAD_DIM))
        proj_s = proj[r_main:r_main + r_samp]
        xbc_s = proj_s[:, COL_XBC:COL_XBC + CONV_DIM].reshape(nseq, t_s, CONV_DIM)
        v_s = proj_s[:, COL_V:COL_V + KV_DIM].reshape(nseq, t_s, N_KV_HEADS, HEAD_DIM)
        outs['s_ssm'].append(ssm_s)
        outs['s_conv'].append(jnp.concatenate([state_conv[l].astype(F32), xbc_s], axis=1)[:, t_s:])
        outs['s_wk'].append(jnp.concatenate([cache_win_k[l].astype(F32),
                                             kp_s.reshape(nseq, t_s, N_KV_HEADS, HEAD_DIM)], axis=1)[:, t_s:])
        outs['s_wv'].append(jnp.concatenate([cache_win_v[l].astype(F32), v_s], axis=1)[:, t_s:])

    y_prompt = x[:r_main].reshape(bsz, seq, D_MODEL)
    y_sample = x[r_main:r_main + r_samp].reshape(nseq, t_s, D_MODEL)
    st = lambda k: jnp.stack(outs[k])
    return (y_prompt, y_sample, st('p_ssm'), st('p_conv'), st('p_mk'), st('p_mv'), st('p_wk'), st('p_wv'),
            st('s_ssm'), st('s_conv'), st('s_wk'), st('s_wv'))
```

```python
import functools

import jax
import jax.numpy as jnp
from jax import lax
from jax.experimental import pallas as pl
from jax.experimental.pallas import tpu as pltpu

F32 = jnp.float32
BF16 = jnp.bfloat16

D_MODEL = 1024
D_SSD = 512
SSD_HEAD_DIM = 64
N_SSD_HEADS = 8
SSD_HEADS_PER_GROUP = 4
N_SSD_GROUPS = 2
D_STATE = 128
CONV_W = 4
CONV_DIM = 1024
D_ATTN = 512
HEAD_DIM = 64
N_Q_HEADS = 8
N_KV_HEADS = 2
Q_PER_KV = 4
KV_DIM = 128
WINDOW = 128
N_META = 16
D_FF = 2816
N_EXPERTS = 8
TOP_K = 2
EPS = 1e-6
NEG = -1e30
ATTN_SCALE = HEAD_DIM ** -0.5
PAST_LEN = 16384
ROPE_THETA = 10000.0

LANES = 128
SUBLANES = 8
CHUNK = 128
ROW_TILE = 512
FF_CHUNK = 256
DMA_UNROLL = 8
VMEM_LIMIT = 60 * 1024 * 1024

COL_XBC = 0
COL_Z = 1024
COL_Q = 1536
COL_K = 2048
COL_V = 2176
COL_DT = 2304
PROJ_W = 2560


def _dot(a, b):
    return jnp.dot(a, b, preferred_element_type=F32)


def _dot_nt(a, b):
    return lax.dot_general(a, b, (((1,), (1,)), ((), ())), preferred_element_type=F32)


def _dot_tn(a, b):
    return lax.dot_general(a, b, (((0,), (0,)), ((), ())), preferred_element_type=F32)


def _rms(x, w):
    return x * lax.rsqrt(jnp.mean(x * x, axis=-1, keepdims=True) + EPS) * w


def _silu(x):
    return x * jax.nn.sigmoid(x)


def _split3(x):
    p1 = x.astype(BF16)
    r1 = x - p1.astype(F32)
    p2 = r1.astype(BF16)
    p3 = (r1 - p2.astype(F32)).astype(BF16)
    return p1, p2, p3


def _dot_exact(sel, x):
    p1, p2, p3 = _split3(x)
    return _dot(sel, p1) + _dot(sel, p2) + _dot(sel, p3)


def _cparams(ndim, **kw):
    return pltpu.CompilerParams(dimension_semantics=("arbitrary",) * ndim, vmem_limit_bytes=VMEM_LIMIT, **kw)


def _src_specs(width, n_main):
    return [pl.BlockSpec((ROW_TILE, width), lambda i: (jnp.minimum(i, n_main - 1), 0)),
            pl.BlockSpec((ROW_TILE, width), lambda i: (jnp.maximum(i - n_main, 0), 0))]


def _pick(n_main, main_ref, tail_ref):
    dtype = main_ref.dtype
    picked = jnp.where(pl.program_id(0) < n_main, main_ref[...].astype(F32), tail_ref[...].astype(F32))
    return picked.astype(dtype)


def _dst_specs(width, n_main):
    return [pl.BlockSpec((ROW_TILE, width), lambda i: (jnp.minimum(i, n_main), 0)),
            pl.BlockSpec((ROW_TILE, width), lambda i: (jnp.maximum(i - n_main, 0), 0))]


def _dst_shapes(width, n_main, n_tail, dtype):
    return [jax.ShapeDtypeStruct(((n_main + 1) * ROW_TILE, width), dtype),
            jax.ShapeDtypeStruct((n_tail * ROW_TILE, width), dtype)]


def _resident(shape):
    nd = len(shape)
    return pl.BlockSpec(shape, lambda *a: (0,) * nd, pipeline_mode=pl.Buffered(1))


def _param_specs(params):
    return [pl.BlockSpec(p.shape, lambda *a: (0, 0)) for p in params]


def _in_proj_kernel(n_main, xa_ref, xb_ref, nw_ref, w_ref, om_ref, ot_ref):
    xn = _rms(_pick(n_main, xa_ref, xb_ref), nw_ref[...]).astype(BF16)
    for j in range(PROJ_W // 512):
        cols = slice(j * 512, (j + 1) * 512)
        r = _dot(xn, w_ref[:, cols])
        om_ref[:, cols] = r
        ot_ref[:, cols] = r


def _in_proj(x_main, x_tail, nw, w, n_main):
    n_tail = x_tail.shape[0] // ROW_TILE
    return pl.pallas_call(
        functools.partial(_in_proj_kernel, n_main),
        grid=(n_main + n_tail,),
        in_specs=_src_specs(D_MODEL, n_main) + [
            pl.BlockSpec((1, D_MODEL), lambda i: (0, 0)),
            pl.BlockSpec((D_MODEL, PROJ_W), lambda i: (0, 0)),
        ],
        out_specs=_dst_specs(PROJ_W, n_main),
        out_shape=_dst_shapes(PROJ_W, n_main, n_tail, F32),
        compiler_params=_cparams(1),
        name="in_proj",
    )(x_main, x_tail, nw, w)


def _ssd_chunk(xbc, z, dt_raw, valid, cw_ref, cb_ref, dtb_ref, an_ref, dsk_ref, nw_ref,
               cbuf, hst, ybuf):
    q = CHUNK
    cbuf[SUBLANES:SUBLANES + q, :] = xbc
    acc = cb_ref[...]
    for k in range(CONV_W):
        off = SUBLANES - (CONV_W - 1) + k
        acc = acc + cbuf[off:off + q, :] * cw_ref[k:k + 1, :]
    cbuf[0:SUBLANES, :] = cbuf[q:q + SUBLANES, :]
    xc = _silu(acc)
    xs = xc[:, :D_SSD]
    bm = xc[:, D_SSD:D_SSD + N_SSD_GROUPS * D_STATE]
    cm = xc[:, D_SSD + N_SSD_GROUPS * D_STATE:]

    dt = jnp.where(valid, jax.nn.softplus(dt_raw + dtb_ref[...]), 0.0)
    da = dt * an_ref[...]
    row_i = lax.broadcasted_iota(jnp.int32, (q, q), 0)
    col_j = lax.broadcasted_iota(jnp.int32, (q, q), 1)
    tril = row_i >= col_j
    cs = _dot_exact(jnp.where(tril, 1.0, 0.0).astype(BF16), da)
    cs_t = cs.T
    last = cs[q - 1:q, :]
    ecs = jnp.exp(cs)
    dte = jnp.exp(last - cs)
    ecl = jnp.exp(last)

    for g in range(N_SSD_GROUPS):
        bg = bm[:, g * D_STATE:(g + 1) * D_STATE].astype(BF16)
        cg = cm[:, g * D_STATE:(g + 1) * D_STATE].astype(BF16)
        cb = _dot_nt(cg, bg)
        for r in range(SSD_HEADS_PER_GROUP):
            h = g * SSD_HEADS_PER_GROUP + r
            seg = cs[:, h:h + 1] - cs_t[h:h + 1, :]
            decay = jnp.where(tril, jnp.exp(jnp.where(tril, seg, 0.0)), 0.0)
            xs_h = xs[:, h * SSD_HEAD_DIM:(h + 1) * SSD_HEAD_DIM]
            xdt = xs_h * dt[:, h:h + 1]
            y_diag = _dot((decay * cb).astype(BF16), xdt.astype(BF16))
            h_prev = hst[h]
            y_off = _dot_nt(cg, h_prev.astype(BF16)) * ecs[:, h:h + 1]
            ybuf[:, h * SSD_HEAD_DIM:(h + 1) * SSD_HEAD_DIM] = y_diag + y_off + dsk_ref[:, h:h + 1] * xs_h
            xw = (xdt * dte[:, h:h + 1]).astype(BF16)
            hst[h] = h_prev * ecl[:, h:h + 1] + _dot_tn(xw, bg)

    return _ssd_gate_norm(ybuf[...], z, nw_ref)


def _ssd_gate_norm(y, z, nw_ref):
    y = y * _silu(z)
    gs = D_SSD // N_SSD_GROUPS
    return jnp.concatenate([_rms(y[:, g * gs:(g + 1) * gs], nw_ref[:, g * gs:(g + 1) * gs])
                            for g in range(N_SSD_GROUPS)], axis=-1)


def _ssd_prompt_kernel(xbc_m, z_m, dt_m, xbc_t, z_t, dt_t, cw, cb, dtb, an, dsk, nw,
                       y_o, ym_o, hf_o, ct_o, cbuf, hst, ybuf, ymbuf):
    c = pl.program_id(1)
    first = c == 0
    pad = CHUNK - N_META

    @pl.when(first)
    def _():
        cbuf[0:SUBLANES, :] = jnp.zeros((SUBLANES, CONV_DIM), F32)
        hst[...] = jnp.zeros(hst.shape, F32)

    def stage(main_ref, meta_ref):
        meta = jnp.concatenate([jnp.zeros((pad, meta_ref.shape[1]), F32), meta_ref[...]], axis=0)
        return jnp.where(first, meta, main_ref[...])

    row = lax.broadcasted_iota(jnp.int32, (CHUNK, 1), 0)
    valid = jnp.logical_or(c > 0, row >= pad)
    y = _ssd_chunk(stage(xbc_m, xbc_t), stage(z_m, z_t), stage(dt_m, dt_t), valid,
                   cw, cb, dtb, an, dsk, nw, cbuf, hst, ybuf).astype(BF16)
    y_o[...] = y

    @pl.when(first)
    def _():
        ymbuf[...] = y[pad:, :].astype(F32)

    ym_o[...] = ymbuf[...].astype(BF16)
    hf_o[...] = hst[...]
    ct_o[...] = cbuf[0:SUBLANES, :]


def _ssd_prompt(proj_main, proj_tail, params, bsz, seq, meta_row):
    nc = seq // CHUNK
    mb = meta_row // N_META

    def main(col):
        return lambda b, c: (b * nc + jnp.maximum(c - 1, 0), col)

    def meta(col):
        return lambda b, c: (mb, col)

    in_specs = [
        pl.BlockSpec((CHUNK, CONV_DIM), main(COL_XBC // CONV_DIM)),
        pl.BlockSpec((CHUNK, D_SSD), main(COL_Z // D_SSD)),
        pl.BlockSpec((CHUNK, LANES), main(COL_DT // LANES)),
        pl.BlockSpec((N_META, CONV_DIM), meta(COL_XBC // CONV_DIM)),
        pl.BlockSpec((N_META, D_SSD), meta(COL_Z // D_SSD)),
        pl.BlockSpec((N_META, LANES), meta(COL_DT // LANES)),
    ] + _param_specs(params)
    return pl.pallas_call(
        _ssd_prompt_kernel,
        grid=(bsz, nc + 1),
        in_specs=in_specs,
        out_specs=[
            pl.BlockSpec((CHUNK, D_SSD), main(0)),
            pl.BlockSpec((None, N_META, D_SSD), lambda b, c: (b, 0, 0)),
            pl.BlockSpec((None, N_SSD_HEADS, SSD_HEAD_DIM, D_STATE), lambda b, c: (b, 0, 0, 0)),
            pl.BlockSpec((None, SUBLANES, CONV_DIM), lambda b, c: (b, 0, 0)),
        ],
        out_shape=[
            jax.ShapeDtypeStruct((bsz * seq, D_SSD), BF16),
            jax.ShapeDtypeStruct((bsz, N_META, D_SSD), BF16),
            jax.ShapeDtypeStruct((bsz, N_SSD_HEADS, SSD_HEAD_DIM, D_STATE), F32),
            jax.ShapeDtypeStruct((bsz, SUBLANES, CONV_DIM), F32),
        ],
        scratch_shapes=[pltpu.VMEM((CHUNK + SUBLANES, CONV_DIM), F32),
                        pltpu.VMEM((N_SSD_HEADS, SSD_HEAD_DIM, D_STATE), F32),
                        pltpu.VMEM((CHUNK, D_SSD), F32),
                        pltpu.VMEM((N_META, D_SSD), F32)],
        compiler_params=_cparams(2),
        name="ssd_prompt",
    )(proj_main, proj_main, proj_main, proj_tail, proj_tail, proj_tail, *params)


def _expand_heads(a):
    hh = lax.broadcasted_iota(jnp.int32, (LANES, D_SSD), 0)
    cc = lax.broadcasted_iota(jnp.int32, (LANES, D_SSD), 1)
    sel = jnp.where(jnp.right_shift(cc, SSD_HEAD_DIM.bit_length() - 1) == hh, 1.0, 0.0).astype(BF16)
    p1, p2, p3 = _split3(a)
    return _dot(p1, sel) + _dot(p2, sel) + _dot(p3, sel)


def _pad_rows_bf16(x, rows):
    return jnp.concatenate([x, jnp.zeros((rows - x.shape[0], x.shape[1]), F32)], axis=0).astype(BF16)


def _ssd_sample_kernel(xbc_s, z_s, dt_s, cprev, h0, cw, cb, dtb, an, dsk, nw,
                       y_o, hf_o, ubuf, ybuf):
    nsq = cprev.shape[0]
    t = SUBLANES
    rows = nsq * t
    x = xbc_s[...]
    for s in range(nsq):
        ubuf[2 * t * s:2 * t * s + t, :] = cprev[s]
        ubuf[2 * t * s + t:2 * t * (s + 1), :] = x[s * t:(s + 1) * t, :]
    parts = []
    for s in range(nsq):
        acc = cb[...]
        for k in range(CONV_W):
            off = 2 * t * s + t - (CONV_W - 1) + k
            acc = acc + ubuf[off:off + t, :] * cw[k:k + 1, :]
        parts.append(acc)
    xc = _silu(jnp.concatenate(parts, axis=0))
    xs = xc[:, :D_SSD]
    bm = xc[:, D_SSD:D_SSD + N_SSD_GROUPS * D_STATE]
    cm = xc[:, D_SSD + N_SSD_GROUPS * D_STATE:]

    dt = jax.nn.softplus(dt_s[...] + dtb[...])
    da = dt * an[...]
    ri = lax.broadcasted_iota(jnp.int32, (rows, rows), 0)
    cj = lax.broadcasted_iota(jnp.int32, (rows, rows), 1)
    mask = jnp.logical_and(ri >= cj, jnp.right_shift(ri, 3) == jnp.right_shift(cj, 3))
    cs = _dot_exact(jnp.where(mask, 1.0, 0.0).astype(BF16), da)
    last = _dot_exact(jnp.where(cj == jnp.bitwise_or(ri, t - 1), 1.0, 0.0).astype(BF16), cs)
    cs_t = cs.T
    ecl = jnp.exp(last)
    xdt = xs * _expand_heads(dt)
    xw = xdt * _expand_heads(jnp.exp(last - cs))
    ecs_x = _expand_heads(jnp.exp(cs))
    skip = xs * _expand_heads(jnp.broadcast_to(dsk[...], (rows, LANES)))

    gw = SSD_HEADS_PER_GROUP * SSD_HEAD_DIM
    for g in range(N_SSD_GROUPS):
        bg_f = bm[:, g * D_STATE:(g + 1) * D_STATE]
        cg_f = cm[:, g * D_STATE:(g + 1) * D_STATE]
        cbm = _dot_nt(cg_f.astype(BF16), bg_f.astype(BF16))
        for r in range(SSD_HEADS_PER_GROUP):
            h = g * SSD_HEADS_PER_GROUP + r
            hc = slice(h * SSD_HEAD_DIM, (h + 1) * SSD_HEAD_DIM)
            seg = cs[:, h:h + 1] - cs_t[h:h + 1, :]
            decay = jnp.where(mask, jnp.exp(jnp.where(mask, seg, 0.0)), 0.0)
            ybuf[:, hc] = _dot((decay * cbm).astype(BF16), xdt[:, hc].astype(BF16)) + skip[:, hc]
        gc = slice(g * gw, (g + 1) * gw)
        heads = slice(g * SSD_HEADS_PER_GROUP, (g + 1) * SSD_HEADS_PER_GROUP)
        for s in range(nsq):
            rs = slice(s * t, (s + 1) * t)
            hg = h0[s, heads].reshape(gw, D_STATE)
            y_off = _dot_nt(_pad_rows_bf16(cg_f[rs, :], 2 * t), hg.astype(BF16))[0:t, :]
            ybuf[rs, gc] = ybuf[rs, gc] + y_off * ecs_x[rs, gc]
            ecl_col = jnp.concatenate(
                [jnp.broadcast_to(ecl[s * t:s * t + 1, h:h + 1], (SSD_HEAD_DIM, 1))
                 for h in range(heads.start, heads.stop)], axis=0)
            h_new = hg * ecl_col + _dot_tn(_pad_rows_bf16(xw[rs, gc], 2 * t), _pad_rows_bf16(bg_f[rs, :], 2 * t))
            hf_o[s, heads] = h_new.reshape(SSD_HEADS_PER_GROUP, SSD_HEAD_DIM, D_STATE)

    y_o[...] = _ssd_gate_norm(ybuf[...], z_s[...], nw).astype(BF16)


def _ssd_sample(proj_tail, conv_prev, h0, params, nseq, t, nsq):
    blk = nsq * t
    in_specs = [
        pl.BlockSpec((blk, CONV_DIM), lambda b: (b, COL_XBC // CONV_DIM)),
        pl.BlockSpec((blk, D_SSD), lambda b: (b, COL_Z // D_SSD)),
        pl.BlockSpec((blk, LANES), lambda b: (b, COL_DT // LANES)),
        pl.BlockSpec((nsq, SUBLANES, CONV_DIM), lambda b: (b, 0, 0)),
        pl.BlockSpec((nsq, N_SSD_HEADS, SSD_HEAD_DIM, D_STATE), lambda b: (b, 0, 0, 0)),
    ] + _param_specs(params)
    return pl.pallas_call(
        _ssd_sample_kernel,
        grid=(nseq // nsq,),
        in_specs=in_specs,
        out_specs=[
            pl.BlockSpec((blk, D_SSD), lambda b: (b, 0)),
            pl.BlockSpec((nsq, N_SSD_HEADS, SSD_HEAD_DIM, D_STATE), lambda b: (b, 0, 0, 0)),
        ],
        out_shape=[
            jax.ShapeDtypeStruct((nseq * t, D_SSD), BF16),
            jax.ShapeDtypeStruct((nseq, N_SSD_HEADS, SSD_HEAD_DIM, D_STATE), F32),
        ],
        scratch_shapes=[pltpu.VMEM((2 * blk, CONV_DIM), F32), pltpu.VMEM((blk, D_SSD), F32)],
        compiler_params=_cparams(1),
        name="ssd_sample",
    )(proj_tail, proj_tail, proj_tail, conv_prev, h0, *params)


def _qk_prep(x, w, cos, sin):
    lane = lax.broadcasted_iota(jnp.int32, x.shape, 1)
    lo_head = lane < HEAD_DIM
    sq = x * x
    s_lo = jnp.sum(jnp.where(lo_head, sq, 0.0), axis=-1, keepdims=True)
    s_all = jnp.sum(sq, axis=-1, keepdims=True)
    ms = jnp.where(lo_head, s_lo, s_all - s_lo) * (1.0 / HEAD_DIM)
    xn = x * lax.rsqrt(ms + EPS) * w
    half = HEAD_DIM // 2
    first_half = (lane % HEAD_DIM) < half
    partner = jnp.where(first_half, pltpu.roll(xn, LANES - half, 1), pltpu.roll(xn, half, 1))
    return xn * cos + partner * sin


def _q_groups(q, qnw_ref, cos, sin):
    return [_qk_prep(q[:, g * LANES:(g + 1) * LANES], qnw_ref[:, g * LANES:(g + 1) * LANES], cos, sin)
            * ATTN_SCALE for g in range(D_ATTN // LANES)]


def _sink_column(snk_ref, rows):
    return jnp.concatenate([jnp.broadcast_to(snk_ref[:, h:h + 1], (rows, 1)) for h in range(N_Q_HEADS)],
                           axis=0)


def _softmax_weights(pieces, sk):
    top = pieces[0]
    for p in pieces[1:]:
        top = jnp.maximum(top, p)
    m = jnp.maximum(jnp.max(top, axis=-1, keepdims=True), sk)
    e = [jnp.exp(p - m) for p in pieces]
    tot = e[0]
    for p in e[1:]:
        tot = tot + p
    return e, 1.0 / (jnp.sum(tot, axis=-1, keepdims=True) + jnp.exp(sk - m))


def _attn_prompt_kernel(q_m, k_m, v_m, q_t, k_t, v_t, cos, sin, qnw, knw, snk, anw,
                        o_o, om_o, kpm_o, kl_o, vl_o,
                        km_lo, km_hi, vm, kp_lo, kp_hi, vp, kmf, ombuf):
    c = pl.program_id(1)
    first = c == 0
    pad = CHUNK - N_META
    nst = Q_PER_KV * CHUNK

    def stage(main_ref, meta_ref):
        meta = jnp.concatenate([jnp.zeros((pad, meta_ref.shape[1]), F32), meta_ref[...]], axis=0)
        return jnp.where(first, meta, main_ref[...])

    cs, sn = cos[...], sin[...]
    v = stage(v_m, v_t)
    v_b = v.astype(BF16)
    kp = _qk_prep(stage(k_m, k_t), knw[...], cs, sn)
    lane = lax.broadcasted_iota(jnp.int32, (CHUNK, LANES), 1)
    k_lo = jnp.where(lane < HEAD_DIM, kp, 0.0).astype(BF16)
    k_hi = jnp.where(lane < HEAD_DIM, 0.0, kp).astype(BF16)

    @pl.when(first)
    def _():
        zpad = jnp.zeros((pad, LANES), BF16)
        km_lo[...] = jnp.concatenate([k_lo[pad:, :], zpad], axis=0)
        km_hi[...] = jnp.concatenate([k_hi[pad:, :], zpad], axis=0)
        vm[...] = jnp.concatenate([v_b[pad:, :], zpad], axis=0)
        kmf[...] = kp[pad:, :]
        kp_lo[...] = jnp.zeros(kp_lo.shape, BF16)
        kp_hi[...] = jnp.zeros(kp_hi.shape, BF16)
        vp[...] = jnp.zeros(vp.shape, BF16)

    qs = jnp.concatenate(_q_groups(stage(q_m, q_t), qnw, cs, sn), axis=0).astype(BF16)

    def scores(key_lo, key_hi):
        return jnp.concatenate([_dot_nt(qs, key_lo), _dot_nt(qs, key_hi)], axis=0)

    ri = lax.broadcasted_iota(jnp.int32, (2 * nst, LANES), 0) & (CHUNK - 1)
    cj = lax.broadcasted_iota(jnp.int32, (2 * nst, LANES), 1)
    tri = cj <= ri
    band_ok = jnp.logical_or(jnp.logical_and(tri, c >= 1), jnp.logical_and(cj > ri, c >= 2))
    band = jnp.where(band_ok, jnp.where(tri, scores(k_lo, k_hi), scores(kp_lo[...], kp_hi[...])), NEG)
    meta_ok = jnp.logical_and(cj < N_META, jnp.logical_or(c > 0, cj <= ri - pad))
    meta = jnp.where(meta_ok, scores(km_lo[...], km_hi[...]), NEG)
    (e_b, e_m), inv = _softmax_weights([band, meta], _sink_column(snk, CHUNK))
    o2 = (_dot(jnp.where(tri, e_b, 0.0).astype(BF16), v_b)
          + _dot(jnp.where(tri, 0.0, e_b).astype(BF16), vp[...])
          + _dot(e_m.astype(BF16), vm[...])) * inv
    lane_st = lax.broadcasted_iota(jnp.int32, (nst, LANES), 1)
    o_st = jnp.where(lane_st < HEAD_DIM, o2[0:nst, :], o2[nst:2 * nst, :])
    o = jnp.concatenate([o_st[g * CHUNK:(g + 1) * CHUNK, :] for g in range(Q_PER_KV)], axis=1)
    o = _rms(o, anw[...]).astype(BF16)
    o_o[...] = o
    kp_lo[...] = k_lo
    kp_hi[...] = k_hi
    vp[...] = v_b

    @pl.when(first)
    def _():
        ombuf[...] = o[pad:, :].astype(F32)

    om_o[...] = ombuf[...].astype(BF16)
    kpm_o[...] = kmf[...]
    kl_o[...] = kp
    vl_o[...] = v


def _attn_sample_kernel(q_s, k_s, v_s, mk, mv, wk, wv, cos, sin, qnw, knw, snk, anw,
                        o_o, kp_o, obuf):
    nsq = mk.shape[0]
    t = SUBLANES
    nst = Q_PER_KV * t
    cs, sn = cos[...], sin[...]
    kp = _qk_prep(k_s[...], knw[...], cs, sn)
    kp_o[...] = kp
    v = v_s[...]
    qg = _q_groups(q_s[...], qnw, cs, sn)

    lo = lax.broadcasted_iota(jnp.int32, (nst, LANES), 1) < HEAD_DIM
    i_q = lax.broadcasted_iota(jnp.int32, (2 * nst, LANES), 0) & (t - 1)
    cj = lax.broadcasted_iota(jnp.int32, (2 * nst, LANES), 1)
    mask_a = cj > i_q
    mask_b = jnp.logical_or(cj <= i_q, jnp.logical_and(cj >= t, cj < t + N_META))
    sk = _sink_column(snk, t)
    zpad = jnp.zeros((WINDOW - t - N_META, LANES), F32)
    for s in range(nsq):
        rows = slice(s * t, (s + 1) * t)
        q_st = jnp.concatenate([g[rows, :] for g in qg], axis=0)
        q2 = jnp.concatenate([jnp.where(lo, q_st, 0.0), jnp.where(lo, 0.0, q_st)], axis=0).astype(BF16)
        k_b = jnp.concatenate([kp[rows, :], mk[s], zpad], axis=0).astype(BF16)
        v_b = jnp.concatenate([v[rows, :], mv[s], zpad], axis=0).astype(BF16)
        (e_a, e_b), inv = _softmax_weights(
            [jnp.where(mask_a, _dot_nt(q2, wk[s].astype(BF16)), NEG),
             jnp.where(mask_b, _dot_nt(q2, k_b), NEG)], sk)
        o2 = (_dot(e_a.astype(BF16), wv[s].astype(BF16)) + _dot(e_b.astype(BF16), v_b)) * inv
        o_st = jnp.where(lo, o2[0:nst, :], o2[nst:2 * nst, :])
        for g in range(Q_PER_KV):
            obuf[rows, g * LANES:(g + 1) * LANES] = o_st[g * t:(g + 1) * t, :]
    o_o[...] = _rms(obuf[...], anw[...]).astype(BF16)


def _attn_prompt(proj_main, proj_tail, cos, sin, params, bsz, seq, meta_row):
    nc = seq // CHUNK
    mb = meta_row // N_META

    def main(col):
        return lambda b, c: (b * nc + jnp.maximum(c - 1, 0), col)

    def meta(col):
        return lambda b, c: (mb, col)

    per_batch = lambda b, c: (b, 0, 0)
    in_specs = [
        pl.BlockSpec((CHUNK, D_ATTN), main(COL_Q // D_ATTN)),
        pl.BlockSpec((CHUNK, KV_DIM), main(COL_K // KV_DIM)),
        pl.BlockSpec((CHUNK, KV_DIM), main(COL_V // KV_DIM)),
        pl.BlockSpec((N_META, D_ATTN), meta(COL_Q // D_ATTN)),
        pl.BlockSpec((N_META, KV_DIM), meta(COL_K // KV_DIM)),
        pl.BlockSpec((N_META, KV_DIM), meta(COL_V // KV_DIM)),
        pl.BlockSpec((CHUNK, LANES), lambda b, c: (c, 0)),
        pl.BlockSpec((CHUNK, LANES), lambda b, c: (c, 0)),
    ] + _param_specs(params)
    return pl.pallas_call(
        _attn_prompt_kernel,
        grid=(bsz, nc + 1),
        in_specs=in_specs,
        out_specs=[
            pl.BlockSpec((CHUNK, D_ATTN), main(0)),
            pl.BlockSpec((None, N_META, D_ATTN), per_batch),
            pl.BlockSpec((None, N_META, KV_DIM), per_batch),
            pl.BlockSpec((None, CHUNK, KV_DIM), per_batch),
            pl.BlockSpec((None, CHUNK, KV_DIM), per_batch),
        ],
        out_shape=[
            jax.ShapeDtypeStruct((bsz * seq, D_ATTN), BF16),
            jax.ShapeDtypeStruct((bsz, N_META, D_ATTN), BF16),
            jax.ShapeDtypeStruct((bsz, N_META, KV_DIM), F32),
            jax.ShapeDtypeStruct((bsz, CHUNK, KV_DIM), F32),
            jax.ShapeDtypeStruct((bsz, CHUNK, KV_DIM), F32),
        ],
        scratch_shapes=[pltpu.VMEM((CHUNK, KV_DIM), BF16) for _ in range(6)] + [
            pltpu.VMEM((N_META, KV_DIM), F32), pltpu.VMEM((N_META, D_ATTN), F32),
        ],
        compiler_params=_cparams(2),
        name="attn_prompt",
    )(proj_main, proj_main, proj_main, proj_tail, proj_tail, proj_tail, cos, sin, *params)


def _attn_sample(proj_tail, mk, mv, wk, wv, cos, sin, params, nseq, t, nsq):
    blk = nsq * t
    per_seq = lambda b: (b, 0, 0)
    in_specs = [
        pl.BlockSpec((blk, D_ATTN), lambda b: (b, COL_Q // D_ATTN)),
        pl.BlockSpec((blk, KV_DIM), lambda b: (b, COL_K // KV_DIM)),
        pl.BlockSpec((blk, KV_DIM), lambda b: (b, COL_V // KV_DIM)),
        pl.BlockSpec((nsq, N_META, KV_DIM), per_seq),
        pl.BlockSpec((nsq, N_META, KV_DIM), per_seq),
        pl.BlockSpec((nsq, WINDOW, KV_DIM), per_seq),
        pl.BlockSpec((nsq, WINDOW, KV_DIM), per_seq),
        pl.BlockSpec((blk, LANES), lambda b: (0, 0)),
        pl.BlockSpec((blk, LANES), lambda b: (0, 0)),
    ] + _param_specs(params)
    return pl.pallas_call(
        _attn_sample_kernel,
        grid=(nseq // nsq,),
        in_specs=in_specs,
        out_specs=[
            pl.BlockSpec((blk, D_ATTN), lambda b: (b, 0)),
            pl.BlockSpec((blk, KV_DIM), lambda b: (b, 0)),
        ],
        out_shape=[
            jax.ShapeDtypeStruct((nseq * t, D_ATTN), BF16),
            jax.ShapeDtypeStruct((nseq * t, KV_DIM), F32),
        ],
        scratch_shapes=[pltpu.VMEM((blk, D_ATTN), F32)],
        compiler_params=_cparams(1),
        name="attn_sample",
    )(proj_tail, proj_tail, proj_tail, mk, mv, wk, wv, cos, sin, *params)


def _mix_out(n_main, x_refs, ys_refs, ya_refs, wo_ref):
    return (_pick(n_main, *x_refs) + _dot(_pick(n_main, *ys_refs), wo_ref[0:D_SSD, :])
            + _dot(_pick(n_main, *ya_refs), wo_ref[D_SSD:D_SSD + D_ATTN, :]))


def _swiglu_acc(hn, wg_ref, wu_ref, wd_ref, acc_ref):
    for j in range(D_FF // FF_CHUNK):
        cols = slice(j * FF_CHUNK, (j + 1) * FF_CHUNK)
        a = (_silu(_dot(hn, wg_ref[:, cols])) * _dot(hn, wu_ref[:, cols])).astype(BF16)
        acc_ref[...] += _dot(a, wd_ref[cols, :])


def _out_ffn_kernel(n_main, xa, xb, ysa, ysb, yaa, yab, wo_ref, nw_ref, wg_ref, wu_ref, wd_ref,
                    om_ref, ot_ref):
    xm = _mix_out(n_main, (xa, xb), (ysa, ysb), (yaa, yab), wo_ref)
    om_ref[...] = xm
    _swiglu_acc(_rms(xm, nw_ref[...]).astype(BF16), wg_ref, wu_ref, wd_ref, om_ref)
    ot_ref[...] = om_ref[...]


def _out_ffn(x, ys, ya, wo, nw, wg, wu, wd, n_main):
    n_tail = x[1].shape[0] // ROW_TILE
    return pl.pallas_call(
        functools.partial(_out_ffn_kernel, n_main),
        grid=(n_main + n_tail,),
        in_specs=_src_specs(D_MODEL, n_main) + _src_specs(D_SSD, n_main) + _src_specs(D_ATTN, n_main) + [
            _resident(wo.shape), _resident(nw.shape),
            _resident(wg.shape), _resident(wu.shape), _resident(wd.shape),
        ],
        out_specs=_dst_specs(D_MODEL, n_main),
        out_shape=_dst_shapes(D_MODEL, n_main, n_tail, F32),
        compiler_params=_cparams(1),
        name="out_ffn",
    )(*x, *ys, *ya, wo, nw, wg, wu, wd)


def _out_router_kernel(n_main, n_tok, xa, xb, ysa, ysb, yaa, yab, wo_ref, nw_ref, wr_hi_ref, wr_lo_ref,
                       xm_o, rt_o, cnt_o):
    xm = _mix_out(n_main, (xa, xb), (ysa, ysb), (yaa, yab), wo_ref)
    xm_o[...] = xm
    hn = _rms(xm, nw_ref[...])
    hi = hn.astype(BF16)
    lo = (hn - hi.astype(F32)).astype(BF16)
    logits = _dot(hi, wr_hi_ref[...]) + _dot(lo, wr_hi_ref[...]) + _dot(hi, wr_lo_ref[...])
    lane = lax.broadcasted_iota(jnp.int32, logits.shape, 1)
    logits = jnp.where(lane < N_EXPERTS, logits, -jnp.inf)
    v1 = jnp.max(logits, axis=-1, keepdims=True)
    i1 = jnp.min(jnp.where(logits == v1, lane, LANES), axis=-1, keepdims=True)
    rest = jnp.where(lane == i1, -jnp.inf, logits)
    v2 = jnp.max(rest, axis=-1, keepdims=True)
    i2 = jnp.min(jnp.where(rest == v2, lane, LANES), axis=-1, keepdims=True)
    e2 = jnp.exp(v2 - v1)
    g1 = 1.0 / (1.0 + e2)
    g2 = e2 / (1.0 + e2)

    row = pl.program_id(0) * ROW_TILE + lax.broadcasted_iota(jnp.int32, (ROW_TILE, 1), 0)
    valid = row < n_tok
    oh1 = jnp.where(jnp.logical_and(lane == i1, valid), 1.0, 0.0)
    oh2 = jnp.where(jnp.logical_and(lane == i2, valid), 1.0, 0.0)
    rr = lax.broadcasted_iota(jnp.int32, (ROW_TILE, ROW_TILE), 0)
    cc = lax.broadcasted_iota(jnp.int32, (ROW_TILE, ROW_TILE), 1)
    before = jnp.where(cc < rr, 1.0, 0.0).astype(BF16)
    c1 = _dot(before, oh1.astype(BF16))
    c2 = _dot(before, oh2.astype(BF16))
    tot1 = jnp.sum(oh1, axis=0, keepdims=True)
    tot2 = jnp.sum(oh2, axis=0, keepdims=True)
    rank1 = jnp.sum(jnp.where(lane == i1, c1, 0.0), axis=-1, keepdims=True)
    rank2 = jnp.sum(jnp.where(lane == i2, c2 + tot1, 0.0), axis=-1, keepdims=True)
    cnt_o[...] = jnp.broadcast_to(tot1 + tot2, cnt_o.shape)
    route = jnp.where(lane == 0, i1.astype(F32), 0.0)
    for k, val in enumerate((i2.astype(F32), g1, g2, rank1, rank2)):
        route = jnp.where(lane == k + 1, val, route)
    rt_o[...] = route


def _out_router(x, ys, ya, wo, nw, wr_hi, wr_lo, n_main, n_tok):
    n_tiles = n_main + x[1].shape[0] // ROW_TILE
    rows = n_tiles * ROW_TILE
    return pl.pallas_call(
        functools.partial(_out_router_kernel, n_main, n_tok),
        grid=(n_tiles,),
        in_specs=_src_specs(D_MODEL, n_main) + _src_specs(D_SSD, n_main) + _src_specs(D_ATTN, n_main) + [
            _resident(wo.shape), _resident(nw.shape), _resident(wr_hi.shape), _resident(wr_lo.shape),
        ],
        out_specs=[
            pl.BlockSpec((ROW_TILE, D_MODEL), lambda i: (i, 0)),
            pl.BlockSpec((ROW_TILE, LANES), lambda i: (i, 0)),
            pl.BlockSpec((None, SUBLANES, LANES), lambda i: (i, 0, 0)),
        ],
        out_shape=[
            jax.ShapeDtypeStruct((rows, D_MODEL), F32),
            jax.ShapeDtypeStruct((rows, LANES), F32),
            jax.ShapeDtypeStruct((n_tiles, SUBLANES, LANES), F32),
        ],
        compiler_params=_cparams(1),
        name="out_router",
    )(*x, *ys, *ya, wo, nw, wr_hi, wr_lo)


def _tile_rows(idx, n_tiles, tail, fn):
    if tail == ROW_TILE:
        fn(ROW_TILE)
    else:
        pl.when(idx < n_tiles - 1)(lambda: fn(ROW_TILE))
        pl.when(idx == n_tiles - 1)(lambda: fn(tail))


def _dispatch_kernel(n_tiles, tail, dest_ref, x_ref, xs_in_ref, xs_ref, sem):
    del xs_in_ref

    def run(nrows):
        def body(r, carry):
            for k in range(TOP_K):
                d = dest_ref[0, 0, TOP_K * r + k]
                pltpu.make_async_copy(x_ref.at[pl.ds(r, 1)], xs_ref.at[pl.ds(d, 1)], sem).start()
            return carry

        lax.fori_loop(0, nrows, body, 0, unroll=DMA_UNROLL)
        for k in range(TOP_K):
            pltpu.make_async_copy(x_ref.at[pl.ds(0, nrows)], xs_ref.at[pl.ds(0, nrows)], sem).wait()

    _tile_rows(pl.program_id(0), n_tiles, tail, run)


def _dispatch(dest, xm, n_tok, m_rows):
    n_tiles = xm.shape[0] // ROW_TILE
    tail = n_tok - (n_tiles - 1) * ROW_TILE
    return pl.pallas_call(
        functools.partial(_dispatch_kernel, n_tiles, tail),
        grid=(n_tiles,),
        in_specs=[
            pl.BlockSpec((1, 1, TOP_K * ROW_TILE), lambda i: (i, 0, 0), memory_space=pltpu.SMEM),
            pl.BlockSpec((ROW_TILE, D_MODEL), lambda i: (i, 0)),
            pl.BlockSpec(memory_space=pl.ANY),
        ],
        out_specs=pl.BlockSpec(memory_space=pl.ANY),
        out_shape=jax.ShapeDtypeStruct((m_rows, D_MODEL), F32),
        scratch_shapes=[pltpu.SemaphoreType.DMA(())],
        input_output_aliases={2: 0},
        compiler_params=_cparams(1, has_side_effects=True),
        name="moe_dispatch",
    )(dest, xm, jnp.zeros((m_rows, D_MODEL), F32))


def _moe_kernel(te_ref, nu_ref, x_ref, nw_ref, wg_ref, wu_ref, wd_ref, o_ref):
    i = pl.program_id(0)
    o_ref[...] = jnp.zeros(o_ref.shape, F32)

    @pl.when(i < nu_ref[0])
    def _():
        _swiglu_acc(_rms(x_ref[...], nw_ref[...]).astype(BF16), wg_ref, wu_ref, wd_ref, o_ref)


def _moe_experts(tile_e, n_used, xs, nw, wg, wu, wd):
    rows = xs.shape[0]
    grid_spec = pltpu.PrefetchScalarGridSpec(
        num_scalar_prefetch=2,
        grid=(rows // ROW_TILE,),
        in_specs=[
            pl.BlockSpec((ROW_TILE, D_MODEL), lambda i, te, nu: (i, 0)),
            pl.BlockSpec((1, D_MODEL), lambda i, te, nu: (0, 0)),
            pl.BlockSpec((None, D_MODEL, D_FF), lambda i, te, nu: (te[i], 0, 0)),
            pl.BlockSpec((None, D_MODEL, D_FF), lambda i, te, nu: (te[i], 0, 0)),
            pl.BlockSpec((None, D_FF, D_MODEL), lambda i, te, nu: (te[i], 0, 0)),
        ],
        out_specs=pl.BlockSpec((ROW_TILE, D_MODEL), lambda i, te, nu: (i, 0)),
    )
    return pl.pallas_call(
        _moe_kernel,
        grid_spec=grid_spec,
        out_shape=jax.ShapeDtypeStruct((rows, D_MODEL), F32),
        compiler_params=_cparams(1),
        name="moe_experts",
    )(tile_e, n_used, xs, nw, wg, wu, wd)


def _combine_kernel(n_main, n_tiles, tail, dcur_ref, dnext_ref, xm_ref, rt_ref, yb_ref,
                    om_ref, ot_ref, gbuf, sem):
    i = pl.program_id(0)

    def issue(dref, slot, nrows):
        def body(r, carry):
            for k in range(TOP_K):
                d = dref[0, 0, TOP_K * r + k]
                pltpu.make_async_copy(yb_ref.at[pl.ds(d, 1)], gbuf.at[slot, k, pl.ds(r, 1)],
                                      sem.at[slot]).start()
            return carry

        lax.fori_loop(0, nrows, body, 0, unroll=DMA_UNROLL)

    def wait(slot, nrows):
        for k in range(TOP_K):
            pltpu.make_async_copy(yb_ref.at[pl.ds(0, nrows)], gbuf.at[slot, k, pl.ds(0, nrows)],
                                  sem.at[slot]).wait()

    @pl.when(i == 0)
    def _():
        gbuf[...] = jnp.zeros(gbuf.shape, F32)
        _tile_rows(i, n_tiles, tail, lambda n: issue(dcur_ref, 0, n))

    @pl.when(i + 1 < n_tiles)
    def _():
        _tile_rows(i + 1, n_tiles, tail, lambda n: issue(dnext_ref, (i + 1) % 2, n))

    slot = i % 2
    _tile_rows(i, n_tiles, tail, lambda n: wait(slot, n))
    gates = rt_ref[...]
    val = (xm_ref[...] + gates[:, TOP_K:TOP_K + 1] * gbuf[slot, 0]
           + gates[:, TOP_K + 1:TOP_K + 2] * gbuf[slot, 1])

    @pl.when(i < n_main)
    def _():
        om_ref[...] = val

    @pl.when(i >= n_main)
    def _():
        ot_ref[...] = val


def _combine(dest, xm, route, yb, n_main, n_tok):
    n_tiles = xm.shape[0] // ROW_TILE
    tail = n_tok - (n_tiles - 1) * ROW_TILE
    dspec = lambda f: pl.BlockSpec((1, 1, TOP_K * ROW_TILE), f, memory_space=pltpu.SMEM)
    return pl.pallas_call(
        functools.partial(_combine_kernel, n_main, n_tiles, tail),
        grid=(n_tiles,),
        in_specs=[
            dspec(lambda i: (i, 0, 0)),
            dspec(lambda i: (jnp.minimum(i + 1, n_tiles - 1), 0, 0)),
            pl.BlockSpec((ROW_TILE, D_MODEL), lambda i: (i, 0)),
            pl.BlockSpec((ROW_TILE, LANES), lambda i: (i, 0)),
            pl.BlockSpec(memory_space=pl.ANY),
        ],
        out_specs=_src_specs(D_MODEL, n_main),
        out_shape=[jax.ShapeDtypeStruct((n_main * ROW_TILE, D_MODEL), F32),
                   jax.ShapeDtypeStruct(((n_tiles - n_main) * ROW_TILE, D_MODEL), F32)],
        scratch_shapes=[pltpu.VMEM((2, TOP_K, ROW_TILE, D_MODEL), F32), pltpu.SemaphoreType.DMA((2,))],
        compiler_params=_cparams(1),
        name="moe_combine",
    )(dest, dest, xm, route, yb)


def _moe_layer(xm, route, counts, n_main, n_tok, nw, wg, wu, wd):
    n_tiles = xm.shape[0] // ROW_TILE
    m_tiles = -(-(n_tok * TOP_K + N_EXPERTS * (ROW_TILE - 1)) // ROW_TILE)
    cnt = counts[:, 0, :N_EXPERTS].astype(jnp.int32)
    total = jnp.sum(cnt, axis=0)
    padded = (total + ROW_TILE - 1) // ROW_TILE * ROW_TILE
    pad_end = jnp.cumsum(padded)
    base = (pad_end - padded)[None, :] + jnp.cumsum(cnt, axis=0) - cnt
    e = route[:, 0:TOP_K].astype(jnp.int32).reshape(n_tiles, ROW_TILE, TOP_K)
    rank = route[:, 2 * TOP_K:3 * TOP_K].astype(jnp.int32).reshape(n_tiles, ROW_TILE, TOP_K)
    onehot = e[..., None] == jnp.arange(N_EXPERTS, dtype=jnp.int32)
    dest = jnp.sum(jnp.where(onehot, base[:, None, None, :], 0), axis=-1) + rank
    dest = dest.reshape(n_tiles, 1, ROW_TILE * TOP_K)
    tile_e = jnp.minimum(jnp.searchsorted(pad_end, jnp.arange(m_tiles, dtype=jnp.int32) * ROW_TILE,
                                          side='right'), N_EXPERTS - 1).astype(jnp.int32)
    n_used = (pad_end[-1:] // ROW_TILE).astype(jnp.int32)
    xs = _dispatch(dest, xm, n_tok, m_tiles * ROW_TILE)
    yb = _moe_experts(tile_e, n_used, xs, nw, wg, wu, wd)
    return _combine(dest, xm, route, yb, n_main, n_tok)


def _rope_tables(pos):
    half = HEAD_DIM // 2
    inv_freq = ROPE_THETA ** (-jnp.arange(half, dtype=F32) / half)
    ang = pos.astype(F32)[:, None] * inv_freq[None, :]
    cos = jnp.cos(ang)
    sin = jnp.sin(ang)
    reps = LANES // HEAD_DIM
    return (jnp.tile(jnp.concatenate([cos, cos], axis=-1), (1, reps)),
            jnp.tile(jnp.concatenate([-sin, sin], axis=-1), (1, reps)))


def _pad_lanes(v, width=LANES):
    v = v.astype(F32).reshape(1, -1)
    return jnp.pad(v, ((0, 0), (0, width - v.shape[1])))


def kernel(x_prompt, x_sample, state_ssm, state_conv, cache_meta_k, cache_meta_v, cache_win_k, cache_win_v, meta_tokens, norm_mix_w, w_in, conv_w, conv_b, dt_bias, a_log, d_skip, ssd_norm_w, q_norm_w, k_norm_w, sinks, attn_norm_w, w_out, norm_ffn_w, w_gate, w_up, w_down, w_router, moe_w_gate, moe_w_up, moe_w_down):
    bsz, seq, _ = x_prompt.shape
    nseq, t_s, _ = x_sample.shape
    depth = w_in.shape[0]
    r_main = bsz * seq
    r_samp = nseq * t_s
    assert seq % CHUNK == 0 and t_s == SUBLANES and r_main % ROW_TILE == 0 and r_samp % N_META == 0
    n_main = r_main // ROW_TILE
    n_tok = r_main + r_samp + N_META
    r_tail = -(-(r_samp + N_META) // ROW_TILE) * ROW_TILE
    tail_pad = r_tail - r_samp - N_META

    x = (x_prompt.reshape(r_main, D_MODEL),
         jnp.concatenate([x_sample.reshape(r_samp, D_MODEL), meta_tokens.astype(F32),
                          jnp.zeros((tail_pad, D_MODEL), F32)], axis=0))

    cos_p, sin_p = _rope_tables(jnp.arange(seq + CHUNK, dtype=jnp.int32) - (CHUNK - N_META))
    nsq = 16 if nseq % 16 == 0 else nseq
    cos_s, sin_s = (jnp.tile(tab, (nsq, 1)) for tab in
                    _rope_tables(PAST_LEN + jnp.arange(t_s, dtype=jnp.int32)))

    o_z, o_xbc, o_dt, o_q, o_k, o_v = 0, 512, 1536, 1544, 2056, 2184
    col = jnp.arange(D_ATTN, dtype=jnp.int32)
    grp, lane = col // LANES, col % LANES
    head_perm = (grp + Q_PER_KV * (lane // HEAD_DIM)) * HEAD_DIM + lane % HEAD_DIM

    def heads4(a):
        return a.reshape(a.shape[0], a.shape[1], N_KV_HEADS, HEAD_DIM)

    outs = {k: [] for k in ('p_ssm', 'p_conv', 'p_mk', 'p_mv', 'p_wk', 'p_wv', 's_ssm', 's_conv', 's_wk', 's_wv')}
    for l in range(depth):
        wl = w_in[l]
        w_re = jnp.concatenate([
            wl[:, o_xbc:o_xbc + CONV_DIM], wl[:, o_z:o_z + D_SSD], wl[:, o_q:o_q + D_ATTN][:, head_perm],
            wl[:, o_k:o_k + KV_DIM], wl[:, o_v:o_v + KV_DIM], wl[:, o_dt:o_dt + N_SSD_HEADS],
            jnp.zeros((D_MODEL, PROJ_W - COL_DT - N_SSD_HEADS), wl.dtype)], axis=1).astype(BF16)
        proj_main, proj_tail = _in_proj(x[0], x[1], norm_mix_w[l].reshape(1, D_MODEL).astype(F32), w_re, n_main)

        ssd_params = (conv_w[l].astype(F32), conv_b[l].reshape(1, CONV_DIM).astype(F32),
                      _pad_lanes(dt_bias[l]), _pad_lanes(-jnp.exp(a_log[l].astype(F32))),
                      _pad_lanes(d_skip[l]), ssd_norm_w[l].reshape(1, D_SSD).astype(F32))
        ys_p, ys_m, ssm_p, conv_p = _ssd_prompt(proj_main, proj_tail, ssd_params, bsz, seq, r_samp)
        conv_prev = jnp.pad(state_conv[l].astype(F32), ((0, 0), (SUBLANES - (CONV_W - 1), 0), (0, 0)))
        ys_s, ssm_s = _ssd_sample(proj_tail, conv_prev, state_ssm[l].astype(F32), ssd_params, nseq, t_s, nsq)

        attn_params = (jnp.tile(q_norm_w[l].astype(F32), N_Q_HEADS).reshape(1, D_ATTN),
                       jnp.tile(k_norm_w[l].astype(F32), N_KV_HEADS).reshape(1, KV_DIM),
                       _pad_lanes(sinks[l]), attn_norm_w[l][head_perm].reshape(1, D_ATTN).astype(F32))
        ya_p, ya_m, kp_m, k_last, v_last = _attn_prompt(proj_main, proj_tail, cos_p, sin_p, attn_params,
                                                        bsz, seq, r_samp)
        ya_s, kp_s = _attn_sample(
            proj_tail, cache_meta_k[l].reshape(nseq, N_META, KV_DIM).astype(F32),
            cache_meta_v[l].reshape(nseq, N_META, KV_DIM).astype(F32),
            cache_win_k[l].reshape(nseq, WINDOW, KV_DIM).astype(F32),
            cache_win_v[l].reshape(nseq, WINDOW, KV_DIM).astype(F32),
            cos_s, sin_s, attn_params, nseq, t_s, nsq)

        ys = (ys_p, jnp.concatenate([ys_s, ys_m[0], jnp.zeros((tail_pad, D_SSD), BF16)], axis=0))
        ya = (ya_p, jnp.concatenate([ya_s, ya_m[0], jnp.zeros((tail_pad, D_ATTN), BF16)], axis=0))

        wo = jnp.concatenate([w_out[l][:D_SSD], w_out[l][D_SSD:][head_perm]], axis=0).astype(BF16)
        nfw = norm_ffn_w[l].reshape(1, D_MODEL).astype(F32)
        i = l // 2
        if l % 2 == 0:
            x = _out_ffn(x, ys, ya, wo, nfw, w_gate[i].astype(BF16), w_up[i].astype(BF16),
                         w_down[i].astype(BF16), n_main)
        else:
            wr = jnp.pad(w_router[i].astype(F32), ((0, 0), (0, LANES - N_EXPERTS)))
            wr_hi = wr.astype(BF16)
            wr_lo = (wr - wr_hi.astype(F32)).astype(BF16)
            xm, route, counts = _out_router(x, ys, ya, wo, nfw, wr_hi, wr_lo, n_main, n_tok)
            x = _moe_layer(xm, route, counts, n_main, n_tok, nfw, moe_w_gate[i].astype(BF16),
                           moe_w_up[i].astype(BF16), moe_w_down[i].astype(BF16))

        samp = proj_tail[:r_samp]
        xbc_s = samp[:, COL_XBC:COL_XBC + CONV_DIM].reshape(nseq, t_s, CONV_DIM)
        v_s = samp[:, COL_V:COL_V + KV_DIM].reshape(nseq, t_s, KV_DIM)
        v_meta = proj_tail[r_samp:r_samp + N_META, COL_V:COL_V + KV_DIM]
        meta_shape = (bsz, N_META, N_KV_HEADS, HEAD_DIM)
        outs['p_ssm'].append(ssm_p)
        outs['p_conv'].append(conv_p[:, SUBLANES - (CONV_W - 1):])
        outs['p_mk'].append(jnp.broadcast_to(heads4(kp_m[0:1]), meta_shape))
        outs['p_mv'].append(jnp.broadcast_to(heads4(v_meta[None]), meta_shape))
        outs['p_wk'].append(heads4(k_last))
        outs['p_wv'].append(heads4(v_last))
        outs['s_ssm'].append(ssm_s)
        outs['s_conv'].append(jnp.concatenate([state_conv[l].astype(F32), xbc_s], axis=1)[:, t_s:])
        outs['s_wk'].append(jnp.concatenate([cache_win_k[l].astype(F32),
                                             heads4(kp_s.reshape(nseq, t_s, KV_DIM))], axis=1)[:, t_s:])
        outs['s_wv'].append(jnp.concatenate([cache_win_v[l].astype(F32), heads4(v_s)], axis=1)[:, t_s:])

    y_prompt = x[0][:r_main].reshape(bsz, seq, D_MODEL)
    y_sample = x[1][:r_samp].reshape(nseq, t_s, D_MODEL)
    st = lambda k: jnp.stack(outs[k])
    return (y_prompt, y_sample, st('p_ssm'), st('p_conv'), st('p_mk'), st('p_mv'), st('p_wk'), st('p_wv'),
            st('s_ssm'), st('s_conv'), st('s_wk'), st('s_wv'))
```

```python
import functools

import jax
import jax.numpy as jnp
from jax import lax
from jax.experimental import pallas as pl
from jax.experimental.pallas import tpu as pltpu

F32 = jnp.float32
BF16 = jnp.bfloat16

D_MODEL = 1024
D_SSD = 512
SSD_HEAD_DIM = 64
N_SSD_HEADS = 8
SSD_HEADS_PER_GROUP = 4
N_SSD_GROUPS = 2
D_STATE = 128
CONV_W = 4
CONV_DIM = 1024
D_ATTN = 512
HEAD_DIM = 64
N_Q_HEADS = 8
N_KV_HEADS = 2
Q_PER_KV = 4
KV_DIM = 128
WINDOW = 128
N_META = 16
D_FF = 2816
N_EXPERTS = 8
TOP_K = 2
EPS = 1e-6
NEG = -1e30
ATTN_SCALE = HEAD_DIM ** -0.5
PAST_LEN = 16384
ROPE_THETA = 10000.0

LANES = 128
SUBLANES = 8
CHUNK = 128
ROW_TILE = 512
FF_CHUNK = 256
DMA_UNROLL = 8
VMEM_LIMIT = 60 * 1024 * 1024

COL_XBC = 0
COL_Z = 1024
COL_Q = 1536
COL_K = 2048
COL_V = 2176
COL_DT = 2304
PROJ_W = 2560


def _dot(a, b):
    return jnp.dot(a, b, preferred_element_type=F32)


def _dot_nt(a, b):
    return lax.dot_general(a, b, (((1,), (1,)), ((), ())), preferred_element_type=F32)


def _dot_tn(a, b):
    return lax.dot_general(a, b, (((0,), (0,)), ((), ())), preferred_element_type=F32)


def _rms(x, w):
    return x * lax.rsqrt(jnp.mean(x * x, axis=-1, keepdims=True) + EPS) * w


def _silu(x):
    return x * jax.nn.sigmoid(x)


def _split3(x):
    p1 = x.astype(BF16)
    r1 = x - p1.astype(F32)
    p2 = r1.astype(BF16)
    p3 = (r1 - p2.astype(F32)).astype(BF16)
    return p1, p2, p3


def _dot_exact(sel, x):
    p1, p2, p3 = _split3(x)
    return _dot(sel, p1) + _dot(sel, p2) + _dot(sel, p3)


def _cparams(ndim, **kw):
    return pltpu.CompilerParams(dimension_semantics=("arbitrary",) * ndim, vmem_limit_bytes=VMEM_LIMIT, **kw)


def _src_specs(width, n_main):
    return [pl.BlockSpec((ROW_TILE, width), lambda i: (jnp.minimum(i, n_main - 1), 0)),
            pl.BlockSpec((ROW_TILE, width), lambda i: (jnp.maximum(i - n_main, 0), 0))]


def _pick(n_main, main_ref, tail_ref):
    dtype = main_ref.dtype
    picked = jnp.where(pl.program_id(0) < n_main, main_ref[...].astype(F32), tail_ref[...].astype(F32))
    return picked.astype(dtype)


def _dst_specs(width, n_main):
    return [pl.BlockSpec((ROW_TILE, width), lambda i: (jnp.minimum(i, n_main), 0)),
            pl.BlockSpec((ROW_TILE, width), lambda i: (jnp.maximum(i - n_main, 0), 0))]


def _dst_shapes(width, n_main, n_tail, dtype):
    return [jax.ShapeDtypeStruct(((n_main + 1) * ROW_TILE, width), dtype),
            jax.ShapeDtypeStruct((n_tail * ROW_TILE, width), dtype)]


def _resident(shape):
    nd = len(shape)
    return pl.BlockSpec(shape, lambda *a: (0,) * nd, pipeline_mode=pl.Buffered(1))


def _param_specs(params):
    return [pl.BlockSpec(p.shape, lambda *a: (0, 0)) for p in params]


def _in_proj_kernel(n_main, xa_ref, xb_ref, nw_ref, w_ref, om_ref, ot_ref):
    xn = _rms(_pick(n_main, xa_ref, xb_ref), nw_ref[...]).astype(BF16)
    for j in range(PROJ_W // 512):
        cols = slice(j * 512, (j + 1) * 512)
        r = _dot(xn, w_ref[:, cols])
        om_ref[:, cols] = r
        ot_ref[:, cols] = r


def _in_proj(x_main, x_tail, nw, w, n_main):
    n_tail = x_tail.shape[0] // ROW_TILE
    return pl.pallas_call(
        functools.partial(_in_proj_kernel, n_main),
        grid=(n_main + n_tail,),
        in_specs=_src_specs(D_MODEL, n_main) + [
            pl.BlockSpec((1, D_MODEL), lambda i: (0, 0)),
            pl.BlockSpec((D_MODEL, PROJ_W), lambda i: (0, 0)),
        ],
        out_specs=_dst_specs(PROJ_W, n_main),
        out_shape=_dst_shapes(PROJ_W, n_main, n_tail, F32),
        compiler_params=_cparams(1),
        name="in_proj",
    )(x_main, x_tail, nw, w)


def _ssd_chunk(xbc, z, dt_raw, valid, cw_ref, cb_ref, dtb_ref, an_ref, dsk_ref, nw_ref, cbuf, hst):
    q = CHUNK
    cbuf[SUBLANES:SUBLANES + q, :] = xbc
    acc = cb_ref[...]
    for k in range(CONV_W):
        off = SUBLANES - (CONV_W - 1) + k
        acc = acc + cbuf[off:off + q, :] * cw_ref[k:k + 1, :]
    cbuf[0:SUBLANES, :] = cbuf[q:q + SUBLANES, :]
    xc = _silu(acc)
    xs = xc[:, :D_SSD]
    bm = xc[:, D_SSD:D_SSD + N_SSD_GROUPS * D_STATE]
    cm = xc[:, D_SSD + N_SSD_GROUPS * D_STATE:]

    dt = jax.nn.softplus(dt_raw + dtb_ref[...])
    if valid is not None:
        dt = jnp.where(valid, dt, 0.0)
    da = dt * an_ref[...]
    row_i = lax.broadcasted_iota(jnp.int32, (q, q), 0)
    col_j = lax.broadcasted_iota(jnp.int32, (q, q), 1)
    tril = row_i >= col_j
    cs = _dot_exact(jnp.where(tril, 1.0, 0.0).astype(BF16), da)
    cs_t = cs.T
    last = cs[q - 1:q, :]
    ecl = jnp.exp(last)
    xdt = xs * _expand_heads(dt)
    xw = xdt * _expand_heads(jnp.exp(last - cs))
    ecs_x = _expand_heads(jnp.exp(cs))
    skip = xs * dsk_ref[...]

    gw = SSD_HEADS_PER_GROUP * SSD_HEAD_DIM
    head_of_col = jnp.right_shift(lax.broadcasted_iota(jnp.int32, (q, gw), 1), SSD_HEAD_DIM.bit_length() - 1)
    ys = []
    for g in range(N_SSD_GROUPS):
        bg = bm[:, g * D_STATE:(g + 1) * D_STATE].astype(BF16)
        cg = cm[:, g * D_STATE:(g + 1) * D_STATE].astype(BF16)
        cb = _dot_nt(cg, bg)
        gc = slice(g * gw, (g + 1) * gw)
        h_prev = hst[gc, :]
        y_g = _dot_nt(cg, h_prev.astype(BF16)) * ecs_x[:, gc] + skip[:, gc]
        xdt_g = xdt[:, gc]
        for r in range(SSD_HEADS_PER_GROUP):
            h = g * SSD_HEADS_PER_GROUP + r
            seg = cs[:, h:h + 1] - cs_t[h:h + 1, :]
            decay = jnp.where(tril, jnp.exp(jnp.where(tril, seg, 0.0)), 0.0)
            y_g = y_g + _dot((decay * cb).astype(BF16), jnp.where(head_of_col == r, xdt_g, 0.0).astype(BF16))
        ys.append(y_g)
        ecl_col = jnp.concatenate(
            [jnp.broadcast_to(ecl[:, h:h + 1], (SSD_HEAD_DIM, 1))
             for h in range(g * SSD_HEADS_PER_GROUP, (g + 1) * SSD_HEADS_PER_GROUP)], axis=0)
        hst[gc, :] = h_prev * ecl_col + _dot_tn(xw[:, gc].astype(BF16), bg)

    return _ssd_gate_norm(jnp.concatenate(ys, axis=1), z, nw_ref)


def _ssd_gate_norm(y, z, nw_ref):
    y = y * _silu(z)
    gs = D_SSD // N_SSD_GROUPS
    return jnp.concatenate([_rms(y[:, g * gs:(g + 1) * gs], nw_ref[:, g * gs:(g + 1) * gs])
                            for g in range(N_SSD_GROUPS)], axis=-1)


def _ssd_prompt_kernel(xbc_m, z_m, dt_m, xbc_t, z_t, dt_t, cw, cb, dtb, an, dsk, nw,
                       y_o, ym_o, hf_o, ct_o, cbuf, hst, ymbuf):
    c = pl.program_id(1)
    pad = CHUNK - N_META
    prm = (cw, cb, dtb, an, dsk, nw)

    @pl.when(c == 0)
    def _():
        cbuf[0:SUBLANES, :] = jnp.zeros((SUBLANES, CONV_DIM), F32)
        hst[...] = jnp.zeros(hst.shape, F32)

        def stage(meta_ref):
            return jnp.concatenate([jnp.zeros((pad, meta_ref.shape[1]), F32), meta_ref[...]], axis=0)

        row = lax.broadcasted_iota(jnp.int32, (CHUNK, 1), 0)
        y = _ssd_chunk(stage(xbc_t), stage(z_t), stage(dt_t), row >= pad, *prm, cbuf, hst)
        ymbuf[...] = y[pad:, :]

    @pl.when(c > 0)
    def _():
        for j in range(xbc_m.shape[0] // CHUNK):
            rows = slice(j * CHUNK, (j + 1) * CHUNK)
            y = _ssd_chunk(xbc_m[rows, :], z_m[rows, :], dt_m[rows, :], None, *prm, cbuf, hst)
            y_o[rows, :] = y.astype(BF16)

    ym_o[...] = ymbuf[...].astype(BF16)
    hf_o[...] = hst[...].reshape(hf_o.shape)
    ct_o[...] = cbuf[0:SUBLANES, :]


def _chunks_per_step(nc):
    return 4 if nc % 4 == 0 else (2 if nc % 2 == 0 else 1)


def _ssd_prompt(proj_main, proj_tail, params, bsz, seq, meta_row):
    cps = _chunks_per_step(seq // CHUNK)
    blk = cps * CHUNK
    nb = seq // blk
    mb = meta_row // N_META

    def main(col):
        return lambda b, c: (b * nb + jnp.maximum(c - 1, 0), col)

    def meta(col):
        return lambda b, c: (mb, col)

    in_specs = [
        pl.BlockSpec((blk, CONV_DIM), main(COL_XBC // CONV_DIM)),
        pl.BlockSpec((blk, D_SSD), main(COL_Z // D_SSD)),
        pl.BlockSpec((blk, LANES), main(COL_DT // LANES)),
        pl.BlockSpec((N_META, CONV_DIM), meta(COL_XBC // CONV_DIM)),
        pl.BlockSpec((N_META, D_SSD), meta(COL_Z // D_SSD)),
        pl.BlockSpec((N_META, LANES), meta(COL_DT // LANES)),
    ] + _param_specs(params)
    return pl.pallas_call(
        _ssd_prompt_kernel,
        grid=(bsz, nb + 1),
        in_specs=in_specs,
        out_specs=[
            pl.BlockSpec((blk, D_SSD), main(0)),
            pl.BlockSpec((None, N_META, D_SSD), lambda b, c: (b, 0, 0)),
            pl.BlockSpec((None, N_SSD_HEADS, SSD_HEAD_DIM, D_STATE), lambda b, c: (b, 0, 0, 0)),
            pl.BlockSpec((None, SUBLANES, CONV_DIM), lambda b, c: (b, 0, 0)),
        ],
        out_shape=[
            jax.ShapeDtypeStruct((bsz * seq, D_SSD), BF16),
            jax.ShapeDtypeStruct((bsz, N_META, D_SSD), BF16),
            jax.ShapeDtypeStruct((bsz, N_SSD_HEADS, SSD_HEAD_DIM, D_STATE), F32),
            jax.ShapeDtypeStruct((bsz, SUBLANES, CONV_DIM), F32),
        ],
        scratch_shapes=[pltpu.VMEM((CHUNK + SUBLANES, CONV_DIM), F32),
                        pltpu.VMEM((N_SSD_HEADS * SSD_HEAD_DIM, D_STATE), F32),
                        pltpu.VMEM((N_META, D_SSD), F32)],
        compiler_params=_cparams(2),
        name="ssd_prompt",
    )(proj_main, proj_main, proj_main, proj_tail, proj_tail, proj_tail, *params)


def _expand_heads(a):
    hh = lax.broadcasted_iota(jnp.int32, (LANES, D_SSD), 0)
    cc = lax.broadcasted_iota(jnp.int32, (LANES, D_SSD), 1)
    sel = jnp.where(jnp.right_shift(cc, SSD_HEAD_DIM.bit_length() - 1) == hh, 1.0, 0.0).astype(BF16)
    p1, p2, p3 = _split3(a)
    return _dot(p1, sel) + _dot(p2, sel) + _dot(p3, sel)


def _pad_rows_bf16(x, rows):
    return jnp.concatenate([x, jnp.zeros((rows - x.shape[0], x.shape[1]), F32)], axis=0).astype(BF16)


def _ssd_sample_kernel(xbc_s, z_s, dt_s, cprev, h0, cw, cb, dtb, an, dsk, nw,
                       y_o, hf_o, ubuf, ybuf):
    nsq = cprev.shape[0]
    t = SUBLANES
    rows = nsq * t
    x = xbc_s[...]
    for s in range(nsq):
        ubuf[2 * t * s:2 * t * s + t, :] = cprev[s]
        ubuf[2 * t * s + t:2 * t * (s + 1), :] = x[s * t:(s + 1) * t, :]
    parts = []
    for s in range(nsq):
        acc = cb[...]
        for k in range(CONV_W):
            off = 2 * t * s + t - (CONV_W - 1) + k
            acc = acc + ubuf[off:off + t, :] * cw[k:k + 1, :]
        parts.append(acc)
    xc = _silu(jnp.concatenate(parts, axis=0))
    xs = xc[:, :D_SSD]
    bm = xc[:, D_SSD:D_SSD + N_SSD_GROUPS * D_STATE]
    cm = xc[:, D_SSD + N_SSD_GROUPS * D_STATE:]

    dt = jax.nn.softplus(dt_s[...] + dtb[...])
    da = dt * an[...]
    ri = lax.broadcasted_iota(jnp.int32, (rows, rows), 0)
    cj = lax.broadcasted_iota(jnp.int32, (rows, rows), 1)
    mask = jnp.logical_and(ri >= cj, jnp.right_shift(ri, 3) == jnp.right_shift(cj, 3))
    cs = _dot_exact(jnp.where(mask, 1.0, 0.0).astype(BF16), da)
    last = _dot_exact(jnp.where(cj == jnp.bitwise_or(ri, t - 1), 1.0, 0.0).astype(BF16), cs)
    cs_t = cs.T
    ecl = jnp.exp(last)
    xdt = xs * _expand_heads(dt)
    xw = xdt * _expand_heads(jnp.exp(last - cs))
    ecs_x = _expand_heads(jnp.exp(cs))
    skip = xs * dsk[...]

    gw = SSD_HEADS_PER_GROUP * SSD_HEAD_DIM
    for g in range(N_SSD_GROUPS):
        bg_f = bm[:, g * D_STATE:(g + 1) * D_STATE]
        cg_f = cm[:, g * D_STATE:(g + 1) * D_STATE]
        cbm = _dot_nt(cg_f.astype(BF16), bg_f.astype(BF16))
        for r in range(SSD_HEADS_PER_GROUP):
            h = g * SSD_HEADS_PER_GROUP + r
            hc = slice(h * SSD_HEAD_DIM, (h + 1) * SSD_HEAD_DIM)
            seg = cs[:, h:h + 1] - cs_t[h:h + 1, :]
            decay = jnp.where(mask, jnp.exp(jnp.where(mask, seg, 0.0)), 0.0)
            ybuf[:, hc] = _dot((decay * cbm).astype(BF16), xdt[:, hc].astype(BF16)) + skip[:, hc]
        gc = slice(g * gw, (g + 1) * gw)
        heads = slice(g * SSD_HEADS_PER_GROUP, (g + 1) * SSD_HEADS_PER_GROUP)
        for s in range(nsq):
            rs = slice(s * t, (s + 1) * t)
            hg = h0[s, heads].reshape(gw, D_STATE)
            y_off = _dot_nt(_pad_rows_bf16(cg_f[rs, :], 2 * t), hg.astype(BF16))[0:t, :]
            ybuf[rs, gc] = ybuf[rs, gc] + y_off * ecs_x[rs, gc]
            ecl_col = jnp.concatenate(
                [jnp.broadcast_to(ecl[s * t:s * t + 1, h:h + 1], (SSD_HEAD_DIM, 1))
                 for h in range(heads.start, heads.stop)], axis=0)
            h_new = hg * ecl_col + _dot_tn(_pad_rows_bf16(xw[rs, gc], 2 * t), _pad_rows_bf16(bg_f[rs, :], 2 * t))
            hf_o[s, heads] = h_new.reshape(SSD_HEADS_PER_GROUP, SSD_HEAD_DIM, D_STATE)

    y_o[...] = _ssd_gate_norm(ybuf[...], z_s[...], nw).astype(BF16)


def _ssd_sample(proj_tail, conv_prev, h0, params, nseq, t, nsq):
    blk = nsq * t
    in_specs = [
        pl.BlockSpec((blk, CONV_DIM), lambda b: (b, COL_XBC // CONV_DIM)),
        pl.BlockSpec((blk, D_SSD), lambda b: (b, COL_Z // D_SSD)),
        pl.BlockSpec((blk, LANES), lambda b: (b, COL_DT // LANES)),
        pl.BlockSpec((nsq, SUBLANES, CONV_DIM), lambda b: (b, 0, 0)),
        pl.BlockSpec((nsq, N_SSD_HEADS, SSD_HEAD_DIM, D_STATE), lambda b: (b, 0, 0, 0)),
    ] + _param_specs(params)
    return pl.pallas_call(
        _ssd_sample_kernel,
        grid=(nseq // nsq,),
        in_specs=in_specs,
        out_specs=[
            pl.BlockSpec((blk, D_SSD), lambda b: (b, 0)),
            pl.BlockSpec((nsq, N_SSD_HEADS, SSD_HEAD_DIM, D_STATE), lambda b: (b, 0, 0, 0)),
        ],
        out_shape=[
            jax.ShapeDtypeStruct((nseq * t, D_SSD), BF16),
            jax.ShapeDtypeStruct((nseq, N_SSD_HEADS, SSD_HEAD_DIM, D_STATE), F32),
        ],
        scratch_shapes=[pltpu.VMEM((2 * blk, CONV_DIM), F32), pltpu.VMEM((blk, D_SSD), F32)],
        compiler_params=_cparams(1),
        name="ssd_sample",
    )(proj_tail, proj_tail, proj_tail, conv_prev, h0, *params)


def _qk_prep(x, w, cos, sin):
    lane = lax.broadcasted_iota(jnp.int32, x.shape, 1)
    lo_head = lane < HEAD_DIM
    sq = x * x
    s_lo = jnp.sum(jnp.where(lo_head, sq, 0.0), axis=-1, keepdims=True)
    s_all = jnp.sum(sq, axis=-1, keepdims=True)
    ms = jnp.where(lo_head, s_lo, s_all - s_lo) * (1.0 / HEAD_DIM)
    xn = x * lax.rsqrt(ms + EPS) * w
    half = HEAD_DIM // 2
    first_half = (lane % HEAD_DIM) < half
    partner = jnp.where(first_half, pltpu.roll(xn, LANES - half, 1), pltpu.roll(xn, half, 1))
    return xn * cos + partner * sin


def _q_groups(q, qnw_ref, cos, sin):
    return [_qk_prep(q[:, g * LANES:(g + 1) * LANES], qnw_ref[:, g * LANES:(g + 1) * LANES], cos, sin)
            * ATTN_SCALE for g in range(D_ATTN // LANES)]


def _sink_column(snk_ref, rows):
    return jnp.concatenate([jnp.broadcast_to(snk_ref[:, h:h + 1], (rows, 1)) for h in range(N_Q_HEADS)],
                           axis=0)


def _softmax_weights(pieces, sk):
    top = pieces[0]
    for p in pieces[1:]:
        top = jnp.maximum(top, p)
    m = jnp.maximum(jnp.max(top, axis=-1, keepdims=True), sk)
    e = [jnp.exp(p - m) for p in pieces]
    tot = e[0]
    for p in e[1:]:
        tot = tot + p
    return e, 1.0 / (jnp.sum(tot, axis=-1, keepdims=True) + jnp.exp(sk - m))


def _kv_prep(k, v, knw_ref, cos, sin):
    kp = _qk_prep(k, knw_ref[...], cos, sin)
    lane = lax.broadcasted_iota(jnp.int32, kp.shape, 1)
    return (kp, jnp.where(lane < HEAD_DIM, kp, 0.0).astype(BF16),
            jnp.where(lane < HEAD_DIM, 0.0, kp).astype(BF16), v.astype(BF16))


def _head_scores(qs, key_lo, key_hi):
    return jnp.concatenate([_dot_nt(qs, key_lo), _dot_nt(qs, key_hi)], axis=0)


def _attn_out(o2, anw_ref):
    nst = o2.shape[0] // 2
    rows = nst // Q_PER_KV
    lane = lax.broadcasted_iota(jnp.int32, (nst, LANES), 1)
    o_st = jnp.where(lane < HEAD_DIM, o2[0:nst, :], o2[nst:2 * nst, :])
    o = jnp.concatenate([o_st[g * rows:(g + 1) * rows, :] for g in range(Q_PER_KV)], axis=1)
    return _rms(o, anw_ref[...])


def _attn_prompt_kernel(q_m, k_m, v_m, q_t, k_t, v_t, cos_m, sin_m, cos_t, sin_t, qnw, knw, snk, anw,
                        o_o, om_o, kpm_o, kl_o, vl_o,
                        km_lo, km_hi, vm, kp_lo, kp_hi, vp, kmf, ombuf):
    c = pl.program_id(1)
    pad = CHUNK - N_META
    nst = Q_PER_KV * CHUNK
    ri = lax.broadcasted_iota(jnp.int32, (2 * nst, LANES), 0) & (CHUNK - 1)
    cj = lax.broadcasted_iota(jnp.int32, (2 * nst, LANES), 1)
    sk = _sink_column(snk, CHUNK)

    @pl.when(c == 0)
    def _():
        def stage(meta_ref):
            return jnp.concatenate([jnp.zeros((pad, meta_ref.shape[1]), F32), meta_ref[...]], axis=0)

        cs, sn = cos_t[...], sin_t[...]
        kp, k_lo, k_hi, v_b = _kv_prep(stage(k_t), stage(v_t), knw, cs, sn)
        zpad = jnp.zeros((pad, LANES), BF16)
        km_lo[...] = jnp.concatenate([k_lo[pad:, :], zpad], axis=0)
        km_hi[...] = jnp.concatenate([k_hi[pad:, :], zpad], axis=0)
        vm[...] = jnp.concatenate([v_b[pad:, :], zpad], axis=0)
        kmf[...] = kp[pad:, :]
        kp_lo[...] = jnp.zeros(kp_lo.shape, BF16)
        kp_hi[...] = jnp.zeros(kp_hi.shape, BF16)
        vp[...] = jnp.zeros(vp.shape, BF16)
        qs = jnp.concatenate(_q_groups(stage(q_t), qnw, cs, sn), axis=0).astype(BF16)
        meta_ok = jnp.logical_and(cj < N_META, cj <= ri - pad)
        (e_m,), inv = _softmax_weights([jnp.where(meta_ok, _head_scores(qs, km_lo[...], km_hi[...]), NEG)], sk)
        ombuf[...] = _attn_out(_dot(e_m.astype(BF16), vm[...]) * inv, anw)[pad:, :]

    @pl.when(c > 0)
    def _():
        nblk = q_m.shape[0] // CHUNK
        prev = (kp_lo[...], kp_hi[...], vp[...])
        tri = cj <= ri
        for j in range(nblk):
            rows = slice(j * CHUNK, (j + 1) * CHUNK)
            cs, sn = cos_m[rows, :], sin_m[rows, :]
            v = v_m[rows, :]
            kp, k_lo, k_hi, v_b = _kv_prep(k_m[rows, :], v, knw, cs, sn)
            qs = jnp.concatenate(_q_groups(q_m[rows, :], qnw, cs, sn), axis=0).astype(BF16)
            band = jnp.where(tri, _head_scores(qs, k_lo, k_hi), _head_scores(qs, prev[0], prev[1]))
            if j == 0:
                band = jnp.where(jnp.logical_or(tri, c > 1), band, NEG)
            meta = jnp.where(cj < N_META, _head_scores(qs, km_lo[...], km_hi[...]), NEG)
            (e_b, e_m), inv = _softmax_weights([band, meta], sk)
            o2 = (_dot(jnp.where(tri, e_b, 0.0).astype(BF16), v_b)
                  + _dot(jnp.where(tri, 0.0, e_b).astype(BF16), prev[2])
                  + _dot(e_m.astype(BF16), vm[...])) * inv
            o_o[rows, :] = _attn_out(o2, anw).astype(BF16)
            prev = (k_lo, k_hi, v_b)
            if j == nblk - 1:
                kl_o[...] = kp
                vl_o[...] = v
        kp_lo[...], kp_hi[...], vp[...] = prev

    om_o[...] = ombuf[...].astype(BF16)
    kpm_o[...] = kmf[...]


def _attn_sample_kernel(q_s, k_s, v_s, mk, mv, wk, wv, cos, sin, qnw, knw, snk, anw,
                        o_o, kp_o, obuf):
    nsq = mk.shape[0]
    t = SUBLANES
    nst = Q_PER_KV * t
    cs, sn = cos[...], sin[...]
    kp = _qk_prep(k_s[...], knw[...], cs, sn)
    kp_o[...] = kp
    v = v_s[...]
    qg = _q_groups(q_s[...], qnw, cs, sn)

    lo = lax.broadcasted_iota(jnp.int32, (nst, LANES), 1) < HEAD_DIM
    i_q = lax.broadcasted_iota(jnp.int32, (2 * nst, LANES), 0) & (t - 1)
    cj = lax.broadcasted_iota(jnp.int32, (2 * nst, LANES), 1)
    mask_a = cj > i_q
    mask_b = jnp.logical_or(cj <= i_q, jnp.logical_and(cj >= t, cj < t + N_META))
    sk = _sink_column(snk, t)
    zpad = jnp.zeros((WINDOW - t - N_META, LANES), F32)
    for s in range(nsq):
        rows = slice(s * t, (s + 1) * t)
        q_st = jnp.concatenate([g[rows, :] for g in qg], axis=0)
        q2 = jnp.concatenate([jnp.where(lo, q_st, 0.0), jnp.where(lo, 0.0, q_st)], axis=0).astype(BF16)
        k_b = jnp.concatenate([kp[rows, :], mk[s], zpad], axis=0).astype(BF16)
        v_b = jnp.concatenate([v[rows, :], mv[s], zpad], axis=0).astype(BF16)
        (e_a, e_b), inv = _softmax_weights(
            [jnp.where(mask_a, _dot_nt(q2, wk[s].astype(BF16)), NEG),
             jnp.where(mask_b, _dot_nt(q2, k_b), NEG)], sk)
        o2 = (_dot(e_a.astype(BF16), wv[s].astype(BF16)) + _dot(e_b.astype(BF16), v_b)) * inv
        o_st = jnp.where(lo, o2[0:nst, :], o2[nst:2 * nst, :])
        for g in range(Q_PER_KV):
            obuf[rows, g * LANES:(g + 1) * LANES] = o_st[g * t:(g + 1) * t, :]
    o_o[...] = _rms(obuf[...], anw[...]).astype(BF16)


def _attn_prompt(proj_main, proj_tail, tabs_main, tabs_meta, params, bsz, seq, meta_row):
    blk = _chunks_per_step(seq // CHUNK) * CHUNK
    nb = seq // blk
    mb = meta_row // N_META

    def main(col):
        return lambda b, c: (b * nb + jnp.maximum(c - 1, 0), col)

    def meta(col):
        return lambda b, c: (mb, col)

    per_batch = lambda b, c: (b, 0, 0)
    in_specs = [
        pl.BlockSpec((blk, D_ATTN), main(COL_Q // D_ATTN)),
        pl.BlockSpec((blk, KV_DIM), main(COL_K // KV_DIM)),
        pl.BlockSpec((blk, KV_DIM), main(COL_V // KV_DIM)),
        pl.BlockSpec((N_META, D_ATTN), meta(COL_Q // D_ATTN)),
        pl.BlockSpec((N_META, KV_DIM), meta(COL_K // KV_DIM)),
        pl.BlockSpec((N_META, KV_DIM), meta(COL_V // KV_DIM)),
        pl.BlockSpec((blk, LANES), lambda b, c: (jnp.maximum(c - 1, 0), 0)),
        pl.BlockSpec((blk, LANES), lambda b, c: (jnp.maximum(c - 1, 0), 0)),
        pl.BlockSpec((CHUNK, LANES), lambda b, c: (0, 0)),
        pl.BlockSpec((CHUNK, LANES), lambda b, c: (0, 0)),
    ] + _param_specs(params)
    return pl.pallas_call(
        _attn_prompt_kernel,
        grid=(bsz, nb + 1),
        in_specs=in_specs,
        out_specs=[
            pl.BlockSpec((blk, D_ATTN), main(0)),
            pl.BlockSpec((None, N_META, D_ATTN), per_batch),
            pl.BlockSpec((None, N_META, KV_DIM), per_batch),
            pl.BlockSpec((None, CHUNK, KV_DIM), per_batch),
            pl.BlockSpec((None, CHUNK, KV_DIM), per_batch),
        ],
        out_shape=[
            jax.ShapeDtypeStruct((bsz * seq, D_ATTN), BF16),
            jax.ShapeDtypeStruct((bsz, N_META, D_ATTN), BF16),
            jax.ShapeDtypeStruct((bsz, N_META, KV_DIM), F32),
            jax.ShapeDtypeStruct((bsz, CHUNK, KV_DIM), F32),
            jax.ShapeDtypeStruct((bsz, CHUNK, KV_DIM), F32),
        ],
        scratch_shapes=[pltpu.VMEM((CHUNK, KV_DIM), BF16) for _ in range(6)] + [
            pltpu.VMEM((N_META, KV_DIM), F32), pltpu.VMEM((N_META, D_ATTN), F32),
        ],
        compiler_params=_cparams(2),
        name="attn_prompt",
    )(proj_main, proj_main, proj_main, proj_tail, proj_tail, proj_tail, *tabs_main, *tabs_meta, *params)


def _attn_sample(proj_tail, mk, mv, wk, wv, cos, sin, params, nseq, t, nsq):
    blk = nsq * t
    per_seq = lambda b: (b, 0, 0)
    in_specs = [
        pl.BlockSpec((blk, D_ATTN), lambda b: (b, COL_Q // D_ATTN)),
        pl.BlockSpec((blk, KV_DIM), lambda b: (b, COL_K // KV_DIM)),
        pl.BlockSpec((blk, KV_DIM), lambda b: (b, COL_V // KV_DIM)),
        pl.BlockSpec((nsq, N_META, KV_DIM), per_seq),
        pl.BlockSpec((nsq, N_META, KV_DIM), per_seq),
        pl.BlockSpec((nsq, WINDOW, KV_DIM), per_seq),
        pl.BlockSpec((nsq, WINDOW, KV_DIM), per_seq),
        pl.BlockSpec((blk, LANES), lambda b: (0, 0)),
        pl.BlockSpec((blk, LANES), lambda b: (0, 0)),
    ] + _param_specs(params)
    return pl.pallas_call(
        _attn_sample_kernel,
        grid=(nseq // nsq,),
        in_specs=in_specs,
        out_specs=[
            pl.BlockSpec((blk, D_ATTN), lambda b: (b, 0)),
            pl.BlockSpec((blk, KV_DIM), lambda b: (b, 0)),
        ],
        out_shape=[
            jax.ShapeDtypeStruct((nseq * t, D_ATTN), BF16),
            jax.ShapeDtypeStruct((nseq * t, KV_DIM), F32),
        ],
        scratch_shapes=[pltpu.VMEM((blk, D_ATTN), F32)],
        compiler_params=_cparams(1),
        name="attn_sample",
    )(proj_tail, proj_tail, proj_tail, mk, mv, wk, wv, cos, sin, *params)


def _mix_out(n_main, x_refs, ys_refs, ya_refs, wo_ref):
    return (_pick(n_main, *x_refs) + _dot(_pick(n_main, *ys_refs), wo_ref[0:D_SSD, :])
            + _dot(_pick(n_main, *ya_refs), wo_ref[D_SSD:D_SSD + D_ATTN, :]))


def _swiglu_acc(hn, wg_ref, wu_ref, wd_ref, acc_ref):
    for j in range(D_FF // FF_CHUNK):
        cols = slice(j * FF_CHUNK, (j + 1) * FF_CHUNK)
        a = (_silu(_dot(hn, wg_ref[:, cols])) * _dot(hn, wu_ref[:, cols])).astype(BF16)
        acc_ref[...] += _dot(a, wd_ref[cols, :])


def _out_ffn_kernel(n_main, xa, xb, ysa, ysb, yaa, yab, wo_ref, nw_ref, wg_ref, wu_ref, wd_ref,
                    om_ref, ot_ref):
    xm = _mix_out(n_main, (xa, xb), (ysa, ysb), (yaa, yab), wo_ref)
    om_ref[...] = xm
    _swiglu_acc(_rms(xm, nw_ref[...]).astype(BF16), wg_ref, wu_ref, wd_ref, om_ref)
    ot_ref[...] = om_ref[...]


def _out_ffn(x, ys, ya, wo, nw, wg, wu, wd, n_main):
    n_tail = x[1].shape[0] // ROW_TILE
    return pl.pallas_call(
        functools.partial(_out_ffn_kernel, n_main),
        grid=(n_main + n_tail,),
        in_specs=_src_specs(D_MODEL, n_main) + _src_specs(D_SSD, n_main) + _src_specs(D_ATTN, n_main) + [
            _resident(wo.shape), _resident(nw.shape),
            _resident(wg.shape), _resident(wu.shape), _resident(wd.shape),
        ],
        out_specs=_dst_specs(D_MODEL, n_main),
        out_shape=_dst_shapes(D_MODEL, n_main, n_tail, F32),
        compiler_params=_cparams(1),
        name="out_ffn",
    )(*x, *ys, *ya, wo, nw, wg, wu, wd)


def _out_router_kernel(n_main, n_tok, xa, xb, ysa, ysb, yaa, yab, wo_ref, nw_ref, wr_hi_ref, wr_lo_ref,
                       before_ref, xm_o, rt_o, cnt_o):
    xm = _mix_out(n_main, (xa, xb), (ysa, ysb), (yaa, yab), wo_ref)
    xm_o[...] = xm
    hn = _rms(xm, nw_ref[...])
    hi = hn.astype(BF16)
    lo = (hn - hi.astype(F32)).astype(BF16)
    logits = _dot(hi, wr_hi_ref[...]) + _dot(lo, wr_hi_ref[...]) + _dot(hi, wr_lo_ref[...])
    lane = lax.broadcasted_iota(jnp.int32, logits.shape, 1)
    logits = jnp.where(lane < N_EXPERTS, logits, -jnp.inf)
    v1 = jnp.max(logits, axis=-1, keepdims=True)
    i1 = jnp.min(jnp.where(logits == v1, lane, LANES), axis=-1, keepdims=True)
    rest = jnp.where(lane == i1, -jnp.inf, logits)
    v2 = jnp.max(rest, axis=-1, keepdims=True)
    i2 = jnp.min(jnp.where(rest == v2, lane, LANES), axis=-1, keepdims=True)
    e2 = jnp.exp(v2 - v1)
    g1 = 1.0 / (1.0 + e2)
    g2 = e2 / (1.0 + e2)

    row = pl.program_id(0) * ROW_TILE + lax.broadcasted_iota(jnp.int32, (ROW_TILE, 1), 0)
    valid = row < n_tok
    oh1 = jnp.where(jnp.logical_and(lane == i1, valid), 1.0, 0.0)
    oh2 = jnp.where(jnp.logical_and(lane == i2, valid), 1.0, 0.0)
    before = before_ref[...]
    c1 = _dot(before, oh1.astype(BF16))
    c2 = _dot(before, oh2.astype(BF16))
    tot1 = jnp.sum(oh1, axis=0, keepdims=True)
    tot2 = jnp.sum(oh2, axis=0, keepdims=True)
    rank1 = jnp.sum(jnp.where(lane == i1, c1, 0.0), axis=-1, keepdims=True)
    rank2 = jnp.sum(jnp.where(lane == i2, c2 + tot1, 0.0), axis=-1, keepdims=True)
    cnt_o[...] = jnp.broadcast_to(tot1 + tot2, cnt_o.shape)
    route = jnp.where(lane == 0, i1.astype(F32), 0.0)
    for k, val in enumerate((i2.astype(F32), g1, g2, rank1, rank2)):
        route = jnp.where(lane == k + 1, val, route)
    rt_o[...] = route


def _out_router(x, ys, ya, wo, nw, wr_hi, wr_lo, n_main, n_tok):
    n_tiles = n_main + x[1].shape[0] // ROW_TILE
    rows = n_tiles * ROW_TILE
    before = jnp.tril(jnp.ones((ROW_TILE, ROW_TILE), BF16), -1)
    return pl.pallas_call(
        functools.partial(_out_router_kernel, n_main, n_tok),
        grid=(n_tiles,),
        in_specs=_src_specs(D_MODEL, n_main) + _src_specs(D_SSD, n_main) + _src_specs(D_ATTN, n_main) + [
            _resident(wo.shape), _resident(nw.shape), _resident(wr_hi.shape), _resident(wr_lo.shape),
            _resident(before.shape),
        ],
        out_specs=[
            pl.BlockSpec((ROW_TILE, D_MODEL), lambda i: (i, 0)),
            pl.BlockSpec((ROW_TILE, LANES), lambda i: (i, 0)),
            pl.BlockSpec((None, SUBLANES, LANES), lambda i: (i, 0, 0)),
        ],
        out_shape=[
            jax.ShapeDtypeStruct((rows, D_MODEL), F32),
            jax.ShapeDtypeStruct((rows, LANES), F32),
            jax.ShapeDtypeStruct((n_tiles, SUBLANES, LANES), F32),
        ],
        compiler_params=_cparams(1),
        name="out_router",
    )(*x, *ys, *ya, wo, nw, wr_hi, wr_lo, before)


def _tile_rows(idx, n_tiles, tail, fn):
    if tail == ROW_TILE:
        fn(ROW_TILE)
    else:
        pl.when(idx < n_tiles - 1)(lambda: fn(ROW_TILE))
        pl.when(idx == n_tiles - 1)(lambda: fn(tail))


def _dispatch_kernel(n_tiles, tail, dest_ref, x_ref, xs_in_ref, xs_ref, sem):
    del xs_in_ref

    def run(nrows):
        def body(r, carry):
            for k in range(TOP_K):
                d = dest_ref[0, 0, TOP_K * r + k]
                pltpu.make_async_copy(x_ref.at[pl.ds(r, 1)], xs_ref.at[pl.ds(d, 1)], sem).start()
            return carry

        lax.fori_loop(0, nrows, body, 0, unroll=DMA_UNROLL)
        for k in range(TOP_K):
            pltpu.make_async_copy(x_ref.at[pl.ds(0, nrows)], xs_ref.at[pl.ds(0, nrows)], sem).wait()

    _tile_rows(pl.program_id(0), n_tiles, tail, run)


def _dispatch(dest, xm, n_tok, m_rows):
    n_tiles = xm.shape[0] // ROW_TILE
    tail = n_tok - (n_tiles - 1) * ROW_TILE
    return pl.pallas_call(
        functools.partial(_dispatch_kernel, n_tiles, tail),
        grid=(n_tiles,),
        in_specs=[
            pl.BlockSpec((1, 1, TOP_K * ROW_TILE), lambda i: (i, 0, 0), memory_space=pltpu.SMEM),
            pl.BlockSpec((ROW_TILE, D_MODEL), lambda i: (i, 0)),
            pl.BlockSpec(memory_space=pl.ANY),
        ],
        out_specs=pl.BlockSpec(memory_space=pl.ANY),
        out_shape=jax.ShapeDtypeStruct((m_rows, D_MODEL), F32),
        scratch_shapes=[pltpu.SemaphoreType.DMA(())],
        input_output_aliases={2: 0},
        compiler_params=_cparams(1, has_side_effects=True),
        name="moe_dispatch",
    )(dest, xm, jnp.zeros((m_rows, D_MODEL), F32))


def _moe_kernel(te_ref, nu_ref, x_ref, nw_ref, wg_ref, wu_ref, wd_ref, o_ref):
    i = pl.program_id(0)
    o_ref[...] = jnp.zeros(o_ref.shape, F32)

    @pl.when(i < nu_ref[0])
    def _():
        _swiglu_acc(_rms(x_ref[...], nw_ref[...]).astype(BF16), wg_ref, wu_ref, wd_ref, o_ref)


def _moe_experts(tile_e, n_used, xs, nw, wg, wu, wd):
    rows = xs.shape[0]
    grid_spec = pltpu.PrefetchScalarGridSpec(
        num_scalar_prefetch=2,
        grid=(rows // ROW_TILE,),
        in_specs=[
            pl.BlockSpec((ROW_TILE, D_MODEL), lambda i, te, nu: (i, 0)),
            pl.BlockSpec((1, D_MODEL), lambda i, te, nu: (0, 0)),
            pl.BlockSpec((None, D_MODEL, D_FF), lambda i, te, nu: (te[i], 0, 0)),
            pl.BlockSpec((None, D_MODEL, D_FF), lambda i, te, nu: (te[i], 0, 0)),
            pl.BlockSpec((None, D_FF, D_MODEL), lambda i, te, nu: (te[i], 0, 0)),
        ],
        out_specs=pl.BlockSpec((ROW_TILE, D_MODEL), lambda i, te, nu: (i, 0)),
    )
    return pl.pallas_call(
        _moe_kernel,
        grid_spec=grid_spec,
        out_shape=jax.ShapeDtypeStruct((rows, D_MODEL), F32),
        compiler_params=_cparams(1),
        name="moe_experts",
    )(tile_e, n_used, xs, nw, wg, wu, wd)


def _combine_kernel(n_main, n_tiles, tail, dcur_ref, dnext_ref, xm_ref, rt_ref, yb_ref,
                    om_ref, ot_ref, gbuf, sem):
    i = pl.program_id(0)

    def issue(dref, slot, nrows):
        def body(r, carry):
            for k in range(TOP_K):
                d = dref[0, 0, TOP_K * r + k]
                pltpu.make_async_copy(yb_ref.at[pl.ds(d, 1)], gbuf.at[slot, k, pl.ds(r, 1)],
                                      sem.at[slot]).start()
            return carry

        lax.fori_loop(0, nrows, body, 0, unroll=DMA_UNROLL)

    def wait(slot, nrows):
        for k in range(TOP_K):
            pltpu.make_async_copy(yb_ref.at[pl.ds(0, nrows)], gbuf.at[slot, k, pl.ds(0, nrows)],
                                  sem.at[slot]).wait()

    @pl.when(i == 0)
    def _():
        gbuf[...] = jnp.zeros(gbuf.shape, F32)
        _tile_rows(i, n_tiles, tail, lambda n: issue(dcur_ref, 0, n))

    @pl.when(i + 1 < n_tiles)
    def _():
        _tile_rows(i + 1, n_tiles, tail, lambda n: issue(dnext_ref, (i + 1) % 2, n))

    slot = i % 2
    _tile_rows(i, n_tiles, tail, lambda n: wait(slot, n))
    gates = rt_ref[...]
    val = (xm_ref[...] + gates[:, TOP_K:TOP_K + 1] * gbuf[slot, 0]
           + gates[:, TOP_K + 1:TOP_K + 2] * gbuf[slot, 1])

    @pl.when(i < n_main)
    def _():
        om_ref[...] = val

    @pl.when(i >= n_main)
    def _():
        ot_ref[...] = val


def _combine(dest, xm, route, yb, n_main, n_tok):
    n_tiles = xm.shape[0] // ROW_TILE
    tail = n_tok - (n_tiles - 1) * ROW_TILE
    dspec = lambda f: pl.BlockSpec((1, 1, TOP_K * ROW_TILE), f, memory_space=pltpu.SMEM)
    return pl.pallas_call(
        functools.partial(_combine_kernel, n_main, n_tiles, tail),
        grid=(n_tiles,),
        in_specs=[
            dspec(lambda i: (i, 0, 0)),
            dspec(lambda i: (jnp.minimum(i + 1, n_tiles - 1), 0, 0)),
            pl.BlockSpec((ROW_TILE, D_MODEL), lambda i: (i, 0)),
            pl.BlockSpec((ROW_TILE, LANES), lambda i: (i, 0)),
            pl.BlockSpec(memory_space=pl.ANY),
        ],
        out_specs=_src_specs(D_MODEL, n_main),
        out_shape=[jax.ShapeDtypeStruct((n_main * ROW_TILE, D_MODEL), F32),
                   jax.ShapeDtypeStruct(((n_tiles - n_main) * ROW_TILE, D_MODEL), F32)],
        scratch_shapes=[pltpu.VMEM((2, TOP_K, ROW_TILE, D_MODEL), F32), pltpu.SemaphoreType.DMA((2,))],
        compiler_params=_cparams(1),
        name="moe_combine",
    )(dest, dest, xm, route, yb)


def _moe_layer(xm, route, counts, n_main, n_tok, nw, wg, wu, wd):
    n_tiles = xm.shape[0] // ROW_TILE
    m_tiles = -(-(n_tok * TOP_K + N_EXPERTS * (ROW_TILE - 1)) // ROW_TILE)
    cnt = counts[:, 0, :N_EXPERTS].astype(jnp.int32)
    total = jnp.sum(cnt, axis=0)
    padded = (total + ROW_TILE - 1) // ROW_TILE * ROW_TILE
    pad_end = jnp.cumsum(padded)
    base = (pad_end - padded)[None, :] + jnp.cumsum(cnt, axis=0) - cnt
    e = route[:, 0:TOP_K].astype(jnp.int32).reshape(n_tiles, ROW_TILE, TOP_K)
    rank = route[:, 2 * TOP_K:3 * TOP_K].astype(jnp.int32).reshape(n_tiles, ROW_TILE, TOP_K)
    onehot = e[..., None] == jnp.arange(N_EXPERTS, dtype=jnp.int32)
    dest = jnp.sum(jnp.where(onehot, base[:, None, None, :], 0), axis=-1) + rank
    dest = dest.reshape(n_tiles, 1, ROW_TILE * TOP_K)
    tile_e = jnp.minimum(jnp.searchsorted(pad_end, jnp.arange(m_tiles, dtype=jnp.int32) * ROW_TILE,
                                          side='right'), N_EXPERTS - 1).astype(jnp.int32)
    n_used = (pad_end[-1:] // ROW_TILE).astype(jnp.int32)
    xs = _dispatch(dest, xm, n_tok, m_tiles * ROW_TILE)
    yb = _moe_experts(tile_e, n_used, xs, nw, wg, wu, wd)
    return _combine(dest, xm, route, yb, n_main, n_tok)


def _rope_tables(pos):
    half = HEAD_DIM // 2
    inv_freq = ROPE_THETA ** (-jnp.arange(half, dtype=F32) / half)
    ang = pos.astype(F32)[:, None] * inv_freq[None, :]
    cos = jnp.cos(ang)
    sin = jnp.sin(ang)
    reps = LANES // HEAD_DIM
    return (jnp.tile(jnp.concatenate([cos, cos], axis=-1), (1, reps)),
            jnp.tile(jnp.concatenate([-sin, sin], axis=-1), (1, reps)))


def _pad_lanes(v, width=LANES):
    v = v.astype(F32).reshape(1, -1)
    return jnp.pad(v, ((0, 0), (0, width - v.shape[1])))


def kernel(x_prompt, x_sample, state_ssm, state_conv, cache_meta_k, cache_meta_v, cache_win_k, cache_win_v, meta_tokens, norm_mix_w, w_in, conv_w, conv_b, dt_bias, a_log, d_skip, ssd_norm_w, q_norm_w, k_norm_w, sinks, attn_norm_w, w_out, norm_ffn_w, w_gate, w_up, w_down, w_router, moe_w_gate, moe_w_up, moe_w_down):
    bsz, seq, _ = x_prompt.shape
    nseq, t_s, _ = x_sample.shape
    depth = w_in.shape[0]
    r_main = bsz * seq
    r_samp = nseq * t_s
    assert seq % CHUNK == 0 and t_s == SUBLANES and r_main % ROW_TILE == 0 and r_samp % N_META == 0
    n_main = r_main // ROW_TILE
    n_tok = r_main + r_samp + N_META
    r_tail = -(-(r_samp + N_META) // ROW_TILE) * ROW_TILE
    tail_pad = r_tail - r_samp - N_META

    x = (x_prompt.reshape(r_main, D_MODEL),
         jnp.concatenate([x_sample.reshape(r_samp, D_MODEL), meta_tokens.astype(F32),
                          jnp.zeros((tail_pad, D_MODEL), F32)], axis=0))

    tabs_main = _rope_tables(N_META + jnp.arange(seq, dtype=jnp.int32))
    tabs_meta = _rope_tables(jnp.arange(CHUNK, dtype=jnp.int32) - (CHUNK - N_META))
    nsq = 16 if nseq % 16 == 0 else nseq
    cos_s, sin_s = (jnp.tile(tab, (nsq, 1)) for tab in
                    _rope_tables(PAST_LEN + jnp.arange(t_s, dtype=jnp.int32)))

    o_z, o_xbc, o_dt, o_q, o_k, o_v = 0, 512, 1536, 1544, 2056, 2184
    col = jnp.arange(D_ATTN, dtype=jnp.int32)
    grp, lane = col // LANES, col % LANES
    head_perm = (grp + Q_PER_KV * (lane // HEAD_DIM)) * HEAD_DIM + lane % HEAD_DIM

    def heads4(a):
        return a.reshape(a.shape[0], a.shape[1], N_KV_HEADS, HEAD_DIM)

    outs = {k: [] for k in ('p_ssm', 'p_conv', 'p_mk', 'p_mv', 'p_wk', 'p_wv', 's_ssm', 's_conv', 's_wk', 's_wv')}
    for l in range(depth):
        wl = w_in[l]
        w_re = jnp.concatenate([
            wl[:, o_xbc:o_xbc + CONV_DIM], wl[:, o_z:o_z + D_SSD], wl[:, o_q:o_q + D_ATTN][:, head_perm],
            wl[:, o_k:o_k + KV_DIM], wl[:, o_v:o_v + KV_DIM], wl[:, o_dt:o_dt + N_SSD_HEADS],
            jnp.zeros((D_MODEL, PROJ_W - COL_DT - N_SSD_HEADS), wl.dtype)], axis=1).astype(BF16)
        proj_main, proj_tail = _in_proj(x[0], x[1], norm_mix_w[l].reshape(1, D_MODEL).astype(F32), w_re, n_main)

        ssd_params = (conv_w[l].astype(F32), conv_b[l].reshape(1, CONV_DIM).astype(F32),
                      _pad_lanes(dt_bias[l]), _pad_lanes(-jnp.exp(a_log[l].astype(F32))),
                      jnp.repeat(d_skip[l].astype(F32), SSD_HEAD_DIM).reshape(1, D_SSD),
                      ssd_norm_w[l].reshape(1, D_SSD).astype(F32))
        ys_p, ys_m, ssm_p, conv_p = _ssd_prompt(proj_main, proj_tail, ssd_params, bsz, seq, r_samp)
        conv_prev = jnp.pad(state_conv[l].astype(F32), ((0, 0), (SUBLANES - (CONV_W - 1), 0), (0, 0)))
        ys_s, ssm_s = _ssd_sample(proj_tail, conv_prev, state_ssm[l].astype(F32), ssd_params, nseq, t_s, nsq)

        attn_params = (jnp.tile(q_norm_w[l].astype(F32), N_Q_HEADS).reshape(1, D_ATTN),
                       jnp.tile(k_norm_w[l].astype(F32), N_KV_HEADS).reshape(1, KV_DIM),
                       _pad_lanes(sinks[l]), attn_norm_w[l][head_perm].reshape(1, D_ATTN).astype(F32))
        ya_p, ya_m, kp_m, k_last, v_last = _attn_prompt(proj_main, proj_tail, tabs_main, tabs_meta,
                                                        attn_params, bsz, seq, r_samp)
        ya_s, kp_s = _attn_sample(
            proj_tail, cache_meta_k[l].reshape(nseq, N_META, KV_DIM).astype(F32),
            cache_meta_v[l].reshape(nseq, N_META, KV_DIM).astype(F32),
            cache_win_k[l].reshape(nseq, WINDOW, KV_DIM).astype(F32),
            cache_win_v[l].reshape(nseq, WINDOW, KV_DIM).astype(F32),
            cos_s, sin_s, attn_params, nseq, t_s, nsq)

        ys = (ys_p, jnp.concatenate([ys_s, ys_m[0], jnp.zeros((tail_pad, D_SSD), BF16)], axis=0))
        ya = (ya_p, jnp.concatenate([ya_s, ya_m[0], jnp.zeros((tail_pad, D_ATTN), BF16)], axis=0))

        wo = jnp.concatenate([w_out[l][:D_SSD], w_out[l][D_SSD:][head_perm]], axis=0).astype(BF16)
        nfw = norm_ffn_w[l].reshape(1, D_MODEL).astype(F32)
        i = l // 2
        if l % 2 == 0:
            x = _out_ffn(x, ys, ya, wo, nfw, w_gate[i].astype(BF16), w_up[i].astype(BF16),
                         w_down[i].astype(BF16), n_main)
        else:
            wr = jnp.pad(w_router[i].astype(F32), ((0, 0), (0, LANES - N_EXPERTS)))
            wr_hi = wr.astype(BF16)
            wr_lo = (wr - wr_hi.astype(F32)).astype(BF16)
            xm, route, counts = _out_router(x, ys, ya, wo, nfw, wr_hi, wr_lo, n_main, n_tok)
            x = _moe_layer(xm, route, counts, n_main, n_tok, nfw, moe_w_gate[i].astype(BF16),
                           moe_w_up[i].astype(BF16), moe_w_down[i].astype(BF16))

        samp = proj_tail[:r_samp]
        xbc_s = samp[:, COL_XBC:COL_XBC + CONV_DIM].reshape(nseq, t_s, CONV_DIM)
        v_s = samp[:, COL_V:COL_V + KV_DIM].reshape(nseq, t_s, KV_DIM)
        v_meta = proj_tail[r_samp:r_samp + N_META, COL_V:COL_V + KV_DIM]
        meta_shape = (bsz, N_META, N_KV_HEADS, HEAD_DIM)
        outs['p_ssm'].append(ssm_p)
        outs['p_conv'].append(conv_p[:, SUBLANES - (CONV_W - 1):])
        outs['p_mk'].append(jnp.broadcast_to(heads4(kp_m[0:1]), meta_shape))
        outs['p_mv'].append(jnp.broadcast_to(heads4(v_meta[None]), meta_shape))
        outs['p_wk'].append(heads4(k_last))
        outs['p_wv'].append(heads4(v_last))
        outs['s_ssm'].append(ssm_s)
        outs['s_conv'].append(jnp.concatenate([state_conv[l].astype(F32), xbc_s], axis=1)[:, t_s:])
        outs['s_wk'].append(jnp.concatenate([cache_win_k[l].astype(F32),
                                             heads4(kp_s.reshape(nseq, t_s, KV_DIM))], axis=1)[:, t_s:])
        outs['s_wv'].append(jnp.concatenate([cache_win_v[l].astype(F32), heads4(v_s)], axis=1)[:, t_s:])

    y_prompt = x[0][:r_main].reshape(bsz, seq, D_MODEL)
    y_sample = x[1][:r_samp].reshape(nseq, t_s, D_MODEL)
    st = lambda k: jnp.stack(outs[k])
    return (y_prompt, y_sample, st('p_ssm'), st('p_conv'), st('p_mk'), st('p_mv'), st('p_wk'), st('p_wv'),
            st('s_ssm'), st('s_conv'), st('s_wk'), st('s_wv'))
```

```python
import functools

import jax
import jax.numpy as jnp
from jax import lax
from jax.experimental import pallas as pl
from jax.experimental.pallas import tpu as pltpu

F32 = jnp.float32
BF16 = jnp.bfloat16

D_MODEL = 1024
D_SSD = 512
SSD_HEAD_DIM = 64
N_SSD_HEADS = 8
SSD_HEADS_PER_GROUP = 4
N_SSD_GROUPS = 2
D_STATE = 128
CONV_W = 4
CONV_DIM = 1024
D_ATTN = 512
HEAD_DIM = 64
N_Q_HEADS = 8
N_KV_HEADS = 2
Q_PER_KV = 4
KV_DIM = 128
WINDOW = 128
N_META = 16
D_FF = 2816
N_EXPERTS = 8
TOP_K = 2
EPS = 1e-6
NEG = -1e30
ATTN_SCALE = HEAD_DIM ** -0.5
PAST_LEN = 16384
ROPE_THETA = 10000.0

LANES = 128
SUBLANES = 8
CHUNK = 128
ROW_TILE = 512
FF_CHUNK = 256
DMA_UNROLL = 8
VMEM_LIMIT = 60 * 1024 * 1024

COL_XBC = 0
COL_Z = 1024
COL_Q = 1536
COL_K = 2048
COL_V = 2176
COL_DT = 2304
PROJ_W = 2560


def _dot(a, b):
    return jnp.dot(a, b, preferred_element_type=F32)


def _dot_nt(a, b):
    return lax.dot_general(a, b, (((1,), (1,)), ((), ())), preferred_element_type=F32)


def _dot_tn(a, b):
    return lax.dot_general(a, b, (((0,), (0,)), ((), ())), preferred_element_type=F32)


def _rms(x, w):
    return x * lax.rsqrt(jnp.mean(x * x, axis=-1, keepdims=True) + EPS) * w


def _silu(x):
    return x * jax.nn.sigmoid(x)


def _split3(x):
    p1 = x.astype(BF16)
    r1 = x - p1.astype(F32)
    p2 = r1.astype(BF16)
    p3 = (r1 - p2.astype(F32)).astype(BF16)
    return p1, p2, p3


def _dot_exact(sel, x):
    p1, p2, p3 = _split3(x)
    return _dot(sel, p1) + _dot(sel, p2) + _dot(sel, p3)


def _cparams(ndim, **kw):
    return pltpu.CompilerParams(dimension_semantics=("arbitrary",) * ndim, vmem_limit_bytes=VMEM_LIMIT, **kw)


def _src_specs(width, n_main):
    return [pl.BlockSpec((ROW_TILE, width), lambda i: (jnp.minimum(i, n_main - 1), 0)),
            pl.BlockSpec((ROW_TILE, width), lambda i: (jnp.maximum(i - n_main, 0), 0))]


def _pick(n_main, main_ref, tail_ref):
    dtype = main_ref.dtype
    picked = jnp.where(pl.program_id(0) < n_main, main_ref[...].astype(F32), tail_ref[...].astype(F32))
    return picked.astype(dtype)


def _dst_specs(width, n_main):
    return [pl.BlockSpec((ROW_TILE, width), lambda i: (jnp.minimum(i, n_main), 0)),
            pl.BlockSpec((ROW_TILE, width), lambda i: (jnp.maximum(i - n_main, 0), 0))]


def _dst_shapes(width, n_main, n_tail, dtype):
    return [jax.ShapeDtypeStruct(((n_main + 1) * ROW_TILE, width), dtype),
            jax.ShapeDtypeStruct((n_tail * ROW_TILE, width), dtype)]


def _resident(shape):
    nd = len(shape)
    return pl.BlockSpec(shape, lambda *a: (0,) * nd, pipeline_mode=pl.Buffered(1))


def _param_specs(params):
    return [pl.BlockSpec(p.shape, lambda *a: (0, 0)) for p in params]


def _in_proj_kernel(n_main, xa_ref, xb_ref, nw_ref, w_ref, om_ref, ot_ref):
    xn = _rms(_pick(n_main, xa_ref, xb_ref), nw_ref[...]).astype(BF16)
    for j in range(PROJ_W // 512):
        cols = slice(j * 512, (j + 1) * 512)
        r = _dot(xn, w_ref[:, cols])
        om_ref[:, cols] = r
        ot_ref[:, cols] = r


def _in_proj(x_main, x_tail, nw, w, n_main):
    n_tail = x_tail.shape[0] // ROW_TILE
    return pl.pallas_call(
        functools.partial(_in_proj_kernel, n_main),
        grid=(n_main + n_tail,),
        in_specs=_src_specs(D_MODEL, n_main) + [
            pl.BlockSpec((1, D_MODEL), lambda i: (0, 0)),
            pl.BlockSpec((D_MODEL, PROJ_W), lambda i: (0, 0)),
        ],
        out_specs=_dst_specs(PROJ_W, n_main),
        out_shape=_dst_shapes(PROJ_W, n_main, n_tail, F32),
        compiler_params=_cparams(1),
        name="in_proj",
    )(x_main, x_tail, nw, w)


def _ssd_chunk(xbc, z, dt_raw, valid, cw_ref, cb_ref, dtb_ref, an_ref, dsk_ref, nw_ref, cbuf, hst):
    q = CHUNK
    cbuf[SUBLANES:SUBLANES + q, :] = xbc
    acc = cb_ref[...]
    for k in range(CONV_W):
        off = SUBLANES - (CONV_W - 1) + k
        acc = acc + cbuf[off:off + q, :] * cw_ref[k:k + 1, :]
    cbuf[0:SUBLANES, :] = cbuf[q:q + SUBLANES, :]
    xc = _silu(acc)
    xs = xc[:, :D_SSD]
    bm = xc[:, D_SSD:D_SSD + N_SSD_GROUPS * D_STATE]
    cm = xc[:, D_SSD + N_SSD_GROUPS * D_STATE:]

    dt = jax.nn.softplus(dt_raw + dtb_ref[...])
    if valid is not None:
        dt = jnp.where(valid, dt, 0.0)
    da = dt * an_ref[...]
    row_i = lax.broadcasted_iota(jnp.int32, (q, q), 0)
    col_j = lax.broadcasted_iota(jnp.int32, (q, q), 1)
    tril = row_i >= col_j
    cs = _dot_exact(jnp.where(tril, 1.0, 0.0).astype(BF16), da)
    cs_t = cs.T
    last = cs[q - 1:q, :]
    ecl = jnp.exp(last)
    xdt = xs * _expand_heads(dt)
    xw = xdt * _expand_heads(jnp.exp(last - cs))
    ecs_x = _expand_heads(jnp.exp(cs))
    skip = xs * dsk_ref[...]

    gw = SSD_HEADS_PER_GROUP * SSD_HEAD_DIM
    head_of_col = jnp.right_shift(lax.broadcasted_iota(jnp.int32, (q, gw), 1), SSD_HEAD_DIM.bit_length() - 1)
    ys = []
    for g in range(N_SSD_GROUPS):
        bg = bm[:, g * D_STATE:(g + 1) * D_STATE].astype(BF16)
        cg = cm[:, g * D_STATE:(g + 1) * D_STATE].astype(BF16)
        cb = _dot_nt(cg, bg)
        gc = slice(g * gw, (g + 1) * gw)
        h_prev = hst[gc, :]
        y_g = _dot_nt(cg, h_prev.astype(BF16)) * ecs_x[:, gc] + skip[:, gc]
        xdt_g = xdt[:, gc]
        for r in range(SSD_HEADS_PER_GROUP):
            h = g * SSD_HEADS_PER_GROUP + r
            seg = cs[:, h:h + 1] - cs_t[h:h + 1, :]
            decay = jnp.where(tril, jnp.exp(jnp.where(tril, seg, 0.0)), 0.0)
            y_g = y_g + _dot((decay * cb).astype(BF16), jnp.where(head_of_col == r, xdt_g, 0.0).astype(BF16))
        ys.append(y_g)
        ecl_col = jnp.concatenate(
            [jnp.broadcast_to(ecl[:, h:h + 1], (SSD_HEAD_DIM, 1))
             for h in range(g * SSD_HEADS_PER_GROUP, (g + 1) * SSD_HEADS_PER_GROUP)], axis=0)
        hst[gc, :] = h_prev * ecl_col + _dot_tn(xw[:, gc].astype(BF16), bg)

    return _ssd_gate_norm(jnp.concatenate(ys, axis=1), z, nw_ref)


def _ssd_gate_norm(y, z, nw_ref):
    y = y * _silu(z)
    gs = D_SSD // N_SSD_GROUPS
    return jnp.concatenate([_rms(y[:, g * gs:(g + 1) * gs], nw_ref[:, g * gs:(g + 1) * gs])
                            for g in range(N_SSD_GROUPS)], axis=-1)


def _ssd_prompt_kernel(xbc_m, z_m, dt_m, xbc_t, z_t, dt_t, cw, cb, dtb, an, dsk, nw,
                       y_o, ym_o, hf_o, ct_o, cbuf, hst, ymbuf):
    c = pl.program_id(1)
    pad = CHUNK - N_META
    prm = (cw, cb, dtb, an, dsk, nw)

    @pl.when(c == 0)
    def _():
        cbuf[0:SUBLANES, :] = jnp.zeros((SUBLANES, CONV_DIM), F32)
        hst[...] = jnp.zeros(hst.shape, F32)

        def stage(meta_ref):
            return jnp.concatenate([jnp.zeros((pad, meta_ref.shape[1]), F32), meta_ref[...]], axis=0)

        row = lax.broadcasted_iota(jnp.int32, (CHUNK, 1), 0)
        y = _ssd_chunk(stage(xbc_t), stage(z_t), stage(dt_t), row >= pad, *prm, cbuf, hst)
        ymbuf[...] = y[pad:, :]

    @pl.when(c > 0)
    def _():
        for j in range(xbc_m.shape[0] // CHUNK):
            rows = slice(j * CHUNK, (j + 1) * CHUNK)
            y = _ssd_chunk(xbc_m[rows, :], z_m[rows, :], dt_m[rows, :], None, *prm, cbuf, hst)
            y_o[rows, :] = y.astype(BF16)

    ym_o[...] = ymbuf[...].astype(BF16)
    hf_o[...] = hst[...].reshape(hf_o.shape)
    ct_o[...] = cbuf[0:SUBLANES, :]


def _chunks_per_step(nc):
    return 4 if nc % 4 == 0 else (2 if nc % 2 == 0 else 1)


def _ssd_prompt(proj_main, proj_tail, params, bsz, seq, meta_row):
    cps = _chunks_per_step(seq // CHUNK)
    blk = cps * CHUNK
    nb = seq // blk
    mb = meta_row // N_META

    def main(col):
        return lambda b, c: (b * nb + jnp.maximum(c - 1, 0), col)

    def meta(col):
        return lambda b, c: (mb, col)

    in_specs = [
        pl.BlockSpec((blk, CONV_DIM), main(COL_XBC // CONV_DIM)),
        pl.BlockSpec((blk, D_SSD), main(COL_Z // D_SSD)),
        pl.BlockSpec((blk, LANES), main(COL_DT // LANES)),
        pl.BlockSpec((N_META, CONV_DIM), meta(COL_XBC // CONV_DIM)),
        pl.BlockSpec((N_META, D_SSD), meta(COL_Z // D_SSD)),
        pl.BlockSpec((N_META, LANES), meta(COL_DT // LANES)),
    ] + _param_specs(params)
    return pl.pallas_call(
        _ssd_prompt_kernel,
        grid=(bsz, nb + 1),
        in_specs=in_specs,
        out_specs=[
            pl.BlockSpec((blk, D_SSD), main(0)),
            pl.BlockSpec((None, N_META, D_SSD), lambda b, c: (b, 0, 0)),
            pl.BlockSpec((None, N_SSD_HEADS, SSD_HEAD_DIM, D_STATE), lambda b, c: (b, 0, 0, 0)),
            pl.BlockSpec((None, SUBLANES, CONV_DIM), lambda b, c: (b, 0, 0)),
        ],
        out_shape=[
            jax.ShapeDtypeStruct((bsz * seq, D_SSD), BF16),
            jax.ShapeDtypeStruct((bsz, N_META, D_SSD), BF16),
            jax.ShapeDtypeStruct((bsz, N_SSD_HEADS, SSD_HEAD_DIM, D_STATE), F32),
            jax.ShapeDtypeStruct((bsz, SUBLANES, CONV_DIM), F32),
        ],
        scratch_shapes=[pltpu.VMEM((CHUNK + SUBLANES, CONV_DIM), F32),
                        pltpu.VMEM((N_SSD_HEADS * SSD_HEAD_DIM, D_STATE), F32),
                        pltpu.VMEM((N_META, D_SSD), F32)],
        compiler_params=_cparams(2),
        name="ssd_prompt",
    )(proj_main, proj_main, proj_main, proj_tail, proj_tail, proj_tail, *params)


def _expand_heads(a):
    hh = lax.broadcasted_iota(jnp.int32, (LANES, D_SSD), 0)
    cc = lax.broadcasted_iota(jnp.int32, (LANES, D_SSD), 1)
    sel = jnp.where(jnp.right_shift(cc, SSD_HEAD_DIM.bit_length() - 1) == hh, 1.0, 0.0).astype(BF16)
    p1, p2, p3 = _split3(a)
    return _dot(p1, sel) + _dot(p2, sel) + _dot(p3, sel)


def _pad_rows_bf16(x, rows):
    return jnp.concatenate([x, jnp.zeros((rows - x.shape[0], x.shape[1]), F32)], axis=0).astype(BF16)


def _ssd_sample_kernel(xbc_s, z_s, dt_s, cprev, h0, cw, cb, dtb, an, dsk, nw,
                       y_o, hf_o, ubuf, ybuf):
    nsq = cprev.shape[0]
    t = SUBLANES
    rows = nsq * t
    x = xbc_s[...]
    for s in range(nsq):
        ubuf[2 * t * s:2 * t * s + t, :] = cprev[s]
        ubuf[2 * t * s + t:2 * t * (s + 1), :] = x[s * t:(s + 1) * t, :]
    parts = []
    for s in range(nsq):
        acc = cb[...]
        for k in range(CONV_W):
            off = 2 * t * s + t - (CONV_W - 1) + k
            acc = acc + ubuf[off:off + t, :] * cw[k:k + 1, :]
        parts.append(acc)
    xc = _silu(jnp.concatenate(parts, axis=0))
    xs = xc[:, :D_SSD]
    bm = xc[:, D_SSD:D_SSD + N_SSD_GROUPS * D_STATE]
    cm = xc[:, D_SSD + N_SSD_GROUPS * D_STATE:]

    dt = jax.nn.softplus(dt_s[...] + dtb[...])
    da = dt * an[...]
    ri = lax.broadcasted_iota(jnp.int32, (rows, rows), 0)
    cj = lax.broadcasted_iota(jnp.int32, (rows, rows), 1)
    mask = jnp.logical_and(ri >= cj, jnp.right_shift(ri, 3) == jnp.right_shift(cj, 3))
    cs = _dot_exact(jnp.where(mask, 1.0, 0.0).astype(BF16), da)
    last = _dot_exact(jnp.where(cj == jnp.bitwise_or(ri, t - 1), 1.0, 0.0).astype(BF16), cs)
    cs_t = cs.T
    ecl = jnp.exp(last)
    xdt = xs * _expand_heads(dt)
    xw = xdt * _expand_heads(jnp.exp(last - cs))
    ecs_x = _expand_heads(jnp.exp(cs))
    skip = xs * dsk[...]

    gw = SSD_HEADS_PER_GROUP * SSD_HEAD_DIM
    for g in range(N_SSD_GROUPS):
        bg_f = bm[:, g * D_STATE:(g + 1) * D_STATE]
        cg_f = cm[:, g * D_STATE:(g + 1) * D_STATE]
        cbm = _dot_nt(cg_f.astype(BF16), bg_f.astype(BF16))
        for r in range(SSD_HEADS_PER_GROUP):
            h = g * SSD_HEADS_PER_GROUP + r
            hc = slice(h * SSD_HEAD_DIM, (h + 1) * SSD_HEAD_DIM)
            seg = cs[:, h:h + 1] - cs_t[h:h + 1, :]
            decay = jnp.where(mask, jnp.exp(jnp.where(mask, seg, 0.0)), 0.0)
            ybuf[:, hc] = _dot((decay * cbm).astype(BF16), xdt[:, hc].astype(BF16)) + skip[:, hc]
        gc = slice(g * gw, (g + 1) * gw)
        heads = slice(g * SSD_HEADS_PER_GROUP, (g + 1) * SSD_HEADS_PER_GROUP)
        for s in range(nsq):
            rs = slice(s * t, (s + 1) * t)
            hg = h0[s, heads].reshape(gw, D_STATE)
            y_off = _dot_nt(_pad_rows_bf16(cg_f[rs, :], 2 * t), hg.astype(BF16))[0:t, :]
            ybuf[rs, gc] = ybuf[rs, gc] + y_off * ecs_x[rs, gc]
            ecl_col = jnp.concatenate(
                [jnp.broadcast_to(ecl[s * t:s * t + 1, h:h + 1], (SSD_HEAD_DIM, 1))
                 for h in range(heads.start, heads.stop)], axis=0)
            h_new = hg * ecl_col + _dot_tn(_pad_rows_bf16(xw[rs, gc], 2 * t), _pad_rows_bf16(bg_f[rs, :], 2 * t))
            hf_o[s, heads] = h_new.reshape(SSD_HEADS_PER_GROUP, SSD_HEAD_DIM, D_STATE)

    y_o[...] = _ssd_gate_norm(ybuf[...], z_s[...], nw).astype(BF16)


def _ssd_sample(proj_tail, conv_prev, h0, params, nseq, t, nsq):
    blk = nsq * t
    in_specs = [
        pl.BlockSpec((blk, CONV_DIM), lambda b: (b, COL_XBC // CONV_DIM)),
        pl.BlockSpec((blk, D_SSD), lambda b: (b, COL_Z // D_SSD)),
        pl.BlockSpec((blk, LANES), lambda b: (b, COL_DT // LANES)),
        pl.BlockSpec((nsq, SUBLANES, CONV_DIM), lambda b: (b, 0, 0)),
        pl.BlockSpec((nsq, N_SSD_HEADS, SSD_HEAD_DIM, D_STATE), lambda b: (b, 0, 0, 0)),
    ] + _param_specs(params)
    return pl.pallas_call(
        _ssd_sample_kernel,
        grid=(nseq // nsq,),
        in_specs=in_specs,
        out_specs=[
            pl.BlockSpec((blk, D_SSD), lambda b: (b, 0)),
            pl.BlockSpec((nsq, N_SSD_HEADS, SSD_HEAD_DIM, D_STATE), lambda b: (b, 0, 0, 0)),
        ],
        out_shape=[
            jax.ShapeDtypeStruct((nseq * t, D_SSD), BF16),
            jax.ShapeDtypeStruct((nseq, N_SSD_HEADS, SSD_HEAD_DIM, D_STATE), F32),
        ],
        scratch_shapes=[pltpu.VMEM((2 * blk, CONV_DIM), F32), pltpu.VMEM((blk, D_SSD), F32)],
        compiler_params=_cparams(1),
        name="ssd_sample",
    )(proj_tail, proj_tail, proj_tail, conv_prev, h0, *params)


def _qk_prep(x, w, cos, sin):
    lane = lax.broadcasted_iota(jnp.int32, x.shape, 1)
    lo_head = lane < HEAD_DIM
    sq = x * x
    s_lo = jnp.sum(jnp.where(lo_head, sq, 0.0), axis=-1, keepdims=True)
    s_all = jnp.sum(sq, axis=-1, keepdims=True)
    ms = jnp.where(lo_head, s_lo, s_all - s_lo) * (1.0 / HEAD_DIM)
    xn = x * lax.rsqrt(ms + EPS) * w
    half = HEAD_DIM // 2
    first_half = (lane % HEAD_DIM) < half
    partner = jnp.where(first_half, pltpu.roll(xn, LANES - half, 1), pltpu.roll(xn, half, 1))
    return xn * cos + partner * sin


def _q_groups(q, qnw_ref, cos, sin):
    return [_qk_prep(q[:, g * LANES:(g + 1) * LANES], qnw_ref[:, g * LANES:(g + 1) * LANES], cos, sin)
            * ATTN_SCALE for g in range(D_ATTN // LANES)]


def _sink_column(snk_ref, rows):
    return jnp.concatenate([jnp.broadcast_to(snk_ref[:, h:h + 1], (rows, 1)) for h in range(N_Q_HEADS)],
                           axis=0)


def _softmax_weights(pieces, sk):
    top = pieces[0]
    for p in pieces[1:]:
        top = jnp.maximum(top, p)
    m = jnp.maximum(jnp.max(top, axis=-1, keepdims=True), sk)
    e = [jnp.exp(p - m) for p in pieces]
    tot = e[0]
    for p in e[1:]:
        tot = tot + p
    return e, 1.0 / (jnp.sum(tot, axis=-1, keepdims=True) + jnp.exp(sk - m))


def _kv_prep_t(k, v, knw_ref, cos, sin):
    kp = _qk_prep(k, knw_ref[...], cos, sin)
    lane = lax.broadcasted_iota(jnp.int32, kp.shape, 1)
    return (kp, jnp.where(lane < HEAD_DIM, kp, 0.0).astype(BF16),
            jnp.where(lane < HEAD_DIM, 0.0, kp).astype(BF16), v.T.astype(BF16))


def _scores_t(qs, key_lo, key_hi):
    return jnp.concatenate([_dot_nt(key_lo, qs), _dot_nt(key_hi, qs)], axis=1)


def _softmax_weights_t(pieces, sk):
    m = sk
    for p in pieces:
        m = jnp.maximum(m, jnp.max(p, axis=0, keepdims=True))
    e = [jnp.exp(p - m) for p in pieces]
    den = jnp.exp(sk - m)
    for p in e:
        den = den + jnp.sum(p, axis=0, keepdims=True)
    return e, 1.0 / den


def _meta_weights(e_m):
    return jnp.concatenate([e_m.astype(BF16), jnp.zeros((CHUNK - N_META, e_m.shape[1]), BF16)], axis=0)


def _attn_out_t(o_t, anw_ref):
    nst = Q_PER_KV * CHUNK
    d = lax.broadcasted_iota(jnp.int32, (LANES, CHUNK), 0)
    groups = [jnp.where(d < HEAD_DIM, o_t[:, g * CHUNK:(g + 1) * CHUNK],
                        o_t[:, nst + g * CHUNK:nst + (g + 1) * CHUNK]).T for g in range(Q_PER_KV)]
    return _rms(jnp.concatenate(groups, axis=1), anw_ref[...])


def _attn_prompt_kernel(q_m, k_m, v_m, q_t, k_t, v_t, cos_m, sin_m, cos_t, sin_t, qnw, knw, snk, anw,
                        o_o, om_o, kpm_o, kl_o, vl_o,
                        km_lo, km_hi, vmt, kp_lo, kp_hi, vpt, kmf, ombuf):
    c = pl.program_id(1)
    pad = CHUNK - N_META
    cols = N_Q_HEADS * CHUNK
    sk = jnp.concatenate([jnp.broadcast_to(snk[:, h:h + 1], (1, CHUNK)) for h in range(N_Q_HEADS)], axis=1)

    @pl.when(c == 0)
    def _():
        def stage(meta_ref):
            return jnp.concatenate([jnp.zeros((pad, meta_ref.shape[1]), F32), meta_ref[...]], axis=0)

        cs, sn = cos_t[...], sin_t[...]
        kp = _qk_prep(stage(k_t), knw[...], cs, sn)
        lane = lax.broadcasted_iota(jnp.int32, (N_META, LANES), 1)
        km_lo[...] = jnp.where(lane < HEAD_DIM, kp[pad:, :], 0.0).astype(BF16)
        km_hi[...] = jnp.where(lane < HEAD_DIM, 0.0, kp[pad:, :]).astype(BF16)
        kmf[...] = kp[pad:, :]
        vmt[...] = jnp.concatenate([v_t[...], jnp.zeros((pad, LANES), F32)], axis=0).T.astype(BF16)
        kp_lo[...] = jnp.zeros(kp_lo.shape, BF16)
        kp_hi[...] = jnp.zeros(kp_hi.shape, BF16)
        vpt[...] = jnp.zeros(vpt.shape, BF16)
        qs = jnp.concatenate(_q_groups(stage(q_t), qnw, cs, sn), axis=0).astype(BF16)
        r = lax.broadcasted_iota(jnp.int32, (N_META, cols), 0)
        qi = lax.broadcasted_iota(jnp.int32, (N_META, cols), 1) & (CHUNK - 1)
        s_m = jnp.where(r <= qi - pad, _scores_t(qs, km_lo[...], km_hi[...]), NEG)
        (e_m,), inv = _softmax_weights_t([s_m], sk)
        ombuf[...] = _attn_out_t(_dot(vmt[...], _meta_weights(e_m)) * inv, anw)[pad:, :]

    @pl.when(c > 0)
    def _():
        nblk = q_m.shape[0] // CHUNK
        prev = (kp_lo[...], kp_hi[...], vpt[...])
        r = lax.broadcasted_iota(jnp.int32, (CHUNK, cols), 0)
        qi = lax.broadcasted_iota(jnp.int32, (CHUNK, cols), 1) & (CHUNK - 1)
        tri = r <= qi
        for j in range(nblk):
            rows = slice(j * CHUNK, (j + 1) * CHUNK)
            cs, sn = cos_m[rows, :], sin_m[rows, :]
            v = v_m[rows, :]
            kp, k_lo, k_hi, v_tb = _kv_prep_t(k_m[rows, :], v, knw, cs, sn)
            qs = jnp.concatenate(_q_groups(q_m[rows, :], qnw, cs, sn), axis=0).astype(BF16)
            band = jnp.where(tri, _scores_t(qs, k_lo, k_hi), _scores_t(qs, prev[0], prev[1]))
            if j == 0:
                band = jnp.where(jnp.logical_or(tri, c > 1), band, NEG)
            (e_b, e_m), inv = _softmax_weights_t([band, _scores_t(qs, km_lo[...], km_hi[...])], sk)
            o_t = (_dot(v_tb, jnp.where(tri, e_b, 0.0).astype(BF16))
                   + _dot(prev[2], jnp.where(tri, 0.0, e_b).astype(BF16))
                   + _dot(vmt[...], _meta_weights(e_m))) * inv
            o_o[rows, :] = _attn_out_t(o_t, anw).astype(BF16)
            prev = (k_lo, k_hi, v_tb)
            if j == nblk - 1:
                kl_o[...] = kp
                vl_o[...] = v
        kp_lo[...], kp_hi[...], vpt[...] = prev

    om_o[...] = ombuf[...].astype(BF16)
    kpm_o[...] = kmf[...]


def _attn_sample_kernel(q_s, k_s, v_s, mk, mv, wk, wv, cos, sin, qnw, knw, snk, anw,
                        o_o, kp_o, obuf):
    nsq = mk.shape[0]
    t = SUBLANES
    nst = Q_PER_KV * t
    cs, sn = cos[...], sin[...]
    kp = _qk_prep(k_s[...], knw[...], cs, sn)
    kp_o[...] = kp
    v = v_s[...]
    qg = _q_groups(q_s[...], qnw, cs, sn)

    lo = lax.broadcasted_iota(jnp.int32, (nst, LANES), 1) < HEAD_DIM
    i_q = lax.broadcasted_iota(jnp.int32, (2 * nst, LANES), 0) & (t - 1)
    cj = lax.broadcasted_iota(jnp.int32, (2 * nst, LANES), 1)
    mask_a = cj > i_q
    mask_b = jnp.logical_or(cj <= i_q, jnp.logical_and(cj >= t, cj < t + N_META))
    sk = _sink_column(snk, t)
    zpad = jnp.zeros((WINDOW - t - N_META, LANES), F32)
    for s in range(nsq):
        rows = slice(s * t, (s + 1) * t)
        q_st = jnp.concatenate([g[rows, :] for g in qg], axis=0)
        q2 = jnp.concatenate([jnp.where(lo, q_st, 0.0), jnp.where(lo, 0.0, q_st)], axis=0).astype(BF16)
        k_b = jnp.concatenate([kp[rows, :], mk[s], zpad], axis=0).astype(BF16)
        v_b = jnp.concatenate([v[rows, :], mv[s], zpad], axis=0).astype(BF16)
        (e_a, e_b), inv = _softmax_weights(
            [jnp.where(mask_a, _dot_nt(q2, wk[s].astype(BF16)), NEG),
             jnp.where(mask_b, _dot_nt(q2, k_b), NEG)], sk)
        o2 = (_dot(e_a.astype(BF16), wv[s].astype(BF16)) + _dot(e_b.astype(BF16), v_b)) * inv
        o_st = jnp.where(lo, o2[0:nst, :], o2[nst:2 * nst, :])
        for g in range(Q_PER_KV):
            obuf[rows, g * LANES:(g + 1) * LANES] = o_st[g * t:(g + 1) * t, :]
    o_o[...] = _rms(obuf[...], anw[...]).astype(BF16)


def _attn_prompt(proj_main, proj_tail, tabs_main, tabs_meta, params, bsz, seq, meta_row):
    blk = _chunks_per_step(seq // CHUNK) * CHUNK
    nb = seq // blk
    mb = meta_row // N_META

    def main(col):
        return lambda b, c: (b * nb + jnp.maximum(c - 1, 0), col)

    def meta(col):
        return lambda b, c: (mb, col)

    per_batch = lambda b, c: (b, 0, 0)
    in_specs = [
        pl.BlockSpec((blk, D_ATTN), main(COL_Q // D_ATTN)),
        pl.BlockSpec((blk, KV_DIM), main(COL_K // KV_DIM)),
        pl.BlockSpec((blk, KV_DIM), main(COL_V // KV_DIM)),
        pl.BlockSpec((N_META, D_ATTN), meta(COL_Q // D_ATTN)),
        pl.BlockSpec((N_META, KV_DIM), meta(COL_K // KV_DIM)),
        pl.BlockSpec((N_META, KV_DIM), meta(COL_V // KV_DIM)),
        pl.BlockSpec((blk, LANES), lambda b, c: (jnp.maximum(c - 1, 0), 0)),
        pl.BlockSpec((blk, LANES), lambda b, c: (jnp.maximum(c - 1, 0), 0)),
        pl.BlockSpec((CHUNK, LANES), lambda b, c: (0, 0)),
        pl.BlockSpec((CHUNK, LANES), lambda b, c: (0, 0)),
    ] + _param_specs(params)
    return pl.pallas_call(
        _attn_prompt_kernel,
        grid=(bsz, nb + 1),
        in_specs=in_specs,
        out_specs=[
            pl.BlockSpec((blk, D_ATTN), main(0)),
            pl.BlockSpec((None, N_META, D_ATTN), per_batch),
            pl.BlockSpec((None, N_META, KV_DIM), per_batch),
            pl.BlockSpec((None, CHUNK, KV_DIM), per_batch),
            pl.BlockSpec((None, CHUNK, KV_DIM), per_batch),
        ],
        out_shape=[
            jax.ShapeDtypeStruct((bsz * seq, D_ATTN), BF16),
            jax.ShapeDtypeStruct((bsz, N_META, D_ATTN), BF16),
            jax.ShapeDtypeStruct((bsz, N_META, KV_DIM), F32),
            jax.ShapeDtypeStruct((bsz, CHUNK, KV_DIM), F32),
            jax.ShapeDtypeStruct((bsz, CHUNK, KV_DIM), F32),
        ],
        scratch_shapes=[pltpu.VMEM((N_META, KV_DIM), BF16) for _ in range(2)] + [
            pltpu.VMEM((CHUNK, KV_DIM), BF16) for _ in range(4)] + [
            pltpu.VMEM((N_META, KV_DIM), F32), pltpu.VMEM((N_META, D_ATTN), F32),
        ],
        compiler_params=_cparams(2),
        name="attn_prompt",
    )(proj_main, proj_main, proj_main, proj_tail, proj_tail, proj_tail, *tabs_main, *tabs_meta, *params)


def _attn_sample(proj_tail, mk, mv, wk, wv, cos, sin, params, nseq, t, nsq):
    blk = nsq * t
    per_seq = lambda b: (b, 0, 0)
    in_specs = [
        pl.BlockSpec((blk, D_ATTN), lambda b: (b, COL_Q // D_ATTN)),
        pl.BlockSpec((blk, KV_DIM), lambda b: (b, COL_K // KV_DIM)),
        pl.BlockSpec((blk, KV_DIM), lambda b: (b, COL_V // KV_DIM)),
        pl.BlockSpec((nsq, N_META, KV_DIM), per_seq),
        pl.BlockSpec((nsq, N_META, KV_DIM), per_seq),
        pl.BlockSpec((nsq, WINDOW, KV_DIM), per_seq),
        pl.BlockSpec((nsq, WINDOW, KV_DIM), per_seq),
        pl.BlockSpec((blk, LANES), lambda b: (0, 0)),
        pl.BlockSpec((blk, LANES), lambda b: (0, 0)),
    ] + _param_specs(params)
    return pl.pallas_call(
        _attn_sample_kernel,
        grid=(nseq // nsq,),
        in_specs=in_specs,
        out_specs=[
            pl.BlockSpec((blk, D_ATTN), lambda b: (b, 0)),
            pl.BlockSpec((blk, KV_DIM), lambda b: (b, 0)),
        ],
        out_shape=[
            jax.ShapeDtypeStruct((nseq * t, D_ATTN), BF16),
            jax.ShapeDtypeStruct((nseq * t, KV_DIM), F32),
        ],
        scratch_shapes=[pltpu.VMEM((blk, D_ATTN), F32)],
        compiler_params=_cparams(1),
        name="attn_sample",
    )(proj_tail, proj_tail, proj_tail, mk, mv, wk, wv, cos, sin, *params)


def _mix_out(n_main, x_refs, ys_refs, ya_refs, wo_ref):
    return (_pick(n_main, *x_refs) + _dot(_pick(n_main, *ys_refs), wo_ref[0:D_SSD, :])
            + _dot(_pick(n_main, *ya_refs), wo_ref[D_SSD:D_SSD + D_ATTN, :]))


def _swiglu_acc(hn, wg_ref, wu_ref, wd_ref, acc_ref):
    for j in range(D_FF // FF_CHUNK):
        cols = slice(j * FF_CHUNK, (j + 1) * FF_CHUNK)
        a = (_silu(_dot(hn, wg_ref[:, cols])) * _dot(hn, wu_ref[:, cols])).astype(BF16)
        acc_ref[...] += _dot(a, wd_ref[cols, :])


def _out_ffn_kernel(n_main, xa, xb, ysa, ysb, yaa, yab, wo_ref, nw_ref, wg_ref, wu_ref, wd_ref,
                    om_ref, ot_ref):
    xm = _mix_out(n_main, (xa, xb), (ysa, ysb), (yaa, yab), wo_ref)
    om_ref[...] = xm
    _swiglu_acc(_rms(xm, nw_ref[...]).astype(BF16), wg_ref, wu_ref, wd_ref, om_ref)
    ot_ref[...] = om_ref[...]


def _out_ffn(x, ys, ya, wo, nw, wg, wu, wd, n_main):
    n_tail = x[1].shape[0] // ROW_TILE
    return pl.pallas_call(
        functools.partial(_out_ffn_kernel, n_main),
        grid=(n_main + n_tail,),
        in_specs=_src_specs(D_MODEL, n_main) + _src_specs(D_SSD, n_main) + _src_specs(D_ATTN, n_main) + [
            _resident(wo.shape), _resident(nw.shape),
            _resident(wg.shape), _resident(wu.shape), _resident(wd.shape),
        ],
        out_specs=_dst_specs(D_MODEL, n_main),
        out_shape=_dst_shapes(D_MODEL, n_main, n_tail, F32),
        compiler_params=_cparams(1),
        name="out_ffn",
    )(*x, *ys, *ya, wo, nw, wg, wu, wd)


def _out_router_kernel(n_main, n_tok, xa, xb, ysa, ysb, yaa, yab, wo_ref, nw_ref, wr_hi_ref, wr_lo_ref,
                       before_ref, xm_o, rt_o, cnt_o):
    xm = _mix_out(n_main, (xa, xb), (ysa, ysb), (yaa, yab), wo_ref)
    xm_o[...] = xm
    hn = _rms(xm, nw_ref[...])
    hi = hn.astype(BF16)
    lo = (hn - hi.astype(F32)).astype(BF16)
    logits = _dot(hi, wr_hi_ref[...]) + _dot(lo, wr_hi_ref[...]) + _dot(hi, wr_lo_ref[...])
    lane = lax.broadcasted_iota(jnp.int32, logits.shape, 1)
    logits = jnp.where(lane < N_EXPERTS, logits, -jnp.inf)
    v1 = jnp.max(logits, axis=-1, keepdims=True)
    i1 = jnp.min(jnp.where(logits == v1, lane, LANES), axis=-1, keepdims=True)
    rest = jnp.where(lane == i1, -jnp.inf, logits)
    v2 = jnp.max(rest, axis=-1, keepdims=True)
    i2 = jnp.min(jnp.where(rest == v2, lane, LANES), axis=-1, keepdims=True)
    e2 = jnp.exp(v2 - v1)
    g1 = 1.0 / (1.0 + e2)
    g2 = e2 / (1.0 + e2)

    row = pl.program_id(0) * ROW_TILE + lax.broadcasted_iota(jnp.int32, (ROW_TILE, 1), 0)
    valid = row < n_tok
    oh1 = jnp.where(jnp.logical_and(lane == i1, valid), 1.0, 0.0)
    oh2 = jnp.where(jnp.logical_and(lane == i2, valid), 1.0, 0.0)
    before = before_ref[...]
    c1 = _dot(before, oh1.astype(BF16))
    c2 = _dot(before, oh2.astype(BF16))
    tot1 = jnp.sum(oh1, axis=0, keepdims=True)
    tot2 = jnp.sum(oh2, axis=0, keepdims=True)
    rank1 = jnp.sum(jnp.where(lane == i1, c1, 0.0), axis=-1, keepdims=True)
    rank2 = jnp.sum(jnp.where(lane == i2, c2 + tot1, 0.0), axis=-1, keepdims=True)
    cnt_o[...] = jnp.broadcast_to(tot1 + tot2, cnt_o.shape)
    route = jnp.where(lane == 0, i1.astype(F32), 0.0)
    for k, val in enumerate((i2.astype(F32), g1, g2, rank1, rank2)):
        route = jnp.where(lane == k + 1, val, route)
    rt_o[...] = route


def _out_router(x, ys, ya, wo, nw, wr_hi, wr_lo, n_main, n_tok):
    n_tiles = n_main + x[1].shape[0] // ROW_TILE
    rows = n_tiles * ROW_TILE
    before = jnp.tril(jnp.ones((ROW_TILE, ROW_TILE), BF16), -1)
    return pl.pallas_call(
        functools.partial(_out_router_kernel, n_main, n_tok),
        grid=(n_tiles,),
        in_specs=_src_specs(D_MODEL, n_main) + _src_specs(D_SSD, n_main) + _src_specs(D_ATTN, n_main) + [
            _resident(wo.shape), _resident(nw.shape), _resident(wr_hi.shape), _resident(wr_lo.shape),
            _resident(before.shape),
        ],
        out_specs=[
            pl.BlockSpec((ROW_TILE, D_MODEL), lambda i: (i, 0)),
            pl.BlockSpec((ROW_TILE, LANES), lambda i: (i, 0)),
            pl.BlockSpec((None, SUBLANES, LANES), lambda i: (i, 0, 0)),
        ],
        out_shape=[
            jax.ShapeDtypeStruct((rows, D_MODEL), F32),
            jax.ShapeDtypeStruct((rows, LANES), F32),
            jax.ShapeDtypeStruct((n_tiles, SUBLANES, LANES), F32),
        ],
        compiler_params=_cparams(1),
        name="out_router",
    )(*x, *ys, *ya, wo, nw, wr_hi, wr_lo, before)


def _tile_rows(idx, n_tiles, tail, fn):
    if tail == ROW_TILE:
        fn(ROW_TILE)
    else:
        pl.when(idx < n_tiles - 1)(lambda: fn(ROW_TILE))
        pl.when(idx == n_tiles - 1)(lambda: fn(tail))


def _dispatch_kernel(n_tiles, tail, dest_ref, x_ref, xs_in_ref, xs_ref, sem):
    del xs_in_ref

    def run(nrows):
        def body(r, carry):
            for k in range(TOP_K):
                d = dest_ref[0, 0, TOP_K * r + k]
                pltpu.make_async_copy(x_ref.at[pl.ds(r, 1)], xs_ref.at[pl.ds(d, 1)], sem).start()
            return carry

        lax.fori_loop(0, nrows, body, 0, unroll=DMA_UNROLL)
        for k in range(TOP_K):
            pltpu.make_async_copy(x_ref.at[pl.ds(0, nrows)], xs_ref.at[pl.ds(0, nrows)], sem).wait()

    _tile_rows(pl.program_id(0), n_tiles, tail, run)


def _dispatch(dest, xm, n_tok, m_rows):
    n_tiles = xm.shape[0] // ROW_TILE
    tail = n_tok - (n_tiles - 1) * ROW_TILE
    return pl.pallas_call(
        functools.partial(_dispatch_kernel, n_tiles, tail),
        grid=(n_tiles,),
        in_specs=[
            pl.BlockSpec((1, 1, TOP_K * ROW_TILE), lambda i: (i, 0, 0), memory_space=pltpu.SMEM),
            pl.BlockSpec((ROW_TILE, D_MODEL), lambda i: (i, 0)),
            pl.BlockSpec(memory_space=pl.ANY),
        ],
        out_specs=pl.BlockSpec(memory_space=pl.ANY),
        out_shape=jax.ShapeDtypeStruct((m_rows, D_MODEL), F32),
        scratch_shapes=[pltpu.SemaphoreType.DMA(())],
        input_output_aliases={2: 0},
        compiler_params=_cparams(1, has_side_effects=True),
        name="moe_dispatch",
    )(dest, xm, jnp.zeros((m_rows, D_MODEL), F32))


def _moe_kernel(te_ref, nu_ref, x_ref, nw_ref, wg_ref, wu_ref, wd_ref, o_ref):
    i = pl.program_id(0)
    o_ref[...] = jnp.zeros(o_ref.shape, F32)

    @pl.when(i < nu_ref[0])
    def _():
        _swiglu_acc(_rms(x_ref[...], nw_ref[...]).astype(BF16), wg_ref, wu_ref, wd_ref, o_ref)


def _moe_experts(tile_e, n_used, xs, nw, wg, wu, wd):
    rows = xs.shape[0]
    grid_spec = pltpu.PrefetchScalarGridSpec(
        num_scalar_prefetch=2,
        grid=(rows // ROW_TILE,),
        in_specs=[
            pl.BlockSpec((ROW_TILE, D_MODEL), lambda i, te, nu: (i, 0)),
            pl.BlockSpec((1, D_MODEL), lambda i, te, nu: (0, 0)),
            pl.BlockSpec((None, D_MODEL, D_FF), lambda i, te, nu: (te[i], 0, 0)),
            pl.BlockSpec((None, D_MODEL, D_FF), lambda i, te, nu: (te[i], 0, 0)),
            pl.BlockSpec((None, D_FF, D_MODEL), lambda i, te, nu: (te[i], 0, 0)),
        ],
        out_specs=pl.BlockSpec((ROW_TILE, D_MODEL), lambda i, te, nu: (i, 0)),
    )
    return pl.pallas_call(
        _moe_kernel,
        grid_spec=grid_spec,
        out_shape=jax.ShapeDtypeStruct((rows, D_MODEL), F32),
        compiler_params=_cparams(1),
        name="moe_experts",
    )(tile_e, n_used, xs, nw, wg, wu, wd)


def _combine_kernel(n_main, n_tiles, tail, dcur_ref, dnext_ref, xm_ref, rt_ref, yb_ref,
                    om_ref, ot_ref, gbuf, sem):
    i = pl.program_id(0)

    def issue(dref, slot, nrows):
        def body(r, carry):
            for k in range(TOP_K):
                d = dref[0, 0, TOP_K * r + k]
                pltpu.make_async_copy(yb_ref.at[pl.ds(d, 1)], gbuf.at[slot, k, pl.ds(r, 1)],
                                      sem.at[slot]).start()
            return carry

        lax.fori_loop(0, nrows, body, 0, unroll=DMA_UNROLL)

    def wait(slot, nrows):
        for k in range(TOP_K):
            pltpu.make_async_copy(yb_ref.at[pl.ds(0, nrows)], gbuf.at[slot, k, pl.ds(0, nrows)],
                                  sem.at[slot]).wait()

    @pl.when(i == 0)
    def _():
        gbuf[...] = jnp.zeros(gbuf.shape, F32)
        _tile_rows(i, n_tiles, tail, lambda n: issue(dcur_ref, 0, n))

    @pl.when(i + 1 < n_tiles)
    def _():
        _tile_rows(i + 1, n_tiles, tail, lambda n: issue(dnext_ref, (i + 1) % 2, n))

    slot = i % 2
    _tile_rows(i, n_tiles, tail, lambda n: wait(slot, n))
    gates = rt_ref[...]
    val = (xm_ref[...] + gates[:, TOP_K:TOP_K + 1] * gbuf[slot, 0]
           + gates[:, TOP_K + 1:TOP_K + 2] * gbuf[slot, 1])

    @pl.when(i < n_main)
    def _():
        om_ref[...] = val

    @pl.when(i >= n_main)
    def _():
        ot_ref[...] = val


def _combine(dest, xm, route, yb, n_main, n_tok):
    n_tiles = xm.shape[0] // ROW_TILE
    tail = n_tok - (n_tiles - 1) * ROW_TILE
    dspec = lambda f: pl.BlockSpec((1, 1, TOP_K * ROW_TILE), f, memory_space=pltpu.SMEM)
    return pl.pallas_call(
        functools.partial(_combine_kernel, n_main, n_tiles, tail),
        grid=(n_tiles,),
        in_specs=[
            dspec(lambda i: (i, 0, 0)),
            dspec(lambda i: (jnp.minimum(i + 1, n_tiles - 1), 0, 0)),
            pl.BlockSpec((ROW_TILE, D_MODEL), lambda i: (i, 0)),
            pl.BlockSpec((ROW_TILE, LANES), lambda i: (i, 0)),
            pl.BlockSpec(memory_space=pl.ANY),
        ],
        out_specs=_src_specs(D_MODEL, n_main),
        out_shape=[jax.ShapeDtypeStruct((n_main * ROW_TILE, D_MODEL), F32),
                   jax.ShapeDtypeStruct(((n_tiles - n_main) * ROW_TILE, D_MODEL), F32)],
        scratch_shapes=[pltpu.VMEM((2, TOP_K, ROW_TILE, D_MODEL), F32), pltpu.SemaphoreType.DMA((2,))],
        compiler_params=_cparams(1),
        name="moe_combine",
    )(dest, dest, xm, route, yb)


def _moe_layer(xm, route, counts, n_main, n_tok, nw, wg, wu, wd):
    n_tiles = xm.shape[0] // ROW_TILE
    m_tiles = -(-(n_tok * TOP_K + N_EXPERTS * (ROW_TILE - 1)) // ROW_TILE)
    cnt = counts[:, 0, :N_EXPERTS].astype(jnp.int32)
    total = jnp.sum(cnt, axis=0)
    padded = (total + ROW_TILE - 1) // ROW_TILE * ROW_TILE
    pad_end = jnp.cumsum(padded)
    base = (pad_end - padded)[None, :] + jnp.cumsum(cnt, axis=0) - cnt
    e = route[:, 0:TOP_K].astype(jnp.int32).reshape(n_tiles, ROW_TILE, TOP_K)
    rank = route[:, 2 * TOP_K:3 * TOP_K].astype(jnp.int32).reshape(n_tiles, ROW_TILE, TOP_K)
    onehot = e[..., None] == jnp.arange(N_EXPERTS, dtype=jnp.int32)
    dest = jnp.sum(jnp.where(onehot, base[:, None, None, :], 0), axis=-1) + rank
    dest = dest.reshape(n_tiles, 1, ROW_TILE * TOP_K)
    tile_start = jnp.arange(m_tiles, dtype=jnp.int32) * ROW_TILE
    tile_e = jnp.minimum(jnp.sum((pad_end[None, :] <= tile_start[:, None]).astype(jnp.int32), axis=1),
                         N_EXPERTS - 1)
    n_used = (pad_end[-1:] // ROW_TILE).astype(jnp.int32)
    xs = _dispatch(dest, xm, n_tok, m_tiles * ROW_TILE)
    yb = _moe_experts(tile_e, n_used, xs, nw, wg, wu, wd)
    return _combine(dest, xm, route, yb, n_main, n_tok)


def _rope_tables(pos):
    half = HEAD_DIM // 2
    inv_freq = ROPE_THETA ** (-jnp.arange(half, dtype=F32) / half)
    ang = pos.astype(F32)[:, None] * inv_freq[None, :]
    cos = jnp.cos(ang)
    sin = jnp.sin(ang)
    reps = LANES // HEAD_DIM
    return (jnp.tile(jnp.concatenate([cos, cos], axis=-1), (1, reps)),
            jnp.tile(jnp.concatenate([-sin, sin], axis=-1), (1, reps)))


def _pad_lanes(v, width=LANES):
    v = v.astype(F32).reshape(1, -1)
    return jnp.pad(v, ((0, 0), (0, width - v.shape[1])))


def kernel(x_prompt, x_sample, state_ssm, state_conv, cache_meta_k, cache_meta_v, cache_win_k, cache_win_v, meta_tokens, norm_mix_w, w_in, conv_w, conv_b, dt_bias, a_log, d_skip, ssd_norm_w, q_norm_w, k_norm_w, sinks, attn_norm_w, w_out, norm_ffn_w, w_gate, w_up, w_down, w_router, moe_w_gate, moe_w_up, moe_w_down):
    bsz, seq, _ = x_prompt.shape
    nseq, t_s, _ = x_sample.shape
    depth = w_in.shape[0]
    r_main = bsz * seq
    r_samp = nseq * t_s
    assert seq % CHUNK == 0 and t_s == SUBLANES and r_main % ROW_TILE == 0 and r_samp % N_META == 0
    n_main = r_main // ROW_TILE
    n_tok = r_main + r_samp + N_META
    r_tail = -(-(r_samp + N_META) // ROW_TILE) * ROW_TILE
    tail_pad = r_tail - r_samp - N_META

    x = (x_prompt.reshape(r_main, D_MODEL),
         jnp.concatenate([x_sample.reshape(r_samp, D_MODEL), meta_tokens.astype(F32),
                          jnp.zeros((tail_pad, D_MODEL), F32)], axis=0))

    tabs_main = _rope_tables(N_META + jnp.arange(seq, dtype=jnp.int32))
    tabs_meta = _rope_tables(jnp.arange(CHUNK, dtype=jnp.int32) - (CHUNK - N_META))
    nsq = 16 if nseq % 16 == 0 else nseq
    cos_s, sin_s = (jnp.tile(tab, (nsq, 1)) for tab in
                    _rope_tables(PAST_LEN + jnp.arange(t_s, dtype=jnp.int32)))

    o_z, o_xbc, o_dt, o_q, o_k, o_v = 0, 512, 1536, 1544, 2056, 2184
    col = jnp.arange(D_ATTN, dtype=jnp.int32)
    grp, lane = col // LANES, col % LANES
    head_perm = (grp + Q_PER_KV * (lane // HEAD_DIM)) * HEAD_DIM + lane % HEAD_DIM

    def heads4(a):
        return a.reshape(a.shape[0], a.shape[1], N_KV_HEADS, HEAD_DIM)

    outs = {k: [] for k in ('p_ssm', 'p_conv', 'p_mk', 'p_mv', 'p_wk', 'p_wv', 's_ssm', 's_conv', 's_wk', 's_wv')}
    for l in range(depth):
        wl = w_in[l]
        w_re = jnp.concatenate([
            wl[:, o_xbc:o_xbc + CONV_DIM], wl[:, o_z:o_z + D_SSD], wl[:, o_q:o_q + D_ATTN][:, head_perm],
            wl[:, o_k:o_k + KV_DIM], wl[:, o_v:o_v + KV_DIM], wl[:, o_dt:o_dt + N_SSD_HEADS],
            jnp.zeros((D_MODEL, PROJ_W - COL_DT - N_SSD_HEADS), wl.dtype)], axis=1).astype(BF16)
        proj_main, proj_tail = _in_proj(x[0], x[1], norm_mix_w[l].reshape(1, D_MODEL).astype(F32), w_re, n_main)

        ssd_params = (conv_w[l].astype(F32), conv_b[l].reshape(1, CONV_DIM).astype(F32),
                      _pad_lanes(dt_bias[l]), _pad_lanes(-jnp.exp(a_log[l].astype(F32))),
                      jnp.repeat(d_skip[l].astype(F32), SSD_HEAD_DIM).reshape(1, D_SSD),
                      ssd_norm_w[l].reshape(1, D_SSD).astype(F32))
        ys_p, ys_m, ssm_p, conv_p = _ssd_prompt(proj_main, proj_tail, ssd_params, bsz, seq, r_samp)
        conv_prev = jnp.pad(state_conv[l].astype(F32), ((0, 0), (SUBLANES - (CONV_W - 1), 0), (0, 0)))
        ys_s, ssm_s = _ssd_sample(proj_tail, conv_prev, state_ssm[l].astype(F32), ssd_params, nseq, t_s, nsq)

        attn_params = (jnp.tile(q_norm_w[l].astype(F32), N_Q_HEADS).reshape(1, D_ATTN),
                       jnp.tile(k_norm_w[l].astype(F32), N_KV_HEADS).reshape(1, KV_DIM),
                       _pad_lanes(sinks[l]), attn_norm_w[l][head_perm].reshape(1, D_ATTN).astype(F32))
        ya_p, ya_m, kp_m, k_last, v_last = _attn_prompt(proj_main, proj_tail, tabs_main, tabs_meta,
                                                        attn_params, bsz, seq, r_samp)
        ya_s, kp_s = _attn_sample(
            proj_tail, cache_meta_k[l].reshape(nseq, N_META, KV_DIM).astype(F32),
            cache_meta_v[l].reshape(nseq, N_META, KV_DIM).astype(F32),
            cache_win_k[l].reshape(nseq, WINDOW, KV_DIM).astype(F32),
            cache_win_v[l].reshape(nseq, WINDOW, KV_DIM).astype(F32),
            cos_s, sin_s, attn_params, nseq, t_s, nsq)

        ys = (ys_p, jnp.concatenate([ys_s, ys_m[0], jnp.zeros((tail_pad, D_SSD), BF16)], axis=0))
        ya = (ya_p, jnp.concatenate([ya_s, ya_m[0], jnp.zeros((tail_pad, D_ATTN), BF16)], axis=0))

        wo = jnp.concatenate([w_out[l][:D_SSD], w_out[l][D_SSD:][head_perm]], axis=0).astype(BF16)
        nfw = norm_ffn_w[l].reshape(1, D_MODEL).astype(F32)
        i = l // 2
        if l % 2 == 0:
            x = _out_ffn(x, ys, ya, wo, nfw, w_gate[i].astype(BF16), w_up[i].astype(BF16),
                         w_down[i].astype(BF16), n_main)
        else:
            wr = jnp.pad(w_router[i].astype(F32), ((0, 0), (0, LANES - N_EXPERTS)))
            wr_hi = wr.astype(BF16)
            wr_lo = (wr - wr_hi.astype(F32)).astype(BF16)
            xm, route, counts = _out_router(x, ys, ya, wo, nfw, wr_hi, wr_lo, n_main, n_tok)
            x = _moe_layer(xm, route, counts, n_main, n_tok, nfw, moe_w_gate[i].astype(BF16),
                           moe_w_up[i].astype(BF16), moe_w_down[i].astype(BF16))

        samp = proj_tail[:r_samp]
        xbc_s = samp[:, COL_XBC:COL_XBC + CONV_DIM].reshape(nseq, t_s, CONV_DIM)
        v_s = samp[:, COL_V:COL_V + KV_DIM].reshape(nseq, t_s, KV_DIM)
        v_meta = proj_tail[r_samp:r_samp + N_META, COL_V:COL_V + KV_DIM]
        meta_shape = (bsz, N_META, N_KV_HEADS, HEAD_DIM)
        outs['p_ssm'].append(ssm_p)
        outs['p_conv'].append(conv_p[:, SUBLANES - (CONV_W - 1):])
        outs['p_mk'].append(jnp.broadcast_to(heads4(kp_m[0:1]), meta_shape))
        outs['p_mv'].append(jnp.broadcast_to(heads4(v_meta[None]), meta_shape))
        outs['p_wk'].append(heads4(k_last))
        outs['p_wv'].append(heads4(v_last))
        outs['s_ssm'].append(ssm_s)
        outs['s_conv'].append(jnp.concatenate([state_conv[l].astype(F32), xbc_s], axis=1)[:, t_s:])
        outs['s_wk'].append(jnp.concatenate([cache_win_k[l].astype(F32),
                                             heads4(kp_s.reshape(nseq, t_s, KV_DIM))], axis=1)[:, t_s:])
        outs['s_wv'].append(jnp.concatenate([cache_win_v[l].astype(F32), heads4(v_s)], axis=1)[:, t_s:])

    y_prompt = x[0][:r_main].reshape(bsz, seq, D_MODEL)
    y_sample = x[1][:r_samp].reshape(nseq, t_s, D_MODEL)
    st = lambda k: jnp.stack(outs[k])
    return (y_prompt, y_sample, st('p_ssm'), st('p_conv'), st('p_mk'), st('p_mv'), st('p_wk'), st('p_wv'),
            st('s_ssm'), st('s_conv'), st('s_wk'), st('s_wv'))
```

```python
import functools

import jax
import jax.numpy as jnp
from jax import lax
from jax.experimental import pallas as pl
from jax.experimental.pallas import tpu as pltpu

F32 = jnp.float32
BF16 = jnp.bfloat16

D_MODEL = 1024
D_SSD = 512
SSD_HEAD_DIM = 64
N_SSD_HEADS = 8
SSD_HEADS_PER_GROUP = 4
N_SSD_GROUPS = 2
D_STATE = 128
CONV_W = 4
CONV_DIM = 1024
D_ATTN = 512
HEAD_DIM = 64
N_Q_HEADS = 8
N_KV_HEADS = 2
Q_PER_KV = 4
KV_DIM = 128
WINDOW = 128
N_META = 16
D_FF = 2816
N_EXPERTS = 8
TOP_K = 2
EPS = 1e-6
NEG = -1e30
ATTN_SCALE = HEAD_DIM ** -0.5
PAST_LEN = 16384
ROPE_THETA = 10000.0

LANES = 128
SUBLANES = 8
CHUNK = 128
ROW_TILE = 512
FF_CHUNK = 256
DMA_UNROLL = 8
VMEM_LIMIT = 60 * 1024 * 1024

COL_XBC = 0
COL_Z = 1024
COL_Q = 1536
COL_K = 2048
COL_V = 2176
COL_DT = 2304
PROJ_W = 2432
PROJ_STEP = 512


def _dot(a, b):
    return jnp.dot(a, b, preferred_element_type=F32)


def _dot_nt(a, b):
    return lax.dot_general(a, b, (((1,), (1,)), ((), ())), preferred_element_type=F32)


def _dot_tn(a, b):
    return lax.dot_general(a, b, (((0,), (0,)), ((), ())), preferred_element_type=F32)


def _rms(x, w):
    return x * lax.rsqrt(jnp.mean(x * x, axis=-1, keepdims=True) + EPS) * w


def _silu(x):
    return x * jax.nn.sigmoid(x)


def _split3(x):
    p1 = x.astype(BF16)
    r1 = x - p1.astype(F32)
    p2 = r1.astype(BF16)
    p3 = (r1 - p2.astype(F32)).astype(BF16)
    return p1, p2, p3


def _dot_exact(sel, x):
    p1, p2, p3 = _split3(x)
    return _dot(sel, p1) + _dot(sel, p2) + _dot(sel, p3)


def _cparams(ndim, **kw):
    return pltpu.CompilerParams(dimension_semantics=("arbitrary",) * ndim, vmem_limit_bytes=VMEM_LIMIT, **kw)


def _src_specs(width, n_main):
    return [pl.BlockSpec((ROW_TILE, width), lambda i: (jnp.minimum(i, n_main - 1), 0)),
            pl.BlockSpec((ROW_TILE, width), lambda i: (jnp.maximum(i - n_main, 0), 0))]


def _pick(n_main, main_ref, tail_ref):
    dtype = main_ref.dtype
    picked = jnp.where(pl.program_id(0) < n_main, main_ref[...].astype(F32), tail_ref[...].astype(F32))
    return picked.astype(dtype)


def _dst_specs(width, n_main):
    return [pl.BlockSpec((ROW_TILE, width), lambda i: (jnp.minimum(i, n_main), 0)),
            pl.BlockSpec((ROW_TILE, width), lambda i: (jnp.maximum(i - n_main, 0), 0))]


def _dst_shapes(width, n_main, n_tail, dtype):
    return [jax.ShapeDtypeStruct(((n_main + 1) * ROW_TILE, width), dtype),
            jax.ShapeDtypeStruct((n_tail * ROW_TILE, width), dtype)]


def _resident(shape):
    nd = len(shape)
    return pl.BlockSpec(shape, lambda *a: (0,) * nd, pipeline_mode=pl.Buffered(1))


def _param_specs(params):
    return [pl.BlockSpec(p.shape, lambda *a: (0, 0)) for p in params]


def _in_proj_kernel(n_main, xa_ref, xb_ref, nw_ref, w_ref, om_ref, ot_ref):
    xn = _rms(_pick(n_main, xa_ref, xb_ref), nw_ref[...]).astype(BF16)
    for c0 in range(0, PROJ_W, PROJ_STEP):
        cols = slice(c0, min(c0 + PROJ_STEP, PROJ_W))
        r = _dot(xn, w_ref[:, cols])
        om_ref[:, cols] = r
        ot_ref[:, cols] = r


def _in_proj(x_main, x_tail, nw, w, n_main):
    n_tail = x_tail.shape[0] // ROW_TILE
    return pl.pallas_call(
        functools.partial(_in_proj_kernel, n_main),
        grid=(n_main + n_tail,),
        in_specs=_src_specs(D_MODEL, n_main) + [
            pl.BlockSpec((1, D_MODEL), lambda i: (0, 0)),
            pl.BlockSpec((D_MODEL, PROJ_W), lambda i: (0, 0)),
        ],
        out_specs=_dst_specs(PROJ_W, n_main),
        out_shape=_dst_shapes(PROJ_W, n_main, n_tail, F32),
        compiler_params=_cparams(1),
        name="in_proj",
    )(x_main, x_tail, nw, w)


def _ssd_chunk(xbc, z, dt_raw, valid, cw_ref, cb_ref, dtb_ref, an_ref, dsk_ref, nw_ref, cbuf, hst):
    q = CHUNK
    ext = jnp.concatenate([cbuf[...], xbc], axis=0)
    cbuf[...] = xbc[q - SUBLANES:q, :]
    acc = cb_ref[...] + xbc * cw_ref[CONV_W - 1:CONV_W, :]
    for s in range(1, CONV_W):
        acc = acc + pltpu.roll(ext, s, 0)[SUBLANES:, :] * cw_ref[CONV_W - 1 - s:CONV_W - s, :]
    xc = _silu(acc)
    xs = xc[:, :D_SSD]
    bm = xc[:, D_SSD:D_SSD + N_SSD_GROUPS * D_STATE]
    cm = xc[:, D_SSD + N_SSD_GROUPS * D_STATE:]

    dt = jax.nn.softplus(dt_raw + dtb_ref[...])
    if valid is not None:
        dt = jnp.where(valid, dt, 0.0)
    da = dt * an_ref[...]
    row_i = lax.broadcasted_iota(jnp.int32, (q, q), 0)
    col_j = lax.broadcasted_iota(jnp.int32, (q, q), 1)
    tril = row_i >= col_j
    cs = _dot_exact(jnp.where(tril, 1.0, 0.0).astype(BF16), da)
    cs_t = cs.T
    last = cs[q - 1:q, :]
    ecl = jnp.exp(last)
    xdt = xs * _expand_heads(dt)
    xw = xdt * _expand_heads(jnp.exp(last - cs))
    ecs_x = _expand_heads(jnp.exp(cs))
    skip = xs * dsk_ref[...]

    gw = SSD_HEADS_PER_GROUP * SSD_HEAD_DIM
    head_of_col = jnp.right_shift(lax.broadcasted_iota(jnp.int32, (q, gw), 1), SSD_HEAD_DIM.bit_length() - 1)
    ys = []
    for g in range(N_SSD_GROUPS):
        bg = bm[:, g * D_STATE:(g + 1) * D_STATE].astype(BF16)
        cg = cm[:, g * D_STATE:(g + 1) * D_STATE].astype(BF16)
        cb = _dot_nt(cg, bg)
        gc = slice(g * gw, (g + 1) * gw)
        h_prev = hst[gc, :]
        y_g = _dot_nt(cg, h_prev.astype(BF16)) * ecs_x[:, gc] + skip[:, gc]
        xdt_g = xdt[:, gc]
        for r in range(SSD_HEADS_PER_GROUP):
            h = g * SSD_HEADS_PER_GROUP + r
            seg = cs[:, h:h + 1] - cs_t[h:h + 1, :]
            decay = jnp.where(tril, jnp.exp(jnp.where(tril, seg, 0.0)), 0.0)
            y_g = y_g + _dot((decay * cb).astype(BF16), jnp.where(head_of_col == r, xdt_g, 0.0).astype(BF16))
        ys.append(y_g)
        ecl_col = jnp.concatenate(
            [jnp.broadcast_to(ecl[:, h:h + 1], (SSD_HEAD_DIM, 1))
             for h in range(g * SSD_HEADS_PER_GROUP, (g + 1) * SSD_HEADS_PER_GROUP)], axis=0)
        hst[gc, :] = h_prev * ecl_col + _dot_tn(xw[:, gc].astype(BF16), bg)

    return _ssd_gate_norm(jnp.concatenate(ys, axis=1), z, nw_ref)


def _ssd_gate_norm(y, z, nw_ref):
    y = y * _silu(z)
    gs = D_SSD // N_SSD_GROUPS
    return jnp.concatenate([_rms(y[:, g * gs:(g + 1) * gs], nw_ref[:, g * gs:(g + 1) * gs])
                            for g in range(N_SSD_GROUPS)], axis=-1)


def _ssd_prompt_kernel(xbc_m, z_m, dt_m, xbc_t, z_t, dt_t, cw, cb, dtb, an, dsk, nw,
                       y_o, ym_o, hf_o, ct_o, cbuf, hst, ymbuf):
    c = pl.program_id(1)
    pad = CHUNK - N_META
    prm = (cw, cb, dtb, an, dsk, nw)

    @pl.when(c == 0)
    def _():
        cbuf[...] = jnp.zeros((SUBLANES, CONV_DIM), F32)
        hst[...] = jnp.zeros(hst.shape, F32)

        def stage(meta_ref):
            return jnp.concatenate([jnp.zeros((pad, meta_ref.shape[1]), F32), meta_ref[...]], axis=0)

        row = lax.broadcasted_iota(jnp.int32, (CHUNK, 1), 0)
        y = _ssd_chunk(stage(xbc_t), stage(z_t), stage(dt_t), row >= pad, *prm, cbuf, hst)
        ymbuf[...] = y[pad:, :]

    @pl.when(c > 0)
    def _():
        for j in range(xbc_m.shape[0] // CHUNK):
            rows = slice(j * CHUNK, (j + 1) * CHUNK)
            y = _ssd_chunk(xbc_m[rows, :], z_m[rows, :], dt_m[rows, :], None, *prm, cbuf, hst)
            y_o[rows, :] = y.astype(BF16)

    ym_o[...] = ymbuf[...].astype(BF16)
    hf_o[...] = hst[...].reshape(hf_o.shape)
    ct_o[...] = cbuf[...]


def _chunks_per_step(nc):
    return 4 if nc % 4 == 0 else (2 if nc % 2 == 0 else 1)


def _ssd_prompt(proj_main, proj_tail, params, bsz, seq, meta_row):
    cps = _chunks_per_step(seq // CHUNK)
    blk = cps * CHUNK
    nb = seq // blk
    mb = meta_row // N_META

    def main(col):
        return lambda b, c: (b * nb + jnp.maximum(c - 1, 0), col)

    def meta(col):
        return lambda b, c: (mb, col)

    in_specs = [
        pl.BlockSpec((blk, CONV_DIM), main(COL_XBC // CONV_DIM)),
        pl.BlockSpec((blk, D_SSD), main(COL_Z // D_SSD)),
        pl.BlockSpec((blk, LANES), main(COL_DT // LANES)),
        pl.BlockSpec((N_META, CONV_DIM), meta(COL_XBC // CONV_DIM)),
        pl.BlockSpec((N_META, D_SSD), meta(COL_Z // D_SSD)),
        pl.BlockSpec((N_META, LANES), meta(COL_DT // LANES)),
    ] + _param_specs(params)
    return pl.pallas_call(
        _ssd_prompt_kernel,
        grid=(bsz, nb + 1),
        in_specs=in_specs,
        out_specs=[
            pl.BlockSpec((blk, D_SSD), main(0)),
            pl.BlockSpec((None, N_META, D_SSD), lambda b, c: (b, 0, 0)),
            pl.BlockSpec((None, N_SSD_HEADS, SSD_HEAD_DIM, D_STATE), lambda b, c: (b, 0, 0, 0)),
            pl.BlockSpec((None, SUBLANES, CONV_DIM), lambda b, c: (b, 0, 0)),
        ],
        out_shape=[
            jax.ShapeDtypeStruct((bsz * seq, D_SSD), BF16),
            jax.ShapeDtypeStruct((bsz, N_META, D_SSD), BF16),
            jax.ShapeDtypeStruct((bsz, N_SSD_HEADS, SSD_HEAD_DIM, D_STATE), F32),
            jax.ShapeDtypeStruct((bsz, SUBLANES, CONV_DIM), F32),
        ],
        scratch_shapes=[pltpu.VMEM((SUBLANES, CONV_DIM), F32),
                        pltpu.VMEM((N_SSD_HEADS * SSD_HEAD_DIM, D_STATE), F32),
                        pltpu.VMEM((N_META, D_SSD), F32)],
        compiler_params=_cparams(2),
        name="ssd_prompt",
    )(proj_main, proj_main, proj_main, proj_tail, proj_tail, proj_tail, *params)


def _expand_heads(a):
    hh = lax.broadcasted_iota(jnp.int32, (LANES, D_SSD), 0)
    cc = lax.broadcasted_iota(jnp.int32, (LANES, D_SSD), 1)
    sel = jnp.where(jnp.right_shift(cc, SSD_HEAD_DIM.bit_length() - 1) == hh, 1.0, 0.0).astype(BF16)
    p1, p2, p3 = _split3(a)
    return _dot(p1, sel) + _dot(p2, sel) + _dot(p3, sel)


def _pad_rows_bf16(x, rows):
    return jnp.concatenate([x, jnp.zeros((rows - x.shape[0], x.shape[1]), F32)], axis=0).astype(BF16)


def _ssd_sample_kernel(xbc_s, z_s, dt_s, cprev, h0, cw, cb, dtb, an, dsk, nw,
                       y_o, hf_o, ubuf, ybuf):
    nsq = cprev.shape[0]
    t = SUBLANES
    rows = nsq * t
    x = xbc_s[...]
    for s in range(nsq):
        ubuf[2 * t * s:2 * t * s + t, :] = cprev[s]
        ubuf[2 * t * s + t:2 * t * (s + 1), :] = x[s * t:(s + 1) * t, :]
    parts = []
    for s in range(nsq):
        acc = cb[...]
        for k in range(CONV_W):
            off = 2 * t * s + t - (CONV_W - 1) + k
            acc = acc + ubuf[off:off + t, :] * cw[k:k + 1, :]
        parts.append(acc)
    xc = _silu(jnp.concatenate(parts, axis=0))
    xs = xc[:, :D_SSD]
    bm = xc[:, D_SSD:D_SSD + N_SSD_GROUPS * D_STATE]
    cm = xc[:, D_SSD + N_SSD_GROUPS * D_STATE:]

    dt = jax.nn.softplus(dt_s[...] + dtb[...])
    da = dt * an[...]
    ri = lax.broadcasted_iota(jnp.int32, (rows, rows), 0)
    cj = lax.broadcasted_iota(jnp.int32, (rows, rows), 1)
    mask = jnp.logical_and(ri >= cj, jnp.right_shift(ri, 3) == jnp.right_shift(cj, 3))
    cs = _dot_exact(jnp.where(mask, 1.0, 0.0).astype(BF16), da)
    last = _dot_exact(jnp.where(cj == jnp.bitwise_or(ri, t - 1), 1.0, 0.0).astype(BF16), cs)
    cs_t = cs.T
    ecl = jnp.exp(last)
    xdt = xs * _expand_heads(dt)
    xw = xdt * _expand_heads(jnp.exp(last - cs))
    ecs_x = _expand_heads(jnp.exp(cs))
    skip = xs * dsk[...]

    gw = SSD_HEADS_PER_GROUP * SSD_HEAD_DIM
    for g in range(N_SSD_GROUPS):
        bg_f = bm[:, g * D_STATE:(g + 1) * D_STATE]
        cg_f = cm[:, g * D_STATE:(g + 1) * D_STATE]
        cbm = _dot_nt(cg_f.astype(BF16), bg_f.astype(BF16))
        for r in range(SSD_HEADS_PER_GROUP):
            h = g * SSD_HEADS_PER_GROUP + r
            hc = slice(h * SSD_HEAD_DIM, (h + 1) * SSD_HEAD_DIM)
            seg = cs[:, h:h + 1] - cs_t[h:h + 1, :]
            decay = jnp.where(mask, jnp.exp(jnp.where(mask, seg, 0.0)), 0.0)
            ybuf[:, hc] = _dot((decay * cbm).astype(BF16), xdt[:, hc].astype(BF16)) + skip[:, hc]
        gc = slice(g * gw, (g + 1) * gw)
        heads = slice(g * SSD_HEADS_PER_GROUP, (g + 1) * SSD_HEADS_PER_GROUP)
        for s in range(nsq):
            rs = slice(s * t, (s + 1) * t)
            hg = h0[s, heads].reshape(gw, D_STATE)
            y_off = _dot_nt(_pad_rows_bf16(cg_f[rs, :], 2 * t), hg.astype(BF16))[0:t, :]
            ybuf[rs, gc] = ybuf[rs, gc] + y_off * ecs_x[rs, gc]
            ecl_col = jnp.concatenate(
                [jnp.broadcast_to(ecl[s * t:s * t + 1, h:h + 1], (SSD_HEAD_DIM, 1))
                 for h in range(heads.start, heads.stop)], axis=0)
            h_new = hg * ecl_col + _dot_tn(_pad_rows_bf16(xw[rs, gc], 2 * t), _pad_rows_bf16(bg_f[rs, :], 2 * t))
            hf_o[s, heads] = h_new.reshape(SSD_HEADS_PER_GROUP, SSD_HEAD_DIM, D_STATE)

    y_o[...] = _ssd_gate_norm(ybuf[...], z_s[...], nw).astype(BF16)


def _ssd_sample(proj_tail, conv_prev, h0, params, nseq, t, nsq):
    blk = nsq * t
    in_specs = [
        pl.BlockSpec((blk, CONV_DIM), lambda b: (b, COL_XBC // CONV_DIM)),
        pl.BlockSpec((blk, D_SSD), lambda b: (b, COL_Z // D_SSD)),
        pl.BlockSpec((blk, LANES), lambda b: (b, COL_DT // LANES)),
        pl.BlockSpec((nsq, SUBLANES, CONV_DIM), lambda b: (b, 0, 0)),
        pl.BlockSpec((nsq, N_SSD_HEADS, SSD_HEAD_DIM, D_STATE), lambda b: (b, 0, 0, 0)),
    ] + _param_specs(params)
    return pl.pallas_call(
        _ssd_sample_kernel,
        grid=(nseq // nsq,),
        in_specs=in_specs,
        out_specs=[
            pl.BlockSpec((blk, D_SSD), lambda b: (b, 0)),
            pl.BlockSpec((nsq, N_SSD_HEADS, SSD_HEAD_DIM, D_STATE), lambda b: (b, 0, 0, 0)),
        ],
        out_shape=[
            jax.ShapeDtypeStruct((nseq * t, D_SSD), BF16),
            jax.ShapeDtypeStruct((nseq, N_SSD_HEADS, SSD_HEAD_DIM, D_STATE), F32),
        ],
        scratch_shapes=[pltpu.VMEM((2 * blk, CONV_DIM), F32), pltpu.VMEM((blk, D_SSD), F32)],
        compiler_params=_cparams(1),
        name="ssd_sample",
    )(proj_tail, proj_tail, proj_tail, conv_prev, h0, *params)


def _qk_prep(x, w, cos, sin):
    lane = lax.broadcasted_iota(jnp.int32, x.shape, 1)
    lo_head = lane < HEAD_DIM
    sq = x * x
    s_lo = jnp.sum(jnp.where(lo_head, sq, 0.0), axis=-1, keepdims=True)
    s_all = jnp.sum(sq, axis=-1, keepdims=True)
    ms = jnp.where(lo_head, s_lo, s_all - s_lo) * (1.0 / HEAD_DIM)
    xn = x * lax.rsqrt(ms + EPS) * w
    half = HEAD_DIM // 2
    first_half = (lane % HEAD_DIM) < half
    partner = jnp.where(first_half, pltpu.roll(xn, LANES - half, 1), pltpu.roll(xn, half, 1))
    return xn * cos + partner * sin


def _q_groups(q, qnw_ref, cos, sin):
    return [_qk_prep(q[:, g * LANES:(g + 1) * LANES], qnw_ref[:, g * LANES:(g + 1) * LANES], cos, sin)
            * ATTN_SCALE for g in range(D_ATTN // LANES)]


def _sink_column(snk_ref, rows):
    return jnp.concatenate([jnp.broadcast_to(snk_ref[:, h:h + 1], (rows, 1)) for h in range(N_Q_HEADS)],
                           axis=0)


def _softmax_weights(pieces, sk):
    top = pieces[0]
    for p in pieces[1:]:
        top = jnp.maximum(top, p)
    m = jnp.maximum(jnp.max(top, axis=-1, keepdims=True), sk)
    e = [jnp.exp(p - m) for p in pieces]
    tot = e[0]
    for p in e[1:]:
        tot = tot + p
    return e, 1.0 / (jnp.sum(tot, axis=-1, keepdims=True) + jnp.exp(sk - m))


def _kv_prep_t(k, v, knw_ref, cos, sin):
    kp = _qk_prep(k, knw_ref[...], cos, sin)
    lane = lax.broadcasted_iota(jnp.int32, kp.shape, 1)
    return (kp, jnp.where(lane < HEAD_DIM, kp, 0.0).astype(BF16),
            jnp.where(lane < HEAD_DIM, 0.0, kp).astype(BF16), v.T.astype(BF16))


def _scores_t(qs, key_lo, key_hi):
    return jnp.concatenate([_dot_nt(key_lo, qs), _dot_nt(key_hi, qs)], axis=1)


def _softmax_weights_t(pieces, sk):
    m = sk
    for p in pieces:
        m = jnp.maximum(m, jnp.max(p, axis=0, keepdims=True))
    e = [jnp.exp(p - m) for p in pieces]
    den = jnp.exp(sk - m)
    for p in e:
        den = den + jnp.sum(p, axis=0, keepdims=True)
    return e, 1.0 / den


def _meta_weights(e_m):
    return jnp.concatenate([e_m.astype(BF16), jnp.zeros((CHUNK - N_META, e_m.shape[1]), BF16)], axis=0)


def _attn_out_t(o_t, anw_ref):
    nst = Q_PER_KV * CHUNK
    d = lax.broadcasted_iota(jnp.int32, (LANES, CHUNK), 0)
    groups = [jnp.where(d < HEAD_DIM, o_t[:, g * CHUNK:(g + 1) * CHUNK],
                        o_t[:, nst + g * CHUNK:nst + (g + 1) * CHUNK]).T for g in range(Q_PER_KV)]
    return _rms(jnp.concatenate(groups, axis=1), anw_ref[...])


def _attn_prompt_kernel(q_m, k_m, v_m, q_t, k_t, v_t, cos_m, sin_m, cos_t, sin_t, qnw, knw, snk, anw,
                        o_o, om_o, kpm_o, kl_o, vl_o,
                        km_lo, km_hi, vmt, kp_lo, kp_hi, vpt, kmf, ombuf):
    c = pl.program_id(1)
    pad = CHUNK - N_META
    cols = N_Q_HEADS * CHUNK
    sk = jnp.concatenate([jnp.broadcast_to(snk[:, h:h + 1], (1, CHUNK)) for h in range(N_Q_HEADS)], axis=1)

    @pl.when(c == 0)
    def _():
        def stage(meta_ref):
            return jnp.concatenate([jnp.zeros((pad, meta_ref.shape[1]), F32), meta_ref[...]], axis=0)

        cs, sn = cos_t[...], sin_t[...]
        kp = _qk_prep(stage(k_t), knw[...], cs, sn)
        lane = lax.broadcasted_iota(jnp.int32, (N_META, LANES), 1)
        km_lo[...] = jnp.where(lane < HEAD_DIM, kp[pad:, :], 0.0).astype(BF16)
        km_hi[...] = jnp.where(lane < HEAD_DIM, 0.0, kp[pad:, :]).astype(BF16)
        kmf[...] = kp[pad:, :]
        vmt[...] = jnp.concatenate([v_t[...], jnp.zeros((pad, LANES), F32)], axis=0).T.astype(BF16)
        kp_lo[...] = jnp.zeros(kp_lo.shape, BF16)
        kp_hi[...] = jnp.zeros(kp_hi.shape, BF16)
        vpt[...] = jnp.zeros(vpt.shape, BF16)
        qs = jnp.concatenate(_q_groups(stage(q_t), qnw, cs, sn), axis=0).astype(BF16)
        r = lax.broadcasted_iota(jnp.int32, (N_META, cols), 0)
        qi = lax.broadcasted_iota(jnp.int32, (N_META, cols), 1) & (CHUNK - 1)
        s_m = jnp.where(r <= qi - pad, _scores_t(qs, km_lo[...], km_hi[...]), NEG)
        (e_m,), inv = _softmax_weights_t([s_m], sk)
        ombuf[...] = _attn_out_t(_dot(vmt[...], _meta_weights(e_m)) * inv, anw)[pad:, :]

    @pl.when(c > 0)
    def _():
        nblk = q_m.shape[0] // CHUNK
        prev = (kp_lo[...], kp_hi[...], vpt[...])
        r = lax.broadcasted_iota(jnp.int32, (CHUNK, cols), 0)
        qi = lax.broadcasted_iota(jnp.int32, (CHUNK, cols), 1) & (CHUNK - 1)
        tri = r <= qi
        for j in range(nblk):
            rows = slice(j * CHUNK, (j + 1) * CHUNK)
            cs, sn = cos_m[rows, :], sin_m[rows, :]
            v = v_m[rows, :]
            kp, k_lo, k_hi, v_tb = _kv_prep_t(k_m[rows, :], v, knw, cs, sn)
            qs = jnp.concatenate(_q_groups(q_m[rows, :], qnw, cs, sn), axis=0).astype(BF16)
            band = jnp.where(tri, _scores_t(qs, k_lo, k_hi), _scores_t(qs, prev[0], prev[1]))
            if j == 0:
                band = jnp.where(jnp.logical_or(tri, c > 1), band, NEG)
            (e_b, e_m), inv = _softmax_weights_t([band, _scores_t(qs, km_lo[...], km_hi[...])], sk)
            o_t = (_dot(v_tb, jnp.where(tri, e_b, 0.0).astype(BF16))
                   + _dot(prev[2], jnp.where(tri, 0.0, e_b).astype(BF16))
                   + _dot(vmt[...], _meta_weights(e_m))) * inv
            o_o[rows, :] = _attn_out_t(o_t, anw).astype(BF16)
            prev = (k_lo, k_hi, v_tb)
            if j == nblk - 1:
                kl_o[...] = kp
                vl_o[...] = v
        kp_lo[...], kp_hi[...], vpt[...] = prev

    om_o[...] = ombuf[...].astype(BF16)
    kpm_o[...] = kmf[...]


def _attn_sample_kernel(q_s, k_s, v_s, mk, mv, wk, wv, cos, sin, qnw, knw, snk, anw,
                        o_o, kp_o, obuf):
    nsq = mk.shape[0]
    t = SUBLANES
    nst = Q_PER_KV * t
    cs, sn = cos[...], sin[...]
    kp = _qk_prep(k_s[...], knw[...], cs, sn)
    kp_o[...] = kp
    v = v_s[...]
    qg = _q_groups(q_s[...], qnw, cs, sn)

    lo = lax.broadcasted_iota(jnp.int32, (nst, LANES), 1) < HEAD_DIM
    i_q = lax.broadcasted_iota(jnp.int32, (2 * nst, LANES), 0) & (t - 1)
    cj = lax.broadcasted_iota(jnp.int32, (2 * nst, LANES), 1)
    mask_a = cj > i_q
    mask_b = jnp.logical_or(cj <= i_q, jnp.logical_and(cj >= t, cj < t + N_META))
    sk = _sink_column(snk, t)
    zpad = jnp.zeros((WINDOW - t - N_META, LANES), F32)
    for s in range(nsq):
        rows = slice(s * t, (s + 1) * t)
        q_st = jnp.concatenate([g[rows, :] for g in qg], axis=0)
        q2 = jnp.concatenate([jnp.where(lo, q_st, 0.0), jnp.where(lo, 0.0, q_st)], axis=0).astype(BF16)
        k_b = jnp.concatenate([kp[rows, :], mk[s], zpad], axis=0).astype(BF16)
        v_b = jnp.concatenate([v[rows, :], mv[s], zpad], axis=0).astype(BF16)
        (e_a, e_b), inv = _softmax_weights(
            [jnp.where(mask_a, _dot_nt(q2, wk[s].astype(BF16)), NEG),
             jnp.where(mask_b, _dot_nt(q2, k_b), NEG)], sk)
        o2 = (_dot(e_a.astype(BF16), wv[s].astype(BF16)) + _dot(e_b.astype(BF16), v_b)) * inv
        o_st = jnp.where(lo, o2[0:nst, :], o2[nst:2 * nst, :])
        for g in range(Q_PER_KV):
            obuf[rows, g * LANES:(g + 1) * LANES] = o_st[g * t:(g + 1) * t, :]
    o_o[...] = _rms(obuf[...], anw[...]).astype(BF16)


def _attn_prompt(proj_main, proj_tail, tabs_main, tabs_meta, params, bsz, seq, meta_row):
    blk = _chunks_per_step(seq // CHUNK) * CHUNK
    nb = seq // blk
    mb = meta_row // N_META

    def main(col):
        return lambda b, c: (b * nb + jnp.maximum(c - 1, 0), col)

    def meta(col):
        return lambda b, c: (mb, col)

    per_batch = lambda b, c: (b, 0, 0)
    in_specs = [
        pl.BlockSpec((blk, D_ATTN), main(COL_Q // D_ATTN)),
        pl.BlockSpec((blk, KV_DIM), main(COL_K // KV_DIM)),
        pl.BlockSpec((blk, KV_DIM), main(COL_V // KV_DIM)),
        pl.BlockSpec((N_META, D_ATTN), meta(COL_Q // D_ATTN)),
        pl.BlockSpec((N_META, KV_DIM), meta(COL_K // KV_DIM)),
        pl.BlockSpec((N_META, KV_DIM), meta(COL_V // KV_DIM)),
        pl.BlockSpec((blk, LANES), lambda b, c: (jnp.maximum(c - 1, 0), 0)),
        pl.BlockSpec((blk, LANES), lambda b, c: (jnp.maximum(c - 1, 0), 0)),
        pl.BlockSpec((CHUNK, LANES), lambda b, c: (0, 0)),
        pl.BlockSpec((CHUNK, LANES), lambda b, c: (0, 0)),
    ] + _param_specs(params)
    return pl.pallas_call(
        _attn_prompt_kernel,
        grid=(bsz, nb + 1),
        in_specs=in_specs,
        out_specs=[
            pl.BlockSpec((blk, D_ATTN), main(0)),
            pl.BlockSpec((None, N_META, D_ATTN), per_batch),
            pl.BlockSpec((None, N_META, KV_DIM), per_batch),
            pl.BlockSpec((None, CHUNK, KV_DIM), per_batch),
            pl.BlockSpec((None, CHUNK, KV_DIM), per_batch),
        ],
        out_shape=[
            jax.ShapeDtypeStruct((bsz * seq, D_ATTN), BF16),
            jax.ShapeDtypeStruct((bsz, N_META, D_ATTN), BF16),
            jax.ShapeDtypeStruct((bsz, N_META, KV_DIM), F32),
            jax.ShapeDtypeStruct((bsz, CHUNK, KV_DIM), F32),
            jax.ShapeDtypeStruct((bsz, CHUNK, KV_DIM), F32),
        ],
        scratch_shapes=[pltpu.VMEM((N_META, KV_DIM), BF16) for _ in range(2)] + [
            pltpu.VMEM((CHUNK, KV_DIM), BF16) for _ in range(4)] + [
            pltpu.VMEM((N_META, KV_DIM), F32), pltpu.VMEM((N_META, D_ATTN), F32),
        ],
        compiler_params=_cparams(2),
        name="attn_prompt",
    )(proj_main, proj_main, proj_main, proj_tail, proj_tail, proj_tail, *tabs_main, *tabs_meta, *params)


def _attn_sample(proj_tail, mk, mv, wk, wv, cos, sin, params, nseq, t, nsq):
    blk = nsq * t
    per_seq = lambda b: (b, 0, 0)
    in_specs = [
        pl.BlockSpec((blk, D_ATTN), lambda b: (b, COL_Q // D_ATTN)),
        pl.BlockSpec((blk, KV_DIM), lambda b: (b, COL_K // KV_DIM)),
        pl.BlockSpec((blk, KV_DIM), lambda b: (b, COL_V // KV_DIM)),
        pl.BlockSpec((nsq, N_META, KV_DIM), per_seq),
        pl.BlockSpec((nsq, N_META, KV_DIM), per_seq),
        pl.BlockSpec((nsq, WINDOW, KV_DIM), per_seq),
        pl.BlockSpec((nsq, WINDOW, KV_DIM), per_seq),
        pl.BlockSpec((blk, LANES), lambda b: (0, 0)),
        pl.BlockSpec((blk, LANES), lambda b: (0, 0)),
    ] + _param_specs(params)
    return pl.pallas_call(
        _attn_sample_kernel,
        grid=(nseq // nsq,),
        in_specs=in_specs,
        out_specs=[
            pl.BlockSpec((blk, D_ATTN), lambda b: (b, 0)),
            pl.BlockSpec((blk, KV_DIM), lambda b: (b, 0)),
        ],
        out_shape=[
            jax.ShapeDtypeStruct((nseq * t, D_ATTN), BF16),
            jax.ShapeDtypeStruct((nseq * t, KV_DIM), F32),
        ],
        scratch_shapes=[pltpu.VMEM((blk, D_ATTN), F32)],
        compiler_params=_cparams(1),
        name="attn_sample",
    )(proj_tail, proj_tail, proj_tail, mk, mv, wk, wv, cos, sin, *params)


def _mix_out(n_main, x_refs, ys_refs, ya_refs, wo_ref):
    return (_pick(n_main, *x_refs) + _dot(_pick(n_main, *ys_refs), wo_ref[0:D_SSD, :])
            + _dot(_pick(n_main, *ya_refs), wo_ref[D_SSD:D_SSD + D_ATTN, :]))


def _swiglu_acc(hn, wg_ref, wu_ref, wd_ref, acc_ref):
    for j in range(D_FF // FF_CHUNK):
        cols = slice(j * FF_CHUNK, (j + 1) * FF_CHUNK)
        a = (_silu(_dot(hn, wg_ref[:, cols])) * _dot(hn, wu_ref[:, cols])).astype(BF16)
        acc_ref[...] += _dot(a, wd_ref[cols, :])


def _out_ffn_kernel(n_main, xa, xb, ysa, ysb, yaa, yab, wo_ref, nw_ref, wg_ref, wu_ref, wd_ref,
                    om_ref, ot_ref):
    xm = _mix_out(n_main, (xa, xb), (ysa, ysb), (yaa, yab), wo_ref)
    om_ref[...] = xm
    _swiglu_acc(_rms(xm, nw_ref[...]).astype(BF16), wg_ref, wu_ref, wd_ref, om_ref)
    ot_ref[...] = om_ref[...]


def _out_ffn(x, ys, ya, wo, nw, wg, wu, wd, n_main):
    n_tail = x[1].shape[0] // ROW_TILE
    return pl.pallas_call(
        functools.partial(_out_ffn_kernel, n_main),
        grid=(n_main + n_tail,),
        in_specs=_src_specs(D_MODEL, n_main) + _src_specs(D_SSD, n_main) + _src_specs(D_ATTN, n_main) + [
            _resident(wo.shape), _resident(nw.shape),
            _resident(wg.shape), _resident(wu.shape), _resident(wd.shape),
        ],
        out_specs=_dst_specs(D_MODEL, n_main),
        out_shape=_dst_shapes(D_MODEL, n_main, n_tail, F32),
        compiler_params=_cparams(1),
        name="out_ffn",
    )(*x, *ys, *ya, wo, nw, wg, wu, wd)


def _out_router_kernel(n_main, n_tok, xa, xb, ysa, ysb, yaa, yab, wo_ref, nw_ref, wr_hi_ref, wr_lo_ref,
                       before_ref, xm_o, rt_o, cnt_o):
    xm = _mix_out(n_main, (xa, xb), (ysa, ysb), (yaa, yab), wo_ref)
    xm_o[...] = xm
    hn = _rms(xm, nw_ref[...])
    hi = hn.astype(BF16)
    lo = (hn - hi.astype(F32)).astype(BF16)
    logits = _dot(hi, wr_hi_ref[...]) + _dot(lo, wr_hi_ref[...]) + _dot(hi, wr_lo_ref[...])
    lane = lax.broadcasted_iota(jnp.int32, logits.shape, 1)
    logits = jnp.where(lane < N_EXPERTS, logits, -jnp.inf)
    v1 = jnp.max(logits, axis=-1, keepdims=True)
    i1 = jnp.min(jnp.where(logits == v1, lane, LANES), axis=-1, keepdims=True)
    rest = jnp.where(lane == i1, -jnp.inf, logits)
    v2 = jnp.max(rest, axis=-1, keepdims=True)
    i2 = jnp.min(jnp.where(rest == v2, lane, LANES), axis=-1, keepdims=True)
    e2 = jnp.exp(v2 - v1)
    g1 = 1.0 / (1.0 + e2)
    g2 = e2 / (1.0 + e2)

    row = pl.program_id(0) * ROW_TILE + lax.broadcasted_iota(jnp.int32, (ROW_TILE, 1), 0)
    valid = row < n_tok
    oh1 = jnp.where(jnp.logical_and(lane == i1, valid), 1.0, 0.0)
    oh2 = jnp.where(jnp.logical_and(lane == i2, valid), 1.0, 0.0)
    before = before_ref[...]
    c1 = _dot(before, oh1.astype(BF16))
    c2 = _dot(before, oh2.astype(BF16))
    tot1 = jnp.sum(oh1, axis=0, keepdims=True)
    tot2 = jnp.sum(oh2, axis=0, keepdims=True)
    rank1 = jnp.sum(jnp.where(lane == i1, c1, 0.0), axis=-1, keepdims=True)
    rank2 = jnp.sum(jnp.where(lane == i2, c2 + tot1, 0.0), axis=-1, keepdims=True)
    cnt_o[...] = jnp.broadcast_to(tot1 + tot2, cnt_o.shape)
    route = jnp.where(lane == 0, i1.astype(F32), 0.0)
    for k, val in enumerate((i2.astype(F32), g1, g2, rank1, rank2)):
        route = jnp.where(lane == k + 1, val, route)
    rt_o[...] = route


def _out_router(x, ys, ya, wo, nw, wr_hi, wr_lo, n_main, n_tok):
    n_tiles = n_main + x[1].shape[0] // ROW_TILE
    rows = n_tiles * ROW_TILE
    before = jnp.tril(jnp.ones((ROW_TILE, ROW_TILE), BF16), -1)
    return pl.pallas_call(
        functools.partial(_out_router_kernel, n_main, n_tok),
        grid=(n_tiles,),
        in_specs=_src_specs(D_MODEL, n_main) + _src_specs(D_SSD, n_main) + _src_specs(D_ATTN, n_main) + [
            _resident(wo.shape), _resident(nw.shape), _resident(wr_hi.shape), _resident(wr_lo.shape),
            _resident(before.shape),
        ],
        out_specs=[
            pl.BlockSpec((ROW_TILE, D_MODEL), lambda i: (i, 0)),
            pl.BlockSpec((ROW_TILE, LANES), lambda i: (i, 0)),
            pl.BlockSpec((None, SUBLANES, LANES), lambda i: (i, 0, 0)),
        ],
        out_shape=[
            jax.ShapeDtypeStruct((rows, D_MODEL), F32),
            jax.ShapeDtypeStruct((rows, LANES), F32),
            jax.ShapeDtypeStruct((n_tiles, SUBLANES, LANES), F32),
        ],
        compiler_params=_cparams(1),
        name="out_router",
    )(*x, *ys, *ya, wo, nw, wr_hi, wr_lo, before)


def _tile_rows(idx, n_tiles, tail, fn):
    if tail == ROW_TILE:
        fn(ROW_TILE)
    else:
        pl.when(idx < n_tiles - 1)(lambda: fn(ROW_TILE))
        pl.when(idx == n_tiles - 1)(lambda: fn(tail))


def _dispatch_kernel(n_tiles, tail, zt_ref, dest_ref, x_ref, xs_ref, zbuf, sem, zsem):
    @pl.when(pl.program_id(0) == 0)
    def _():
        zbuf[...] = jnp.zeros(zbuf.shape, F32)

        def zero_copy(j):
            start = pl.multiple_of(zt_ref[j] * ROW_TILE, ROW_TILE)
            return pltpu.make_async_copy(zbuf, xs_ref.at[pl.ds(start, ROW_TILE)], zsem)

        for j in range(zt_ref.shape[0]):
            pl.when(zt_ref[j] >= 0)(lambda j=j: zero_copy(j).start())
        for j in range(zt_ref.shape[0]):
            pl.when(zt_ref[j] >= 0)(lambda j=j: zero_copy(j).wait())

    def run(nrows):
        def body(r, carry):
            for k in range(TOP_K):
                d = dest_ref[0, 0, TOP_K * r + k]
                pltpu.make_async_copy(x_ref.at[pl.ds(r, 1)], xs_ref.at[pl.ds(d, 1)], sem).start()
            return carry

        lax.fori_loop(0, nrows, body, 0, unroll=DMA_UNROLL)
        for k in range(TOP_K):
            pltpu.make_async_copy(x_ref.at[pl.ds(0, nrows)], xs_ref.at[pl.ds(0, nrows)], sem).wait()

    _tile_rows(pl.program_id(0), n_tiles, tail, run)


def _dispatch(last_tile, dest, xm, n_tok, m_rows):
    n_tiles = xm.shape[0] // ROW_TILE
    tail = n_tok - (n_tiles - 1) * ROW_TILE
    grid_spec = pltpu.PrefetchScalarGridSpec(
        num_scalar_prefetch=1,
        grid=(n_tiles,),
        in_specs=[
            pl.BlockSpec((1, 1, TOP_K * ROW_TILE), lambda i, lt: (i, 0, 0), memory_space=pltpu.SMEM),
            pl.BlockSpec((ROW_TILE, D_MODEL), lambda i, lt: (i, 0)),
        ],
        out_specs=pl.BlockSpec(memory_space=pl.ANY),
        scratch_shapes=[pltpu.VMEM((ROW_TILE, D_MODEL), F32), pltpu.SemaphoreType.DMA(()),
                        pltpu.SemaphoreType.DMA(())],
    )
    return pl.pallas_call(
        functools.partial(_dispatch_kernel, n_tiles, tail),
        grid_spec=grid_spec,
        out_shape=jax.ShapeDtypeStruct((m_rows, D_MODEL), F32),
        compiler_params=_cparams(1, has_side_effects=True),
        name="moe_dispatch",
    )(last_tile, dest, xm)


def _moe_kernel(te_ref, nu_ref, x_ref, nw_ref, wg_ref, wu_ref, wd_ref, o_ref):
    i = pl.program_id(0)
    o_ref[...] = jnp.zeros(o_ref.shape, F32)

    @pl.when(i < nu_ref[0])
    def _():
        _swiglu_acc(_rms(x_ref[...], nw_ref[...]).astype(BF16), wg_ref, wu_ref, wd_ref, o_ref)


def _moe_experts(tile_e, n_used, xs, nw, wg, wu, wd):
    rows = xs.shape[0]
    grid_spec = pltpu.PrefetchScalarGridSpec(
        num_scalar_prefetch=2,
        grid=(rows // ROW_TILE,),
        in_specs=[
            pl.BlockSpec((ROW_TILE, D_MODEL), lambda i, te, nu: (jnp.maximum(jnp.minimum(i, nu[0] - 1), 0), 0)),
            pl.BlockSpec((1, D_MODEL), lambda i, te, nu: (0, 0)),
            pl.BlockSpec((None, D_MODEL, D_FF), lambda i, te, nu: (te[i], 0, 0)),
            pl.BlockSpec((None, D_MODEL, D_FF), lambda i, te, nu: (te[i], 0, 0)),
            pl.BlockSpec((None, D_FF, D_MODEL), lambda i, te, nu: (te[i], 0, 0)),
        ],
        out_specs=pl.BlockSpec((ROW_TILE, D_MODEL), lambda i, te, nu: (i, 0)),
    )
    return pl.pallas_call(
        _moe_kernel,
        grid_spec=grid_spec,
        out_shape=jax.ShapeDtypeStruct((rows, D_MODEL), F32),
        compiler_params=_cparams(1),
        name="moe_experts",
    )(tile_e, n_used, xs, nw, wg, wu, wd)


def _combine_kernel(n_main, n_tiles, tail, dcur_ref, dnext_ref, xm_ref, rt_ref, yb_ref,
                    om_ref, ot_ref, gbuf, sem):
    i = pl.program_id(0)

    def issue(dref, slot, nrows):
        def body(r, carry):
            for k in range(TOP_K):
                d = dref[0, 0, TOP_K * r + k]
                pltpu.make_async_copy(yb_ref.at[pl.ds(d, 1)], gbuf.at[slot, k, pl.ds(r, 1)],
                                      sem.at[slot]).start()
            return carry

        lax.fori_loop(0, nrows, body, 0, unroll=DMA_UNROLL)

    def wait(slot, nrows):
        for k in range(TOP_K):
            pltpu.make_async_copy(yb_ref.at[pl.ds(0, nrows)], gbuf.at[slot, k, pl.ds(0, nrows)],
                                  sem.at[slot]).wait()

    @pl.when(i == 0)
    def _():
        gbuf[...] = jnp.zeros(gbuf.shape, F32)
        _tile_rows(i, n_tiles, tail, lambda n: issue(dcur_ref, 0, n))

    for slot in range(2):
        @pl.when(jnp.logical_and(i + 1 < n_tiles, (i + 1) % 2 == slot))
        def _(slot=slot):
            _tile_rows(i + 1, n_tiles, tail, lambda n: issue(dnext_ref, slot, n))

    for slot in range(2):
        @pl.when(i % 2 == slot)
        def _(slot=slot):
            _tile_rows(i, n_tiles, tail, lambda n: wait(slot, n))
            gates = rt_ref[...]
            val = (xm_ref[...] + gates[:, TOP_K:TOP_K + 1] * gbuf[slot, 0]
                   + gates[:, TOP_K + 1:TOP_K + 2] * gbuf[slot, 1])

            @pl.when(i < n_main)
            def _():
                om_ref[...] = val

            @pl.when(i >= n_main)
            def _():
                ot_ref[...] = val


def _combine(dest, xm, route, yb, n_main, n_tok):
    n_tiles = xm.shape[0] // ROW_TILE
    tail = n_tok - (n_tiles - 1) * ROW_TILE
    dspec = lambda f: pl.BlockSpec((1, 1, TOP_K * ROW_TILE), f, memory_space=pltpu.SMEM)
    return pl.pallas_call(
        functools.partial(_combine_kernel, n_main, n_tiles, tail),
        grid=(n_tiles,),
        in_specs=[
            dspec(lambda i: (i, 0, 0)),
            dspec(lambda i: (jnp.minimum(i + 1, n_tiles - 1), 0, 0)),
            pl.BlockSpec((ROW_TILE, D_MODEL), lambda i: (i, 0)),
            pl.BlockSpec((ROW_TILE, LANES), lambda i: (i, 0)),
            pl.BlockSpec(memory_space=pl.ANY),
        ],
        out_specs=_src_specs(D_MODEL, n_main),
        out_shape=[jax.ShapeDtypeStruct((n_main * ROW_TILE, D_MODEL), F32),
                   jax.ShapeDtypeStruct(((n_tiles - n_main) * ROW_TILE, D_MODEL), F32)],
        scratch_shapes=[pltpu.VMEM((2, TOP_K, ROW_TILE, D_MODEL), F32), pltpu.SemaphoreType.DMA((2,))],
        compiler_params=_cparams(1),
        name="moe_combine",
    )(dest, dest, xm, route, yb)


def _moe_layer(xm, route, counts, n_main, n_tok, nw, wg, wu, wd):
    n_tiles = xm.shape[0] // ROW_TILE
    m_tiles = -(-(n_tok * TOP_K + N_EXPERTS * (ROW_TILE - 1)) // ROW_TILE)
    cnt = counts[:, 0, :N_EXPERTS].astype(jnp.int32)
    total = jnp.sum(cnt, axis=0)
    padded = (total + ROW_TILE - 1) // ROW_TILE * ROW_TILE
    pad_end = jnp.cumsum(padded)
    base = (pad_end - padded)[None, :] + jnp.cumsum(cnt, axis=0) - cnt
    e = route[:, 0:TOP_K].astype(jnp.int32).reshape(n_tiles, ROW_TILE, TOP_K)
    rank = route[:, 2 * TOP_K:3 * TOP_K].astype(jnp.int32).reshape(n_tiles, ROW_TILE, TOP_K)
    onehot = e[..., None] == jnp.arange(N_EXPERTS, dtype=jnp.int32)
    dest = jnp.sum(jnp.where(onehot, base[:, None, None, :], 0), axis=-1) + rank
    dest = dest.reshape(n_tiles, 1, ROW_TILE * TOP_K)
    tile_start = jnp.arange(m_tiles, dtype=jnp.int32) * ROW_TILE
    tile_e = jnp.minimum(jnp.sum((pad_end[None, :] <= tile_start[:, None]).astype(jnp.int32), axis=1),
                         N_EXPERTS - 1)
    n_used = (pad_end[-1:] // ROW_TILE).astype(jnp.int32)
    last_tile = jnp.where(padded > 0, pad_end // ROW_TILE - 1, -1)
    spare = n_used[0] + jnp.arange(m_tiles - (n_tok * TOP_K) // ROW_TILE, dtype=jnp.int32)
    zero_tiles = jnp.concatenate([last_tile, jnp.where(spare < m_tiles, spare, -1)]).astype(jnp.int32)
    xs = _dispatch(zero_tiles, dest, xm, n_tok, m_tiles * ROW_TILE)
    yb = _moe_experts(tile_e, n_used, xs, nw, wg, wu, wd)
    return _combine(dest, xm, route, yb, n_main, n_tok)


def _rope_tables(pos):
    half = HEAD_DIM // 2
    inv_freq = ROPE_THETA ** (-jnp.arange(half, dtype=F32) / half)
    ang = pos.astype(F32)[:, None] * inv_freq[None, :]
    cos = jnp.cos(ang)
    sin = jnp.sin(ang)
    reps = LANES // HEAD_DIM
    return (jnp.tile(jnp.concatenate([cos, cos], axis=-1), (1, reps)),
            jnp.tile(jnp.concatenate([-sin, sin], axis=-1), (1, reps)))


def _pad_lanes(v, width=LANES):
    v = v.astype(F32).reshape(1, -1)
    return jnp.pad(v, ((0, 0), (0, width - v.shape[1])))


def kernel(x_prompt, x_sample, state_ssm, state_conv, cache_meta_k, cache_meta_v, cache_win_k, cache_win_v, meta_tokens, norm_mix_w, w_in, conv_w, conv_b, dt_bias, a_log, d_skip, ssd_norm_w, q_norm_w, k_norm_w, sinks, attn_norm_w, w_out, norm_ffn_w, w_gate, w_up, w_down, w_router, moe_w_gate, moe_w_up, moe_w_down):
    bsz, seq, _ = x_prompt.shape
    nseq, t_s, _ = x_sample.shape
    depth = w_in.shape[0]
    r_main = bsz * seq
    r_samp = nseq * t_s
    assert seq % CHUNK == 0 and t_s == SUBLANES and r_main % ROW_TILE == 0 and r_samp % N_META == 0
    n_main = r_main // ROW_TILE
    n_tok = r_main + r_samp + N_META
    r_tail = -(-(r_samp + N_META) // ROW_TILE) * ROW_TILE
    tail_pad = r_tail - r_samp - N_META

    x = (x_prompt.reshape(r_main, D_MODEL),
         jnp.concatenate([x_sample.reshape(r_samp, D_MODEL), meta_tokens.astype(F32),
                          jnp.zeros((tail_pad, D_MODEL), F32)], axis=0))

    tabs_main = _rope_tables(N_META + jnp.arange(seq, dtype=jnp.int32))
    tabs_meta = _rope_tables(jnp.arange(CHUNK, dtype=jnp.int32) - (CHUNK - N_META))
    nsq = 16 if nseq % 16 == 0 else nseq
    cos_s, sin_s = (jnp.tile(tab, (nsq, 1)) for tab in
                    _rope_tables(PAST_LEN + jnp.arange(t_s, dtype=jnp.int32)))

    o_z, o_xbc, o_dt, o_q, o_k, o_v = 0, 512, 1536, 1544, 2056, 2184
    col = jnp.arange(D_ATTN, dtype=jnp.int32)
    grp, lane = col // LANES, col % LANES
    head_perm = (grp + Q_PER_KV * (lane // HEAD_DIM)) * HEAD_DIM + lane % HEAD_DIM

    def heads4(a):
        return a.reshape(a.shape[0], a.shape[1], N_KV_HEADS, HEAD_DIM)

    outs = {k: [] for k in ('p_ssm', 'p_conv', 'p_mk', 'p_mv', 'p_wk', 'p_wv', 's_ssm', 's_conv', 's_wk', 's_wv')}
    for l in range(depth):
        wl = w_in[l]
        w_re = jnp.concatenate([
            wl[:, o_xbc:o_xbc + CONV_DIM], wl[:, o_z:o_z + D_SSD], wl[:, o_q:o_q + D_ATTN][:, head_perm],
            wl[:, o_k:o_k + KV_DIM], wl[:, o_v:o_v + KV_DIM], wl[:, o_dt:o_dt + N_SSD_HEADS],
            jnp.zeros((D_MODEL, PROJ_W - COL_DT - N_SSD_HEADS), wl.dtype)], axis=1).astype(BF16)
        proj_main, proj_tail = _in_proj(x[0], x[1], norm_mix_w[l].reshape(1, D_MODEL).astype(F32), w_re, n_main)

        ssd_params = (conv_w[l].astype(F32), conv_b[l].reshape(1, CONV_DIM).astype(F32),
                      _pad_lanes(dt_bias[l]), _pad_lanes(-jnp.exp(a_log[l].astype(F32))),
                      jnp.repeat(d_skip[l].astype(F32), SSD_HEAD_DIM).reshape(1, D_SSD),
                      ssd_norm_w[l].reshape(1, D_SSD).astype(F32))
        ys_p, ys_m, ssm_p, conv_p = _ssd_prompt(proj_main, proj_tail, ssd_params, bsz, seq, r_samp)
        conv_prev = jnp.pad(state_conv[l].astype(F32), ((0, 0), (SUBLANES - (CONV_W - 1), 0), (0, 0)))
        ys_s, ssm_s = _ssd_sample(proj_tail, conv_prev, state_ssm[l].astype(F32), ssd_params, nseq, t_s, nsq)

        attn_params = (jnp.tile(q_norm_w[l].astype(F32), N_Q_HEADS).reshape(1, D_ATTN),
                       jnp.tile(k_norm_w[l].astype(F32), N_KV_HEADS).reshape(1, KV_DIM),
                       _pad_lanes(sinks[l]), attn_norm_w[l][head_perm].reshape(1, D_ATTN).astype(F32))
        ya_p, ya_m, kp_m, k_last, v_last = _attn_prompt(proj_main, proj_tail, tabs_main, tabs_meta,
                                                        attn_params, bsz, seq, r_samp)
        ya_s, kp_s = _attn_sample(
            proj_tail, cache_meta_k[l].reshape(nseq, N_META, KV_DIM).astype(F32),
            cache_meta_v[l].reshape(nseq, N_META, KV_DIM).astype(F32),
            cache_win_k[l].reshape(nseq, WINDOW, KV_DIM).astype(F32),
            cache_win_v[l].reshape(nseq, WINDOW, KV_DIM).astype(F32),
            cos_s, sin_s, attn_params, nseq, t_s, nsq)

        ys = (ys_p, jnp.concatenate([ys_s, ys_m[0], jnp.zeros((tail_pad, D_SSD), BF16)], axis=0))
        ya = (ya_p, jnp.concatenate([ya_s, ya_m[0], jnp.zeros((tail_pad, D_ATTN), BF16)], axis=0))

        wo = jnp.concatenate([w_out[l][:D_SSD], w_out[l][D_SSD:][head_perm]], axis=0).astype(BF16)
        nfw = norm_ffn_w[l].reshape(1, D_MODEL).astype(F32)
        i = l // 2
        if l % 2 == 0:
            x = _out_ffn(x, ys, ya, wo, nfw, w_gate[i].astype(BF16), w_up[i].astype(BF16),
                         w_down[i].astype(BF16), n_main)
        else:
            wr = jnp.pad(w_router[i].astype(F32), ((0, 0), (0, LANES - N_EXPERTS)))
            wr_hi = wr.astype(BF16)
            wr_lo = (wr - wr_hi.astype(F32)).astype(BF16)
            xm, route, counts = _out_router(x, ys, ya, wo, nfw, wr_hi, wr_lo, n_main, n_tok)
            x = _moe_layer(xm, route, counts, n_main, n_tok, nfw, moe_w_gate[i].astype(BF16),
                           moe_w_up[i].astype(BF16), moe_w_down[i].astype(BF16))

        samp = proj_tail[:r_samp]
        xbc_s = samp[:, COL_XBC:COL_XBC + CONV_DIM].reshape(nseq, t_s, CONV_DIM)
        v_s = samp[:, COL_V:COL_V + KV_DIM].reshape(nseq, t_s, KV_DIM)
        v_meta = proj_tail[r_samp:r_samp + N_META, COL_V:COL_V + KV_DIM]
        meta_shape = (bsz, N_META, N_KV_HEADS, HEAD_DIM)
        outs['p_ssm'].append(ssm_p)
        outs['p_conv'].append(conv_p[:, SUBLANES - (CONV_W - 1):])
        outs['p_mk'].append(jnp.broadcast_to(heads4(kp_m[0:1]), meta_shape))
        outs['p_mv'].append(jnp.broadcast_to(heads4(v_meta[None]), meta_shape))
        outs['p_wk'].append(heads4(k_last))
        outs['p_wv'].append(heads4(v_last))
        outs['s_ssm'].append(ssm_s)
        outs['s_conv'].append(jnp.concatenate([state_conv[l].astype(F32), xbc_s], axis=1)[:, t_s:])
        outs['s_wk'].append(jnp.concatenate([cache_win_k[l].astype(F32),
                                             heads4(kp_s.reshape(nseq, t_s, KV_DIM))], axis=1)[:, t_s:])
        outs['s_wv'].append(jnp.concatenate([cache_win_v[l].astype(F32), heads4(v_s)], axis=1)[:, t_s:])

    y_prompt = x[0][:r_main].reshape(bsz, seq, D_MODEL)
    y_sample = x[1][:r_samp].reshape(nseq, t_s, D_MODEL)
    st = lambda k: jnp.stack(outs[k])
    return (y_prompt, y_sample, st('p_ssm'), st('p_conv'), st('p_mk'), st('p_mv'), st('p_wk'), st('p_wv'),
            st('s_ssm'), st('s_conv'), st('s_wk'), st('s_wv'))
```

```python
import functools

import jax
import jax.numpy as jnp
from jax import lax
from jax.experimental import pallas as pl
from jax.experimental.pallas import tpu as pltpu

F32 = jnp.float32
BF16 = jnp.bfloat16

D_MODEL = 1024
D_SSD = 512
SSD_HEAD_DIM = 64
N_SSD_HEADS = 8
SSD_HEADS_PER_GROUP = 4
N_SSD_GROUPS = 2
D_STATE = 128
CONV_W = 4
CONV_DIM = 1024
D_ATTN = 512
HEAD_DIM = 64
N_Q_HEADS = 8
N_KV_HEADS = 2
Q_PER_KV = 4
KV_DIM = 128
WINDOW = 128
N_META = 16
D_FF = 2816
N_EXPERTS = 8
TOP_K = 2
EPS = 1e-6
NEG = -1e30
ATTN_SCALE = HEAD_DIM ** -0.5
PAST_LEN = 16384
ROPE_THETA = 10000.0

LANES = 128
SUBLANES = 8
CHUNK = 128
ROW_TILE = 512
FF_CHUNK = 256
DMA_UNROLL = 8
VMEM_LIMIT = 60 * 1024 * 1024

COL_XBC = 0
COL_Z = 1024
COL_Q = 1536
COL_K = 2048
COL_V = 2176
COL_DT = 2304
PROJ_W = 2432
PROJ_STEP = 512


def _dot(a, b):
    return jnp.dot(a, b, preferred_element_type=F32)


def _dot_nt(a, b):
    return lax.dot_general(a, b, (((1,), (1,)), ((), ())), preferred_element_type=F32)


def _dot_tn(a, b):
    return lax.dot_general(a, b, (((0,), (0,)), ((), ())), preferred_element_type=F32)


def _rms(x, w):
    return x * lax.rsqrt(jnp.mean(x * x, axis=-1, keepdims=True) + EPS) * w


def _silu(x):
    return x * jax.nn.sigmoid(x)


def _split3(x):
    p1 = x.astype(BF16)
    r1 = x - p1.astype(F32)
    p2 = r1.astype(BF16)
    p3 = (r1 - p2.astype(F32)).astype(BF16)
    return p1, p2, p3


def _dot_exact(sel, x):
    p1, p2, p3 = _split3(x)
    return _dot(sel, p1) + _dot(sel, p2) + _dot(sel, p3)


def _cparams(ndim, **kw):
    return pltpu.CompilerParams(dimension_semantics=("arbitrary",) * ndim, vmem_limit_bytes=VMEM_LIMIT, **kw)


def _src_specs(width, n_main):
    return [pl.BlockSpec((ROW_TILE, width), lambda i: (jnp.minimum(i, n_main - 1), 0)),
            pl.BlockSpec((ROW_TILE, width), lambda i: (jnp.maximum(i - n_main, 0), 0))]


def _pick(n_main, main_ref, tail_ref):
    dtype = main_ref.dtype
    picked = jnp.where(pl.program_id(0) < n_main, main_ref[...].astype(F32), tail_ref[...].astype(F32))
    return picked.astype(dtype)


def _dst_specs(width, n_main):
    return [pl.BlockSpec((ROW_TILE, width), lambda i: (jnp.minimum(i, n_main), 0)),
            pl.BlockSpec((ROW_TILE, width), lambda i: (jnp.maximum(i - n_main, 0), 0))]


def _dst_shapes(width, n_main, n_tail, dtype):
    return [jax.ShapeDtypeStruct(((n_main + 1) * ROW_TILE, width), dtype),
            jax.ShapeDtypeStruct((n_tail * ROW_TILE, width), dtype)]


def _resident(shape):
    nd = len(shape)
    return pl.BlockSpec(shape, lambda *a: (0,) * nd, pipeline_mode=pl.Buffered(1))


def _param_specs(params):
    return [pl.BlockSpec(p.shape, lambda *a: (0, 0)) for p in params]


def _in_proj_kernel(n_main, xa_ref, xb_ref, nw_ref, w_ref, om_ref, ot_ref):
    xn = _rms(_pick(n_main, xa_ref, xb_ref), nw_ref[...]).astype(BF16)
    for c0 in range(0, PROJ_W, PROJ_STEP):
        cols = slice(c0, min(c0 + PROJ_STEP, PROJ_W))
        r = _dot(xn, w_ref[:, cols])
        om_ref[:, cols] = r
        ot_ref[:, cols] = r


def _in_proj(x_main, x_tail, nw, w, n_main):
    n_tail = x_tail.shape[0] // ROW_TILE
    return pl.pallas_call(
        functools.partial(_in_proj_kernel, n_main),
        grid=(n_main + n_tail,),
        in_specs=_src_specs(D_MODEL, n_main) + [
            pl.BlockSpec((1, D_MODEL), lambda i: (0, 0)),
            pl.BlockSpec((D_MODEL, PROJ_W), lambda i: (0, 0)),
        ],
        out_specs=_dst_specs(PROJ_W, n_main),
        out_shape=_dst_shapes(PROJ_W, n_main, n_tail, F32),
        compiler_params=_cparams(1),
        name="in_proj",
    )(x_main, x_tail, nw, w)


def _ssd_chunk(xbc, z, dt_raw, valid, cw_ref, cb_ref, dtb_ref, an_ref, dsk_ref, nw_ref, cbuf, hst):
    q = CHUNK
    ext = jnp.concatenate([cbuf[...], xbc], axis=0)
    cbuf[...] = xbc[q - SUBLANES:q, :]
    acc = cb_ref[...] + xbc * cw_ref[CONV_W - 1:CONV_W, :]
    for s in range(1, CONV_W):
        acc = acc + pltpu.roll(ext, s, 0)[SUBLANES:, :] * cw_ref[CONV_W - 1 - s:CONV_W - s, :]
    xc = _silu(acc)
    xs = xc[:, :D_SSD]
    bm = xc[:, D_SSD:D_SSD + N_SSD_GROUPS * D_STATE]
    cm = xc[:, D_SSD + N_SSD_GROUPS * D_STATE:]

    dt = jax.nn.softplus(dt_raw + dtb_ref[...])
    if valid is not None:
        dt = jnp.where(valid, dt, 0.0)
    da = dt * an_ref[...]
    row_i = lax.broadcasted_iota(jnp.int32, (q, q), 0)
    col_j = lax.broadcasted_iota(jnp.int32, (q, q), 1)
    tril = row_i >= col_j
    cs = _dot_exact(jnp.where(tril, 1.0, 0.0).astype(BF16), da)
    cs_t = cs.T
    last = cs[q - 1:q, :]
    ecl = jnp.exp(last)
    xdt = xs * _expand_heads(dt)
    xw = xdt * _expand_heads(jnp.exp(last - cs))
    ecs_x = _expand_heads(jnp.exp(cs))
    skip = xs * dsk_ref[...]

    gw = SSD_HEADS_PER_GROUP * SSD_HEAD_DIM
    head_of_col = jnp.right_shift(lax.broadcasted_iota(jnp.int32, (q, gw), 1), SSD_HEAD_DIM.bit_length() - 1)
    ys = []
    for g in range(N_SSD_GROUPS):
        bg = bm[:, g * D_STATE:(g + 1) * D_STATE].astype(BF16)
        cg = cm[:, g * D_STATE:(g + 1) * D_STATE].astype(BF16)
        cb = _dot_nt(cg, bg)
        gc = slice(g * gw, (g + 1) * gw)
        h_prev = hst[gc, :]
        y_g = _dot_nt(cg, h_prev.astype(BF16)) * ecs_x[:, gc] + skip[:, gc]
        xdt_g = xdt[:, gc]
        for r in range(SSD_HEADS_PER_GROUP):
            h = g * SSD_HEADS_PER_GROUP + r
            seg = cs[:, h:h + 1] - cs_t[h:h + 1, :]
            decay = jnp.where(tril, jnp.exp(jnp.where(tril, seg, 0.0)), 0.0)
            y_g = y_g + _dot((decay * cb).astype(BF16), jnp.where(head_of_col == r, xdt_g, 0.0).astype(BF16))
        ys.append(y_g)
        ecl_col = jnp.concatenate(
            [jnp.broadcast_to(ecl[:, h:h + 1], (SSD_HEAD_DIM, 1))
             for h in range(g * SSD_HEADS_PER_GROUP, (g + 1) * SSD_HEADS_PER_GROUP)], axis=0)
        hst[gc, :] = h_prev * ecl_col + _dot_tn(xw[:, gc].astype(BF16), bg)

    return _ssd_gate_norm(jnp.concatenate(ys, axis=1), z, nw_ref)


def _ssd_gate_norm(y, z, nw_ref):
    y = y * _silu(z)
    gs = D_SSD // N_SSD_GROUPS
    return jnp.concatenate([_rms(y[:, g * gs:(g + 1) * gs], nw_ref[:, g * gs:(g + 1) * gs])
                            for g in range(N_SSD_GROUPS)], axis=-1)


N_SSD_IN, N_SSD_OUT, N_SSD_SCRATCH = 12, 4, 3


def _ssd_prompt_phases(xbc_m, z_m, dt_m, xbc_t, z_t, dt_t, cw, cb, dtb, an, dsk, nw,
                       y_o, ym_o, hf_o, ct_o, cbuf, hst, ymbuf):
    pad = CHUNK - N_META
    prm = (cw, cb, dtb, an, dsk, nw)

    def meta():
        cbuf[...] = jnp.zeros((SUBLANES, CONV_DIM), F32)
        hst[...] = jnp.zeros(hst.shape, F32)

        def stage(meta_ref):
            return jnp.concatenate([jnp.zeros((pad, meta_ref.shape[1]), F32), meta_ref[...]], axis=0)

        row = lax.broadcasted_iota(jnp.int32, (CHUNK, 1), 0)
        y = _ssd_chunk(stage(xbc_t), stage(z_t), stage(dt_t), row >= pad, *prm, cbuf, hst)
        ymbuf[...] = y[pad:, :]

    def block(j):
        rows = slice(j * CHUNK, (j + 1) * CHUNK)
        y = _ssd_chunk(xbc_m[rows, :], z_m[rows, :], dt_m[rows, :], None, *prm, cbuf, hst)
        y_o[rows, :] = y.astype(BF16)

    def every():
        ym_o[...] = ymbuf[...].astype(BF16)
        hf_o[...] = hst[...].reshape(hf_o.shape)
        ct_o[...] = cbuf[...]

    return meta, block, every


def _chunks_per_step(nc):
    return 4 if nc % 4 == 0 else (2 if nc % 2 == 0 else 1)


def _expand_heads(a):
    hh = lax.broadcasted_iota(jnp.int32, (LANES, D_SSD), 0)
    cc = lax.broadcasted_iota(jnp.int32, (LANES, D_SSD), 1)
    sel = jnp.where(jnp.right_shift(cc, SSD_HEAD_DIM.bit_length() - 1) == hh, 1.0, 0.0).astype(BF16)
    p1, p2, p3 = _split3(a)
    return _dot(p1, sel) + _dot(p2, sel) + _dot(p3, sel)


def _pad_rows_bf16(x, rows):
    return jnp.concatenate([x, jnp.zeros((rows - x.shape[0], x.shape[1]), F32)], axis=0).astype(BF16)


def _ssd_sample_kernel(xbc_s, z_s, dt_s, cprev, h0, cw, cb, dtb, an, dsk, nw,
                       y_o, hf_o, ubuf, ybuf):
    nsq = cprev.shape[0]
    t = SUBLANES
    rows = nsq * t
    x = xbc_s[...]
    for s in range(nsq):
        ubuf[2 * t * s:2 * t * s + t, :] = cprev[s]
        ubuf[2 * t * s + t:2 * t * (s + 1), :] = x[s * t:(s + 1) * t, :]
    parts = []
    for s in range(nsq):
        acc = cb[...]
        for k in range(CONV_W):
            off = 2 * t * s + t - (CONV_W - 1) + k
            acc = acc + ubuf[off:off + t, :] * cw[k:k + 1, :]
        parts.append(acc)
    xc = _silu(jnp.concatenate(parts, axis=0))
    xs = xc[:, :D_SSD]
    bm = xc[:, D_SSD:D_SSD + N_SSD_GROUPS * D_STATE]
    cm = xc[:, D_SSD + N_SSD_GROUPS * D_STATE:]

    dt = jax.nn.softplus(dt_s[...] + dtb[...])
    da = dt * an[...]
    ri = lax.broadcasted_iota(jnp.int32, (rows, rows), 0)
    cj = lax.broadcasted_iota(jnp.int32, (rows, rows), 1)
    mask = jnp.logical_and(ri >= cj, jnp.right_shift(ri, 3) == jnp.right_shift(cj, 3))
    cs = _dot_exact(jnp.where(mask, 1.0, 0.0).astype(BF16), da)
    last = _dot_exact(jnp.where(cj == jnp.bitwise_or(ri, t - 1), 1.0, 0.0).astype(BF16), cs)
    cs_t = cs.T
    ecl = jnp.exp(last)
    xdt = xs * _expand_heads(dt)
    xw = xdt * _expand_heads(jnp.exp(last - cs))
    ecs_x = _expand_heads(jnp.exp(cs))
    skip = xs * dsk[...]

    gw = SSD_HEADS_PER_GROUP * SSD_HEAD_DIM
    for g in range(N_SSD_GROUPS):
        bg_f = bm[:, g * D_STATE:(g + 1) * D_STATE]
        cg_f = cm[:, g * D_STATE:(g + 1) * D_STATE]
        cbm = _dot_nt(cg_f.astype(BF16), bg_f.astype(BF16))
        for r in range(SSD_HEADS_PER_GROUP):
            h = g * SSD_HEADS_PER_GROUP + r
            hc = slice(h * SSD_HEAD_DIM, (h + 1) * SSD_HEAD_DIM)
            seg = cs[:, h:h + 1] - cs_t[h:h + 1, :]
            decay = jnp.where(mask, jnp.exp(jnp.where(mask, seg, 0.0)), 0.0)
            ybuf[:, hc] = _dot((decay * cbm).astype(BF16), xdt[:, hc].astype(BF16)) + skip[:, hc]
        gc = slice(g * gw, (g + 1) * gw)
        heads = slice(g * SSD_HEADS_PER_GROUP, (g + 1) * SSD_HEADS_PER_GROUP)
        for s in range(nsq):
            rs = slice(s * t, (s + 1) * t)
            hg = h0[s, heads].reshape(gw, D_STATE)
            y_off = _dot_nt(_pad_rows_bf16(cg_f[rs, :], 2 * t), hg.astype(BF16))[0:t, :]
            ybuf[rs, gc] = ybuf[rs, gc] + y_off * ecs_x[rs, gc]
            ecl_col = jnp.concatenate(
                [jnp.broadcast_to(ecl[s * t:s * t + 1, h:h + 1], (SSD_HEAD_DIM, 1))
                 for h in range(heads.start, heads.stop)], axis=0)
            h_new = hg * ecl_col + _dot_tn(_pad_rows_bf16(xw[rs, gc], 2 * t), _pad_rows_bf16(bg_f[rs, :], 2 * t))
            hf_o[s, heads] = h_new.reshape(SSD_HEADS_PER_GROUP, SSD_HEAD_DIM, D_STATE)

    y_o[...] = _ssd_gate_norm(ybuf[...], z_s[...], nw).astype(BF16)


def _ssd_sample(proj_tail, conv_prev, h0, params, nseq, t, nsq):
    blk = nsq * t
    in_specs = [
        pl.BlockSpec((blk, CONV_DIM), lambda b: (b, COL_XBC // CONV_DIM)),
        pl.BlockSpec((blk, D_SSD), lambda b: (b, COL_Z // D_SSD)),
        pl.BlockSpec((blk, LANES), lambda b: (b, COL_DT // LANES)),
        pl.BlockSpec((nsq, SUBLANES, CONV_DIM), lambda b: (b, 0, 0)),
        pl.BlockSpec((nsq, N_SSD_HEADS, SSD_HEAD_DIM, D_STATE), lambda b: (b, 0, 0, 0)),
    ] + _param_specs(params)
    return pl.pallas_call(
        _ssd_sample_kernel,
        grid=(nseq // nsq,),
        in_specs=in_specs,
        out_specs=[
            pl.BlockSpec((blk, D_SSD), lambda b: (b, 0)),
            pl.BlockSpec((nsq, N_SSD_HEADS, SSD_HEAD_DIM, D_STATE), lambda b: (b, 0, 0, 0)),
        ],
        out_shape=[
            jax.ShapeDtypeStruct((nseq * t, D_SSD), BF16),
            jax.ShapeDtypeStruct((nseq, N_SSD_HEADS, SSD_HEAD_DIM, D_STATE), F32),
        ],
        scratch_shapes=[pltpu.VMEM((2 * blk, CONV_DIM), F32), pltpu.VMEM((blk, D_SSD), F32)],
        compiler_params=_cparams(1),
        name="ssd_sample",
    )(proj_tail, proj_tail, proj_tail, conv_prev, h0, *params)


def _qk_prep(x, w, cos, sin):
    lane = lax.broadcasted_iota(jnp.int32, x.shape, 1)
    lo_head = lane < HEAD_DIM
    sq = x * x
    s_lo = jnp.sum(jnp.where(lo_head, sq, 0.0), axis=-1, keepdims=True)
    s_all = jnp.sum(sq, axis=-1, keepdims=True)
    ms = jnp.where(lo_head, s_lo, s_all - s_lo) * (1.0 / HEAD_DIM)
    xn = x * lax.rsqrt(ms + EPS) * w
    half = HEAD_DIM // 2
    first_half = (lane % HEAD_DIM) < half
    partner = jnp.where(first_half, pltpu.roll(xn, LANES - half, 1), pltpu.roll(xn, half, 1))
    return xn * cos + partner * sin


def _q_groups(q, qnw_ref, cos, sin):
    return [_qk_prep(q[:, g * LANES:(g + 1) * LANES], qnw_ref[:, g * LANES:(g + 1) * LANES], cos, sin)
            * ATTN_SCALE for g in range(D_ATTN // LANES)]


def _sink_column(snk_ref, rows):
    return jnp.concatenate([jnp.broadcast_to(snk_ref[:, h:h + 1], (rows, 1)) for h in range(N_Q_HEADS)],
                           axis=0)


def _softmax_weights(pieces, sk):
    top = pieces[0]
    for p in pieces[1:]:
        top = jnp.maximum(top, p)
    m = jnp.maximum(jnp.max(top, axis=-1, keepdims=True), sk)
    e = [jnp.exp(p - m) for p in pieces]
    tot = e[0]
    for p in e[1:]:
        tot = tot + p
    return e, 1.0 / (jnp.sum(tot, axis=-1, keepdims=True) + jnp.exp(sk - m))


def _kv_prep_t(k, v, knw_ref, cos, sin):
    kp = _qk_prep(k, knw_ref[...], cos, sin)
    lane = lax.broadcasted_iota(jnp.int32, kp.shape, 1)
    return (kp, jnp.where(lane < HEAD_DIM, kp, 0.0).astype(BF16),
            jnp.where(lane < HEAD_DIM, 0.0, kp).astype(BF16), v.T.astype(BF16))


def _scores_t(qs, key_lo, key_hi):
    return jnp.concatenate([_dot_nt(key_lo, qs), _dot_nt(key_hi, qs)], axis=1)


def _softmax_weights_t(pieces, sk):
    m = sk
    for p in pieces:
        m = jnp.maximum(m, jnp.max(p, axis=0, keepdims=True))
    e = [jnp.exp(p - m) for p in pieces]
    den = jnp.exp(sk - m)
    for p in e:
        den = den + jnp.sum(p, axis=0, keepdims=True)
    return e, 1.0 / den


def _meta_weights(e_m):
    return jnp.concatenate([e_m.astype(BF16), jnp.zeros((CHUNK - N_META, e_m.shape[1]), BF16)], axis=0)


def _attn_out_t(o_t, anw_ref):
    nst = Q_PER_KV * CHUNK
    d = lax.broadcasted_iota(jnp.int32, (LANES, CHUNK), 0)
    groups = [jnp.where(d < HEAD_DIM, o_t[:, g * CHUNK:(g + 1) * CHUNK],
                        o_t[:, nst + g * CHUNK:nst + (g + 1) * CHUNK]).T for g in range(Q_PER_KV)]
    return _rms(jnp.concatenate(groups, axis=1), anw_ref[...])


N_ATTN_IN, N_ATTN_OUT, N_ATTN_SCRATCH = 14, 5, 8


def _attn_prompt_phases(q_m, k_m, v_m, q_t, k_t, v_t, cos_m, sin_m, cos_t, sin_t, qnw, knw, snk, anw,
                        o_o, om_o, kpm_o, kl_o, vl_o,
                        km_lo, km_hi, vmt, kp_lo, kp_hi, vpt, kmf, ombuf):
    c = pl.program_id(1)
    pad = CHUNK - N_META
    cols = N_Q_HEADS * CHUNK
    nblk = q_m.shape[0] // CHUNK
    carry = {}

    def sink_row():
        return jnp.concatenate([jnp.broadcast_to(snk[:, h:h + 1], (1, CHUNK)) for h in range(N_Q_HEADS)],
                               axis=1)

    def meta():
        sk = sink_row()

        def stage(meta_ref):
            return jnp.concatenate([jnp.zeros((pad, meta_ref.shape[1]), F32), meta_ref[...]], axis=0)

        cs, sn = cos_t[...], sin_t[...]
        kp = _qk_prep(stage(k_t), knw[...], cs, sn)
        lane = lax.broadcasted_iota(jnp.int32, (N_META, LANES), 1)
        km_lo[...] = jnp.where(lane < HEAD_DIM, kp[pad:, :], 0.0).astype(BF16)
        km_hi[...] = jnp.where(lane < HEAD_DIM, 0.0, kp[pad:, :]).astype(BF16)
        kmf[...] = kp[pad:, :]
        vmt[...] = jnp.concatenate([v_t[...], jnp.zeros((pad, LANES), F32)], axis=0).T.astype(BF16)
        kp_lo[...] = jnp.zeros(kp_lo.shape, BF16)
        kp_hi[...] = jnp.zeros(kp_hi.shape, BF16)
        vpt[...] = jnp.zeros(vpt.shape, BF16)
        qs = jnp.concatenate(_q_groups(stage(q_t), qnw, cs, sn), axis=0).astype(BF16)
        r = lax.broadcasted_iota(jnp.int32, (N_META, cols), 0)
        qi = lax.broadcasted_iota(jnp.int32, (N_META, cols), 1) & (CHUNK - 1)
        s_m = jnp.where(r <= qi - pad, _scores_t(qs, km_lo[...], km_hi[...]), NEG)
        (e_m,), inv = _softmax_weights_t([s_m], sk)
        ombuf[...] = _attn_out_t(_dot(vmt[...], _meta_weights(e_m)) * inv, anw)[pad:, :]

    def begin():
        carry['prev'] = (kp_lo[...], kp_hi[...], vpt[...])
        carry['sk'] = sink_row()
        r = lax.broadcasted_iota(jnp.int32, (CHUNK, cols), 0)
        qi = lax.broadcasted_iota(jnp.int32, (CHUNK, cols), 1) & (CHUNK - 1)
        carry['tri'] = r <= qi

    def block(j):
        prev, tri = carry['prev'], carry['tri']
        rows = slice(j * CHUNK, (j + 1) * CHUNK)
        cs, sn = cos_m[rows, :], sin_m[rows, :]
        v = v_m[rows, :]
        kp, k_lo, k_hi, v_tb = _kv_prep_t(k_m[rows, :], v, knw, cs, sn)
        qs = jnp.concatenate(_q_groups(q_m[rows, :], qnw, cs, sn), axis=0).astype(BF16)
        band = jnp.where(tri, _scores_t(qs, k_lo, k_hi), _scores_t(qs, prev[0], prev[1]))
        if j == 0:
            band = jnp.where(jnp.logical_or(tri, c > 1), band, NEG)
        (e_b, e_m), inv = _softmax_weights_t([band, _scores_t(qs, km_lo[...], km_hi[...])], carry['sk'])
        o_t = (_dot(v_tb, jnp.where(tri, e_b, 0.0).astype(BF16))
               + _dot(prev[2], jnp.where(tri, 0.0, e_b).astype(BF16))
               + _dot(vmt[...], _meta_weights(e_m))) * inv
        o_o[rows, :] = _attn_out_t(o_t, anw).astype(BF16)
        carry['prev'] = (k_lo, k_hi, v_tb)
        if j == nblk - 1:
            kl_o[...] = kp
            vl_o[...] = v

    def end():
        kp_lo[...], kp_hi[...], vpt[...] = carry['prev']

    def every():
        om_o[...] = ombuf[...].astype(BF16)
        kpm_o[...] = kmf[...]

    return meta, begin, block, end, every


def _mixer_prompt_kernel(*refs):
    n_in, n_out = N_SSD_IN + N_ATTN_IN, N_SSD_OUT + N_ATTN_OUT
    ins, outs, scratch = refs[:n_in], refs[n_in:n_in + n_out], refs[n_in + n_out:]
    s_meta, s_block, s_every = _ssd_prompt_phases(*ins[:N_SSD_IN], *outs[:N_SSD_OUT], *scratch[:N_SSD_SCRATCH])
    a_meta, a_begin, a_block, a_end, a_every = _attn_prompt_phases(
        *ins[N_SSD_IN:], *outs[N_SSD_OUT:], *scratch[N_SSD_SCRATCH:])
    c = pl.program_id(1)

    @pl.when(c == 0)
    def _():
        s_meta()
        a_meta()

    @pl.when(c > 0)
    def _():
        a_begin()
        for j in range(ins[0].shape[0] // CHUNK):
            s_block(j)
            a_block(j)
        a_end()

    s_every()
    a_every()


def _attn_sample_kernel(q_s, k_s, v_s, mk, mv, wk, wv, cos, sin, qnw, knw, snk, anw,
                        o_o, kp_o, obuf):
    nsq = mk.shape[0]
    t = SUBLANES
    nst = Q_PER_KV * t
    cs, sn = cos[...], sin[...]
    kp = _qk_prep(k_s[...], knw[...], cs, sn)
    kp_o[...] = kp
    v = v_s[...]
    qg = _q_groups(q_s[...], qnw, cs, sn)

    lo = lax.broadcasted_iota(jnp.int32, (nst, LANES), 1) < HEAD_DIM
    i_q = lax.broadcasted_iota(jnp.int32, (2 * nst, LANES), 0) & (t - 1)
    cj = lax.broadcasted_iota(jnp.int32, (2 * nst, LANES), 1)
    mask_a = cj > i_q
    mask_b = jnp.logical_or(cj <= i_q, jnp.logical_and(cj >= t, cj < t + N_META))
    sk = _sink_column(snk, t)
    zpad = jnp.zeros((WINDOW - t - N_META, LANES), F32)
    for s in range(nsq):
        rows = slice(s * t, (s + 1) * t)
        q_st = jnp.concatenate([g[rows, :] for g in qg], axis=0)
        q2 = jnp.concatenate([jnp.where(lo, q_st, 0.0), jnp.where(lo, 0.0, q_st)], axis=0).astype(BF16)
        k_b = jnp.concatenate([kp[rows, :], mk[s], zpad], axis=0).astype(BF16)
        v_b = jnp.concatenate([v[rows, :], mv[s], zpad], axis=0).astype(BF16)
        (e_a, e_b), inv = _softmax_weights(
            [jnp.where(mask_a, _dot_nt(q2, wk[s].astype(BF16)), NEG),
             jnp.where(mask_b, _dot_nt(q2, k_b), NEG)], sk)
        o2 = (_dot(e_a.astype(BF16), wv[s].astype(BF16)) + _dot(e_b.astype(BF16), v_b)) * inv
        o_st = jnp.where(lo, o2[0:nst, :], o2[nst:2 * nst, :])
        for g in range(Q_PER_KV):
            obuf[rows, g * LANES:(g + 1) * LANES] = o_st[g * t:(g + 1) * t, :]
    o_o[...] = _rms(obuf[...], anw[...]).astype(BF16)


def _mixer_prompt(proj_main, proj_tail, ssd_params, tabs_main, tabs_meta, attn_params, bsz, seq, meta_row):
    blk = _chunks_per_step(seq // CHUNK) * CHUNK
    nb = seq // blk
    mb = meta_row // N_META

    def main(col):
        return lambda b, c: (b * nb + jnp.maximum(c - 1, 0), col)

    def meta(col):
        return lambda b, c: (mb, col)

    per_batch = lambda b, c: (b, 0, 0)
    in_specs = [
        pl.BlockSpec((blk, CONV_DIM), main(COL_XBC // CONV_DIM)),
        pl.BlockSpec((blk, D_SSD), main(COL_Z // D_SSD)),
        pl.BlockSpec((blk, LANES), main(COL_DT // LANES)),
        pl.BlockSpec((N_META, CONV_DIM), meta(COL_XBC // CONV_DIM)),
        pl.BlockSpec((N_META, D_SSD), meta(COL_Z // D_SSD)),
        pl.BlockSpec((N_META, LANES), meta(COL_DT // LANES)),
    ] + _param_specs(ssd_params) + [
        pl.BlockSpec((blk, D_ATTN), main(COL_Q // D_ATTN)),
        pl.BlockSpec((blk, KV_DIM), main(COL_K // KV_DIM)),
        pl.BlockSpec((blk, KV_DIM), main(COL_V // KV_DIM)),
        pl.BlockSpec((N_META, D_ATTN), meta(COL_Q // D_ATTN)),
        pl.BlockSpec((N_META, KV_DIM), meta(COL_K // KV_DIM)),
        pl.BlockSpec((N_META, KV_DIM), meta(COL_V // KV_DIM)),
        pl.BlockSpec((blk, LANES), lambda b, c: (jnp.maximum(c - 1, 0), 0)),
        pl.BlockSpec((blk, LANES), lambda b, c: (jnp.maximum(c - 1, 0), 0)),
        pl.BlockSpec((CHUNK, LANES), lambda b, c: (0, 0)),
        pl.BlockSpec((CHUNK, LANES), lambda b, c: (0, 0)),
    ] + _param_specs(attn_params)
    assert len(in_specs) == N_SSD_IN + N_ATTN_IN
    return pl.pallas_call(
        _mixer_prompt_kernel,
        grid=(bsz, nb + 1),
        in_specs=in_specs,
        out_specs=[
            pl.BlockSpec((blk, D_SSD), main(0)),
            pl.BlockSpec((None, N_META, D_SSD), per_batch),
            pl.BlockSpec((None, N_SSD_HEADS, SSD_HEAD_DIM, D_STATE), lambda b, c: (b, 0, 0, 0)),
            pl.BlockSpec((None, SUBLANES, CONV_DIM), per_batch),
            pl.BlockSpec((blk, D_ATTN), main(0)),
            pl.BlockSpec((None, N_META, D_ATTN), per_batch),
            pl.BlockSpec((None, N_META, KV_DIM), per_batch),
            pl.BlockSpec((None, CHUNK, KV_DIM), per_batch),
            pl.BlockSpec((None, CHUNK, KV_DIM), per_batch),
        ],
        out_shape=[
            jax.ShapeDtypeStruct((bsz * seq, D_SSD), BF16),
            jax.ShapeDtypeStruct((bsz, N_META, D_SSD), BF16),
            jax.ShapeDtypeStruct((bsz, N_SSD_HEADS, SSD_HEAD_DIM, D_STATE), F32),
            jax.ShapeDtypeStruct((bsz, SUBLANES, CONV_DIM), F32),
            jax.ShapeDtypeStruct((bsz * seq, D_ATTN), BF16),
            jax.ShapeDtypeStruct((bsz, N_META, D_ATTN), BF16),
            jax.ShapeDtypeStruct((bsz, N_META, KV_DIM), F32),
            jax.ShapeDtypeStruct((bsz, CHUNK, KV_DIM), F32),
            jax.ShapeDtypeStruct((bsz, CHUNK, KV_DIM), F32),
        ],
        scratch_shapes=[
            pltpu.VMEM((SUBLANES, CONV_DIM), F32), pltpu.VMEM((N_SSD_HEADS * SSD_HEAD_DIM, D_STATE), F32),
            pltpu.VMEM((N_META, D_SSD), F32)] + [
            pltpu.VMEM((N_META, KV_DIM), BF16) for _ in range(2)] + [
            pltpu.VMEM((CHUNK, KV_DIM), BF16) for _ in range(4)] + [
            pltpu.VMEM((N_META, KV_DIM), F32), pltpu.VMEM((N_META, D_ATTN), F32),
        ],
        compiler_params=_cparams(2),
        name="mixer_prompt",
    )(proj_main, proj_main, proj_main, proj_tail, proj_tail, proj_tail, *ssd_params,
      proj_main, proj_main, proj_main, proj_tail, proj_tail, proj_tail, *tabs_main, *tabs_meta, *attn_params)


def _attn_sample(proj_tail, mk, mv, wk, wv, cos, sin, params, nseq, t, nsq):
    blk = nsq * t
    per_seq = lambda b: (b, 0, 0)
    in_specs = [
        pl.BlockSpec((blk, D_ATTN), lambda b: (b, COL_Q // D_ATTN)),
        pl.BlockSpec((blk, KV_DIM), lambda b: (b, COL_K // KV_DIM)),
        pl.BlockSpec((blk, KV_DIM), lambda b: (b, COL_V // KV_DIM)),
        pl.BlockSpec((nsq, N_META, KV_DIM), per_seq),
        pl.BlockSpec((nsq, N_META, KV_DIM), per_seq),
        pl.BlockSpec((nsq, WINDOW, KV_DIM), per_seq),
        pl.BlockSpec((nsq, WINDOW, KV_DIM), per_seq),
        pl.BlockSpec((blk, LANES), lambda b: (0, 0)),
        pl.BlockSpec((blk, LANES), lambda b: (0, 0)),
    ] + _param_specs(params)
    return pl.pallas_call(
        _attn_sample_kernel,
        grid=(nseq // nsq,),
        in_specs=in_specs,
        out_specs=[
            pl.BlockSpec((blk, D_ATTN), lambda b: (b, 0)),
            pl.BlockSpec((blk, KV_DIM), lambda b: (b, 0)),
        ],
        out_shape=[
            jax.ShapeDtypeStruct((nseq * t, D_ATTN), BF16),
            jax.ShapeDtypeStruct((nseq * t, KV_DIM), F32),
        ],
        scratch_shapes=[pltpu.VMEM((blk, D_ATTN), F32)],
        compiler_params=_cparams(1),
        name="attn_sample",
    )(proj_tail, proj_tail, proj_tail, mk, mv, wk, wv, cos, sin, *params)


def _mix_out(n_main, x_refs, ys_refs, ya_refs, wo_ref):
    return (_pick(n_main, *x_refs) + _dot(_pick(n_main, *ys_refs), wo_ref[0:D_SSD, :])
            + _dot(_pick(n_main, *ya_refs), wo_ref[D_SSD:D_SSD + D_ATTN, :]))


def _swiglu_acc(hn, wg_ref, wu_ref, wd_ref, acc_ref):
    for j in range(D_FF // FF_CHUNK):
        cols = slice(j * FF_CHUNK, (j + 1) * FF_CHUNK)
        a = (_silu(_dot(hn, wg_ref[:, cols])) * _dot(hn, wu_ref[:, cols])).astype(BF16)
        acc_ref[...] += _dot(a, wd_ref[cols, :])


def _out_ffn_kernel(n_main, xa, xb, ysa, ysb, yaa, yab, wo_ref, nw_ref, wg_ref, wu_ref, wd_ref,
                    om_ref, ot_ref):
    xm = _mix_out(n_main, (xa, xb), (ysa, ysb), (yaa, yab), wo_ref)
    om_ref[...] = xm
    _swiglu_acc(_rms(xm, nw_ref[...]).astype(BF16), wg_ref, wu_ref, wd_ref, om_ref)
    ot_ref[...] = om_ref[...]


def _out_ffn(x, ys, ya, wo, nw, wg, wu, wd, n_main):
    n_tail = x[1].shape[0] // ROW_TILE
    return pl.pallas_call(
        functools.partial(_out_ffn_kernel, n_main),
        grid=(n_main + n_tail,),
        in_specs=_src_specs(D_MODEL, n_main) + _src_specs(D_SSD, n_main) + _src_specs(D_ATTN, n_main) + [
            _resident(wo.shape), _resident(nw.shape),
            _resident(wg.shape), _resident(wu.shape), _resident(wd.shape),
        ],
        out_specs=_dst_specs(D_MODEL, n_main),
        out_shape=_dst_shapes(D_MODEL, n_main, n_tail, F32),
        compiler_params=_cparams(1),
        name="out_ffn",
    )(*x, *ys, *ya, wo, nw, wg, wu, wd)


def _out_router_kernel(n_main, n_tok, xa, xb, ysa, ysb, yaa, yab, wo_ref, nw_ref, wr_hi_ref, wr_lo_ref,
                       before_ref, xm_o, rt_o, cnt_o):
    xm = _mix_out(n_main, (xa, xb), (ysa, ysb), (yaa, yab), wo_ref)
    xm_o[...] = xm
    hn = _rms(xm, nw_ref[...])
    hi = hn.astype(BF16)
    lo = (hn - hi.astype(F32)).astype(BF16)
    logits = _dot(hi, wr_hi_ref[...]) + _dot(lo, wr_hi_ref[...]) + _dot(hi, wr_lo_ref[...])
    lane = lax.broadcasted_iota(jnp.int32, logits.shape, 1)
    logits = jnp.where(lane < N_EXPERTS, logits, -jnp.inf)
    v1 = jnp.max(logits, axis=-1, keepdims=True)
    i1 = jnp.min(jnp.where(logits == v1, lane, LANES), axis=-1, keepdims=True)
    rest = jnp.where(lane == i1, -jnp.inf, logits)
    v2 = jnp.max(rest, axis=-1, keepdims=True)
    i2 = jnp.min(jnp.where(rest == v2, lane, LANES), axis=-1, keepdims=True)
    e2 = jnp.exp(v2 - v1)
    g1 = 1.0 / (1.0 + e2)
    g2 = e2 / (1.0 + e2)

    row = pl.program_id(0) * ROW_TILE + lax.broadcasted_iota(jnp.int32, (ROW_TILE, 1), 0)
    valid = row < n_tok
    oh1 = jnp.where(jnp.logical_and(lane == i1, valid), 1.0, 0.0)
    oh2 = jnp.where(jnp.logical_and(lane == i2, valid), 1.0, 0.0)
    before = before_ref[...]
    c1 = _dot(before, oh1.astype(BF16))
    c2 = _dot(before, oh2.astype(BF16))
    tot1 = jnp.sum(oh1, axis=0, keepdims=True)
    tot2 = jnp.sum(oh2, axis=0, keepdims=True)
    rank1 = jnp.sum(jnp.where(lane == i1, c1, 0.0), axis=-1, keepdims=True)
    rank2 = jnp.sum(jnp.where(lane == i2, c2 + tot1, 0.0), axis=-1, keepdims=True)
    cnt_o[...] = jnp.broadcast_to(tot1 + tot2, cnt_o.shape)
    route = jnp.where(lane == 0, i1.astype(F32), 0.0)
    for k, val in enumerate((i2.astype(F32), g1, g2, rank1, rank2)):
        route = jnp.where(lane == k + 1, val, route)
    rt_o[...] = route


def _out_router(x, ys, ya, wo, nw, wr_hi, wr_lo, n_main, n_tok):
    n_tiles = n_main + x[1].shape[0] // ROW_TILE
    rows = n_tiles * ROW_TILE
    before = jnp.tril(jnp.ones((ROW_TILE, ROW_TILE), BF16), -1)
    return pl.pallas_call(
        functools.partial(_out_router_kernel, n_main, n_tok),
        grid=(n_tiles,),
        in_specs=_src_specs(D_MODEL, n_main) + _src_specs(D_SSD, n_main) + _src_specs(D_ATTN, n_main) + [
            _resident(wo.shape), _resident(nw.shape), _resident(wr_hi.shape), _resident(wr_lo.shape),
            _resident(before.shape),
        ],
        out_specs=[
            pl.BlockSpec((ROW_TILE, D_MODEL), lambda i: (i, 0)),
            pl.BlockSpec((ROW_TILE, LANES), lambda i: (i, 0)),
            pl.BlockSpec((None, SUBLANES, LANES), lambda i: (i, 0, 0)),
        ],
        out_shape=[
            jax.ShapeDtypeStruct((rows, D_MODEL), F32),
            jax.ShapeDtypeStruct((rows, LANES), F32),
            jax.ShapeDtypeStruct((n_tiles, SUBLANES, LANES), F32),
        ],
        compiler_params=_cparams(1),
        name="out_router",
    )(*x, *ys, *ya, wo, nw, wr_hi, wr_lo, before)


def _tile_rows(idx, n_tiles, tail, fn):
    if tail == ROW_TILE:
        fn(ROW_TILE)
    else:
        pl.when(idx < n_tiles - 1)(lambda: fn(ROW_TILE))
        pl.when(idx == n_tiles - 1)(lambda: fn(tail))


def _dispatch_kernel(n_tiles, tail, zt_ref, dest_ref, x_ref, xs_ref, zbuf, sem, zsem):
    @pl.when(pl.program_id(0) == 0)
    def _():
        zbuf[...] = jnp.zeros(zbuf.shape, F32)

        def zero_copy(j):
            start = pl.multiple_of(zt_ref[j] * ROW_TILE, ROW_TILE)
            return pltpu.make_async_copy(zbuf, xs_ref.at[pl.ds(start, ROW_TILE)], zsem)

        for j in range(zt_ref.shape[0]):
            pl.when(zt_ref[j] >= 0)(lambda j=j: zero_copy(j).start())
        for j in range(zt_ref.shape[0]):
            pl.when(zt_ref[j] >= 0)(lambda j=j: zero_copy(j).wait())

    def run(nrows):
        def body(r, carry):
            for k in range(TOP_K):
                d = dest_ref[0, 0, TOP_K * r + k]
                pltpu.make_async_copy(x_ref.at[pl.ds(r, 1)], xs_ref.at[pl.ds(d, 1)], sem).start()
            return carry

        lax.fori_loop(0, nrows, body, 0, unroll=DMA_UNROLL)
        for k in range(TOP_K):
            pltpu.make_async_copy(x_ref.at[pl.ds(0, nrows)], xs_ref.at[pl.ds(0, nrows)], sem).wait()

    _tile_rows(pl.program_id(0), n_tiles, tail, run)


def _dispatch(last_tile, dest, xm, n_tok, m_rows):
    n_tiles = xm.shape[0] // ROW_TILE
    tail = n_tok - (n_tiles - 1) * ROW_TILE
    grid_spec = pltpu.PrefetchScalarGridSpec(
        num_scalar_prefetch=1,
        grid=(n_tiles,),
        in_specs=[
            pl.BlockSpec((1, 1, TOP_K * ROW_TILE), lambda i, lt: (i, 0, 0), memory_space=pltpu.SMEM),
            pl.BlockSpec((ROW_TILE, D_MODEL), lambda i, lt: (i, 0)),
        ],
        out_specs=pl.BlockSpec(memory_space=pl.ANY),
        scratch_shapes=[pltpu.VMEM((ROW_TILE, D_MODEL), F32), pltpu.SemaphoreType.DMA(()),
                        pltpu.SemaphoreType.DMA(())],
    )
    return pl.pallas_call(
        functools.partial(_dispatch_kernel, n_tiles, tail),
        grid_spec=grid_spec,
        out_shape=jax.ShapeDtypeStruct((m_rows, D_MODEL), F32),
        compiler_params=_cparams(1, has_side_effects=True),
        name="moe_dispatch",
    )(last_tile, dest, xm)


def _moe_kernel(te_ref, nu_ref, x_ref, nw_ref, wg_ref, wu_ref, wd_ref, o_ref):
    i = pl.program_id(0)
    o_ref[...] = jnp.zeros(o_ref.shape, F32)

    @pl.when(i < nu_ref[0])
    def _():
        _swiglu_acc(_rms(x_ref[...], nw_ref[...]).astype(BF16), wg_ref, wu_ref, wd_ref, o_ref)


def _moe_experts(tile_e, n_used, xs, nw, wg, wu, wd):
    rows = xs.shape[0]
    grid_spec = pltpu.PrefetchScalarGridSpec(
        num_scalar_prefetch=2,
        grid=(rows // ROW_TILE,),
        in_specs=[
            pl.BlockSpec((ROW_TILE, D_MODEL), lambda i, te, nu: (jnp.maximum(jnp.minimum(i, nu[0] - 1), 0), 0)),
            pl.BlockSpec((1, D_MODEL), lambda i, te, nu: (0, 0)),
            pl.BlockSpec((None, D_MODEL, D_FF), lambda i, te, nu: (te[i], 0, 0)),
            pl.BlockSpec((None, D_MODEL, D_FF), lambda i, te, nu: (te[i], 0, 0)),
            pl.BlockSpec((None, D_FF, D_MODEL), lambda i, te, nu: (te[i], 0, 0)),
        ],
        out_specs=pl.BlockSpec((ROW_TILE, D_MODEL), lambda i, te, nu: (i, 0)),
    )
    return pl.pallas_call(
        _moe_kernel,
        grid_spec=grid_spec,
        out_shape=jax.ShapeDtypeStruct((rows, D_MODEL), F32),
        compiler_params=_cparams(1),
        name="moe_experts",
    )(tile_e, n_used, xs, nw, wg, wu, wd)


def _combine_kernel(n_main, n_tiles, tail, dcur_ref, dnext_ref, xm_ref, rt_ref, yb_ref,
                    om_ref, ot_ref, gbuf, sem):
    i = pl.program_id(0)

    def issue(dref, slot, nrows):
        def body(r, carry):
            for k in range(TOP_K):
                d = dref[0, 0, TOP_K * r + k]
                pltpu.make_async_copy(yb_ref.at[pl.ds(d, 1)], gbuf.at[slot, k, pl.ds(r, 1)],
                                      sem.at[slot]).start()
            return carry

        lax.fori_loop(0, nrows, body, 0, unroll=DMA_UNROLL)

    def wait(slot, nrows):
        for k in range(TOP_K):
            pltpu.make_async_copy(yb_ref.at[pl.ds(0, nrows)], gbuf.at[slot, k, pl.ds(0, nrows)],
                                  sem.at[slot]).wait()

    @pl.when(i == 0)
    def _():
        gbuf[...] = jnp.zeros(gbuf.shape, F32)
        _tile_rows(i, n_tiles, tail, lambda n: issue(dcur_ref, 0, n))

    for slot in range(2):
        @pl.when(jnp.logical_and(i + 1 < n_tiles, (i + 1) % 2 == slot))
        def _(slot=slot):
            _tile_rows(i + 1, n_tiles, tail, lambda n: issue(dnext_ref, slot, n))

    for slot in range(2):
        @pl.when(i % 2 == slot)
        def _(slot=slot):
            _tile_rows(i, n_tiles, tail, lambda n: wait(slot, n))
            gates = rt_ref[...]
            val = (xm_ref[...] + gates[:, TOP_K:TOP_K + 1] * gbuf[slot, 0]
                   + gates[:, TOP_K + 1:TOP_K + 2] * gbuf[slot, 1])

            @pl.when(i < n_main)
            def _():
                om_ref[...] = val

            @pl.when(i >= n_main)
            def _():
                ot_ref[...] = val


def _combine(dest, xm, route, yb, n_main, n_tok):
    n_tiles = xm.shape[0] // ROW_TILE
    tail = n_tok - (n_tiles - 1) * ROW_TILE
    dspec = lambda f: pl.BlockSpec((1, 1, TOP_K * ROW_TILE), f, memory_space=pltpu.SMEM)
    return pl.pallas_call(
        functools.partial(_combine_kernel, n_main, n_tiles, tail),
        grid=(n_tiles,),
        in_specs=[
            dspec(lambda i: (i, 0, 0)),
            dspec(lambda i: (jnp.minimum(i + 1, n_tiles - 1), 0, 0)),
            pl.BlockSpec((ROW_TILE, D_MODEL), lambda i: (i, 0)),
            pl.BlockSpec((ROW_TILE, LANES), lambda i: (i, 0)),
            pl.BlockSpec(memory_space=pl.ANY),
        ],
        out_specs=_src_specs(D_MODEL, n_main),
        out_shape=[jax.ShapeDtypeStruct((n_main * ROW_TILE, D_MODEL), F32),
                   jax.ShapeDtypeStruct(((n_tiles - n_main) * ROW_TILE, D_MODEL), F32)],
        scratch_shapes=[pltpu.VMEM((2, TOP_K, ROW_TILE, D_MODEL), F32), pltpu.SemaphoreType.DMA((2,))],
        compiler_params=_cparams(1),
        name="moe_combine",
    )(dest, dest, xm, route, yb)


def _moe_layer(xm, route, counts, n_main, n_tok, nw, wg, wu, wd):
    n_tiles = xm.shape[0] // ROW_TILE
    m_tiles = -(-(n_tok * TOP_K + N_EXPERTS * (ROW_TILE - 1)) // ROW_TILE)
    cnt = counts[:, 0, :N_EXPERTS].astype(jnp.int32)
    total = jnp.sum(cnt, axis=0)
    padded = (total + ROW_TILE - 1) // ROW_TILE * ROW_TILE
    pad_end = jnp.cumsum(padded)
    base = (pad_end - padded)[None, :] + jnp.cumsum(cnt, axis=0) - cnt
    e = route[:, 0:TOP_K].astype(jnp.int32).reshape(n_tiles, ROW_TILE, TOP_K)
    rank = route[:, 2 * TOP_K:3 * TOP_K].astype(jnp.int32).reshape(n_tiles, ROW_TILE, TOP_K)
    onehot = e[..., None] == jnp.arange(N_EXPERTS, dtype=jnp.int32)
    dest = jnp.sum(jnp.where(onehot, base[:, None, None, :], 0), axis=-1) + rank
    dest = dest.reshape(n_tiles, 1, ROW_TILE * TOP_K)
    tile_start = jnp.arange(m_tiles, dtype=jnp.int32) * ROW_TILE
    tile_e = jnp.minimum(jnp.sum((pad_end[None, :] <= tile_start[:, None]).astype(jnp.int32), axis=1),
                         N_EXPERTS - 1)
    n_used = (pad_end[-1:] // ROW_TILE).astype(jnp.int32)
    last_tile = jnp.where(padded > 0, pad_end // ROW_TILE - 1, -1)
    spare = n_used[0] + jnp.arange(m_tiles - (n_tok * TOP_K) // ROW_TILE, dtype=jnp.int32)
    zero_tiles = jnp.concatenate([last_tile, jnp.where(spare < m_tiles, spare, -1)]).astype(jnp.int32)
    xs = _dispatch(zero_tiles, dest, xm, n_tok, m_tiles * ROW_TILE)
    yb = _moe_experts(tile_e, n_used, xs, nw, wg, wu, wd)
    return _combine(dest, xm, route, yb, n_main, n_tok)


def _rope_tables(pos):
    half = HEAD_DIM // 2
    inv_freq = ROPE_THETA ** (-jnp.arange(half, dtype=F32) / half)
    ang = pos.astype(F32)[:, None] * inv_freq[None, :]
    cos = jnp.cos(ang)
    sin = jnp.sin(ang)
    reps = LANES // HEAD_DIM
    return (jnp.tile(jnp.concatenate([cos, cos], axis=-1), (1, reps)),
            jnp.tile(jnp.concatenate([-sin, sin], axis=-1), (1, reps)))


def _pad_lanes(v, width=LANES):
    v = v.astype(F32).reshape(1, -1)
    return jnp.pad(v, ((0, 0), (0, width - v.shape[1])))


def kernel(x_prompt, x_sample, state_ssm, state_conv, cache_meta_k, cache_meta_v, cache_win_k, cache_win_v, meta_tokens, norm_mix_w, w_in, conv_w, conv_b, dt_bias, a_log, d_skip, ssd_norm_w, q_norm_w, k_norm_w, sinks, attn_norm_w, w_out, norm_ffn_w, w_gate, w_up, w_down, w_router, moe_w_gate, moe_w_up, moe_w_down):
    bsz, seq, _ = x_prompt.shape
    nseq, t_s, _ = x_sample.shape
    depth = w_in.shape[0]
    r_main = bsz * seq
    r_samp = nseq * t_s
    assert seq % CHUNK == 0 and t_s == SUBLANES and r_main % ROW_TILE == 0 and r_samp % N_META == 0
    n_main = r_main // ROW_TILE
    n_tok = r_main + r_samp + N_META
    r_tail = -(-(r_samp + N_META) // ROW_TILE) * ROW_TILE
    tail_pad = r_tail - r_samp - N_META

    x = (x_prompt.reshape(r_main, D_MODEL),
         jnp.concatenate([x_sample.reshape(r_samp, D_MODEL), meta_tokens.astype(F32),
                          jnp.zeros((tail_pad, D_MODEL), F32)], axis=0))

    tabs_main = _rope_tables(N_META + jnp.arange(seq, dtype=jnp.int32))
    tabs_meta = _rope_tables(jnp.arange(CHUNK, dtype=jnp.int32) - (CHUNK - N_META))
    nsq = 16 if nseq % 16 == 0 else nseq
    cos_s, sin_s = (jnp.tile(tab, (nsq, 1)) for tab in
                    _rope_tables(PAST_LEN + jnp.arange(t_s, dtype=jnp.int32)))

    o_z, o_xbc, o_dt, o_q, o_k, o_v = 0, 512, 1536, 1544, 2056, 2184
    col = jnp.arange(D_ATTN, dtype=jnp.int32)
    grp, lane = col // LANES, col % LANES
    head_perm = (grp + Q_PER_KV * (lane // HEAD_DIM)) * HEAD_DIM + lane % HEAD_DIM

    def heads4(a):
        return a.reshape(a.shape[0], a.shape[1], N_KV_HEADS, HEAD_DIM)

    outs = {k: [] for k in ('p_ssm', 'p_conv', 'p_mk', 'p_mv', 'p_wk', 'p_wv', 's_ssm', 's_conv', 's_wk', 's_wv')}
    for l in range(depth):
        wl = w_in[l]
        w_re = jnp.concatenate([
            wl[:, o_xbc:o_xbc + CONV_DIM], wl[:, o_z:o_z + D_SSD], wl[:, o_q:o_q + D_ATTN][:, head_perm],
            wl[:, o_k:o_k + KV_DIM], wl[:, o_v:o_v + KV_DIM], wl[:, o_dt:o_dt + N_SSD_HEADS],
            jnp.zeros((D_MODEL, PROJ_W - COL_DT - N_SSD_HEADS), wl.dtype)], axis=1).astype(BF16)
        proj_main, proj_tail = _in_proj(x[0], x[1], norm_mix_w[l].reshape(1, D_MODEL).astype(F32), w_re, n_main)

        ssd_params = (conv_w[l].astype(F32), conv_b[l].reshape(1, CONV_DIM).astype(F32),
                      _pad_lanes(dt_bias[l]), _pad_lanes(-jnp.exp(a_log[l].astype(F32))),
                      jnp.repeat(d_skip[l].astype(F32), SSD_HEAD_DIM).reshape(1, D_SSD),
                      ssd_norm_w[l].reshape(1, D_SSD).astype(F32))
        attn_params = (jnp.tile(q_norm_w[l].astype(F32), N_Q_HEADS).reshape(1, D_ATTN),
                       jnp.tile(k_norm_w[l].astype(F32), N_KV_HEADS).reshape(1, KV_DIM),
                       _pad_lanes(sinks[l]), attn_norm_w[l][head_perm].reshape(1, D_ATTN).astype(F32))
        ys_p, ys_m, ssm_p, conv_p, ya_p, ya_m, kp_m, k_last, v_last = _mixer_prompt(
            proj_main, proj_tail, ssd_params, tabs_main, tabs_meta, attn_params, bsz, seq, r_samp)
        conv_prev = jnp.pad(state_conv[l].astype(F32), ((0, 0), (SUBLANES - (CONV_W - 1), 0), (0, 0)))
        ys_s, ssm_s = _ssd_sample(proj_tail, conv_prev, state_ssm[l].astype(F32), ssd_params, nseq, t_s, nsq)
        ya_s, kp_s = _attn_sample(
            proj_tail, cache_meta_k[l].reshape(nseq, N_META, KV_DIM).astype(F32),
            cache_meta_v[l].reshape(nseq, N_META, KV_DIM).astype(F32),
            cache_win_k[l].reshape(nseq, WINDOW, KV_DIM).astype(F32),
            cache_win_v[l].reshape(nseq, WINDOW, KV_DIM).astype(F32),
            cos_s, sin_s, attn_params, nseq, t_s, nsq)

        ys = (ys_p, jnp.concatenate([ys_s, ys_m[0], jnp.zeros((tail_pad, D_SSD), BF16)], axis=0))
        ya = (ya_p, jnp.concatenate([ya_s, ya_m[0], jnp.zeros((tail_pad, D_ATTN), BF16)], axis=0))

        wo = jnp.concatenate([w_out[l][:D_SSD], w_out[l][D_SSD:][head_perm]], axis=0).astype(BF16)
        nfw = norm_ffn_w[l].reshape(1, D_MODEL).astype(F32)
        i = l // 2
        if l % 2 == 0:
            x = _out_ffn(x, ys, ya, wo, nfw, w_gate[i].astype(BF16), w_up[i].astype(BF16),
                         w_down[i].astype(BF16), n_main)
        else:
            wr = jnp.pad(w_router[i].astype(F32), ((0, 0), (0, LANES - N_EXPERTS)))
            wr_hi = wr.astype(BF16)
            wr_lo = (wr - wr_hi.astype(F32)).astype(BF16)
            xm, route, counts = _out_router(x, ys, ya, wo, nfw, wr_hi, wr_lo, n_main, n_tok)
            x = _moe_layer(xm, route, counts, n_main, n_tok, nfw, moe_w_gate[i].astype(BF16),
                           moe_w_up[i].astype(BF16), moe_w_down[i].astype(BF16))

        samp = proj_tail[:r_samp]
        xbc_s = samp[:, COL_XBC:COL_XBC + CONV_DIM].reshape(nseq, t_s, CONV_DIM)
        v_s = samp[:, COL_V:COL_V + KV_DIM].reshape(nseq, t_s, KV_DIM)
        v_meta = proj_tail[r_samp:r_samp + N_META, COL_V:COL_V + KV_DIM]
        meta_shape = (bsz, N_META, N_KV_HEADS, HEAD_DIM)
        outs['p_ssm'].append(ssm_p)
        outs['p_conv'].append(conv_p[:, SUBLANES - (CONV_W - 1):])
        outs['p_mk'].append(jnp.broadcast_to(heads4(kp_m[0:1]), meta_shape))
        outs['p_mv'].append(jnp.broadcast_to(heads4(v_meta[None]), meta_shape))
        outs['p_wk'].append(heads4(k_last))
        outs['p_wv'].append(heads4(v_last))
        outs['s_ssm'].append(ssm_s)
        outs['s_conv'].append(jnp.concatenate([state_conv[l].astype(F32), xbc_s], axis=1)[:, t_s:])
        outs['s_wk'].append(jnp.concatenate([cache_win_k[l].astype(F32),
                                             heads4(kp_s.reshape(nseq, t_s, KV_DIM))], axis=1)[:, t_s:])
        outs['s_wv'].append(jnp.concatenate([cache_win_v[l].astype(F32), heads4(v_s)], axis=1)[:, t_s:])

    y_prompt = x[0][:r_main].reshape(bsz, seq, D_MODEL)
    y_sample = x[1][:r_samp].reshape(nseq, t_s, D_MODEL)
    st = lambda k: jnp.stack(outs[k])
    return (y_prompt, y_sample, st('p_ssm'), st('p_conv'), st('p_mk'), st('p_mv'), st('p_wk'), st('p_wv'),
            st('s_ssm'), st('s_conv'), st('s_wk'), st('s_wv'))
```

```python
import functools

import jax
import jax.numpy as jnp
from jax import lax
from jax.experimental import pallas as pl
from jax.experimental.pallas import tpu as pltpu

F32 = jnp.float32
BF16 = jnp.bfloat16

D_MODEL = 1024
D_SSD = 512
SSD_HEAD_DIM = 64
N_SSD_HEADS = 8
SSD_HEADS_PER_GROUP = 4
N_SSD_GROUPS = 2
D_STATE = 128
CONV_W = 4
CONV_DIM = 1024
D_ATTN = 512
HEAD_DIM = 64
N_Q_HEADS = 8
N_KV_HEADS = 2
Q_PER_KV = 4
KV_DIM = 128
WINDOW = 128
N_META = 16
D_FF = 2816
N_EXPERTS = 8
TOP_K = 2
EPS = 1e-6
NEG = -1e30
ATTN_SCALE = HEAD_DIM ** -0.5
PAST_LEN = 16384
ROPE_THETA = 10000.0

LANES = 128
SUBLANES = 8
CHUNK = 128
ROW_TILE = 512
FF_CHUNK = 256
DMA_UNROLL = 8
VMEM_LIMIT = 60 * 1024 * 1024

COL_XBC = 0
COL_Z = 1024
COL_Q = 1536
COL_K = 2048
COL_V = 2176
COL_DT = 2304
PROJ_W = 2432
PROJ_STEP = 512


def _dot(a, b):
    return jnp.dot(a, b, preferred_element_type=F32)


def _dot_nt(a, b):
    return lax.dot_general(a, b, (((1,), (1,)), ((), ())), preferred_element_type=F32)


def _dot_tn(a, b):
    return lax.dot_general(a, b, (((0,), (0,)), ((), ())), preferred_element_type=F32)


def _rms(x, w):
    return x * lax.rsqrt(jnp.mean(x * x, axis=-1, keepdims=True) + EPS) * w


def _silu(x):
    return x * jax.nn.sigmoid(x)


def _split3(x):
    p1 = x.astype(BF16)
    r1 = x - p1.astype(F32)
    p2 = r1.astype(BF16)
    p3 = (r1 - p2.astype(F32)).astype(BF16)
    return p1, p2, p3


def _dot_exact(sel, x):
    p1, p2, p3 = _split3(x)
    return _dot(sel, p1) + _dot(sel, p2) + _dot(sel, p3)


def _cparams(ndim, **kw):
    return pltpu.CompilerParams(dimension_semantics=("arbitrary",) * ndim, vmem_limit_bytes=VMEM_LIMIT, **kw)


def _src_specs(width, n_main):
    return [pl.BlockSpec((ROW_TILE, width), lambda i: (jnp.minimum(i, n_main - 1), 0)),
            pl.BlockSpec((ROW_TILE, width), lambda i: (jnp.maximum(i - n_main, 0), 0))]


def _pick(n_main, main_ref, tail_ref):
    dtype = main_ref.dtype
    picked = jnp.where(pl.program_id(0) < n_main, main_ref[...].astype(F32), tail_ref[...].astype(F32))
    return picked.astype(dtype)


def _dst_specs(width, n_main):
    return [pl.BlockSpec((ROW_TILE, width), lambda i: (jnp.minimum(i, n_main), 0)),
            pl.BlockSpec((ROW_TILE, width), lambda i: (jnp.maximum(i - n_main, 0), 0))]


def _dst_shapes(width, n_main, n_tail, dtype):
    return [jax.ShapeDtypeStruct(((n_main + 1) * ROW_TILE, width), dtype),
            jax.ShapeDtypeStruct((n_tail * ROW_TILE, width), dtype)]


def _resident(shape):
    nd = len(shape)
    return pl.BlockSpec(shape, lambda *a: (0,) * nd, pipeline_mode=pl.Buffered(1))


def _param_specs(params):
    return [pl.BlockSpec(p.shape, lambda *a: (0, 0)) for p in params]


def _in_proj_kernel(n_main, xa_ref, xb_ref, nw_ref, w_ref, om_ref, ot_ref):
    xn = _rms(_pick(n_main, xa_ref, xb_ref), nw_ref[...]).astype(BF16)
    for c0 in range(0, PROJ_W, PROJ_STEP):
        cols = slice(c0, min(c0 + PROJ_STEP, PROJ_W))
        r = _dot(xn, w_ref[:, cols])
        om_ref[:, cols] = r
        ot_ref[:, cols] = r


def _in_proj(x_main, x_tail, nw, w, n_main):
    n_tail = x_tail.shape[0] // ROW_TILE
    return pl.pallas_call(
        functools.partial(_in_proj_kernel, n_main),
        grid=(n_main + n_tail,),
        in_specs=_src_specs(D_MODEL, n_main) + [
            pl.BlockSpec((1, D_MODEL), lambda i: (0, 0)),
            pl.BlockSpec((D_MODEL, PROJ_W), lambda i: (0, 0)),
        ],
        out_specs=_dst_specs(PROJ_W, n_main),
        out_shape=_dst_shapes(PROJ_W, n_main, n_tail, F32),
        compiler_params=_cparams(1),
        name="in_proj",
    )(x_main, x_tail, nw, w)


def _ssd_chunk(xbc, z, dt_raw, valid, cw_ref, cb_ref, dtb_ref, an_ref, dsk_ref, nw_ref, cbuf, hst):
    q = CHUNK
    ext = jnp.concatenate([cbuf[...], xbc], axis=0)
    cbuf[...] = xbc[q - SUBLANES:q, :]
    acc = cb_ref[...] + xbc * cw_ref[CONV_W - 1:CONV_W, :]
    for s in range(1, CONV_W):
        acc = acc + pltpu.roll(ext, s, 0)[SUBLANES:, :] * cw_ref[CONV_W - 1 - s:CONV_W - s, :]
    xc = _silu(acc)
    xs = xc[:, :D_SSD]
    bm = xc[:, D_SSD:D_SSD + N_SSD_GROUPS * D_STATE]
    cm = xc[:, D_SSD + N_SSD_GROUPS * D_STATE:]

    dt = jax.nn.softplus(dt_raw + dtb_ref[...])
    if valid is not None:
        dt = jnp.where(valid, dt, 0.0)
    da = dt * an_ref[...]
    row_i = lax.broadcasted_iota(jnp.int32, (q, q), 0)
    col_j = lax.broadcasted_iota(jnp.int32, (q, q), 1)
    tril = row_i >= col_j
    cs = _dot_exact(jnp.where(tril, 1.0, 0.0).astype(BF16), da)
    cs_t = cs.T
    last = cs[q - 1:q, :]
    ecl = jnp.exp(last)
    xdt = xs * _expand_heads(dt)
    xw = xdt * _expand_heads(jnp.exp(last - cs))
    ecs_x = _expand_heads(jnp.exp(cs))
    skip = xs * dsk_ref[...]

    gw = SSD_HEADS_PER_GROUP * SSD_HEAD_DIM
    head_of_col = jnp.right_shift(lax.broadcasted_iota(jnp.int32, (q, gw), 1), SSD_HEAD_DIM.bit_length() - 1)
    ys = []
    for g in range(N_SSD_GROUPS):
        bg = bm[:, g * D_STATE:(g + 1) * D_STATE].astype(BF16)
        cg = cm[:, g * D_STATE:(g + 1) * D_STATE].astype(BF16)
        cb = _dot_nt(cg, bg)
        gc = slice(g * gw, (g + 1) * gw)
        h_prev = hst[gc, :]
        y_g = _dot_nt(cg, h_prev.astype(BF16)) * ecs_x[:, gc] + skip[:, gc]
        xdt_g = xdt[:, gc]
        for r in range(SSD_HEADS_PER_GROUP):
            h = g * SSD_HEADS_PER_GROUP + r
            seg = cs[:, h:h + 1] - cs_t[h:h + 1, :]
            decay = jnp.where(tril, jnp.exp(jnp.where(tril, seg, 0.0)), 0.0)
            y_g = y_g + _dot((decay * cb).astype(BF16), jnp.where(head_of_col == r, xdt_g, 0.0).astype(BF16))
        ys.append(y_g)
        ecl_col = jnp.concatenate(
            [jnp.broadcast_to(ecl[:, h:h + 1], (SSD_HEAD_DIM, 1))
             for h in range(g * SSD_HEADS_PER_GROUP, (g + 1) * SSD_HEADS_PER_GROUP)], axis=0)
        hst[gc, :] = h_prev * ecl_col + _dot_tn(xw[:, gc].astype(BF16), bg)

    return _ssd_gate_norm(jnp.concatenate(ys, axis=1), z, nw_ref)


def _ssd_gate_norm(y, z, nw_ref):
    y = y * _silu(z)
    gs = D_SSD // N_SSD_GROUPS
    return jnp.concatenate([_rms(y[:, g * gs:(g + 1) * gs], nw_ref[:, g * gs:(g + 1) * gs])
                            for g in range(N_SSD_GROUPS)], axis=-1)


N_SSD_IN, N_SSD_OUT, N_SSD_SCRATCH = 12, 4, 3


def _ssd_prompt_phases(xbc_m, z_m, dt_m, xbc_t, z_t, dt_t, cw, cb, dtb, an, dsk, nw,
                       y_o, ym_o, hf_o, ct_o, cbuf, hst, ymbuf):
    pad = CHUNK - N_META
    prm = (cw, cb, dtb, an, dsk, nw)

    def meta():
        cbuf[...] = jnp.zeros((SUBLANES, CONV_DIM), F32)
        hst[...] = jnp.zeros(hst.shape, F32)

        def stage(meta_ref):
            return jnp.concatenate([jnp.zeros((pad, meta_ref.shape[1]), F32), meta_ref[...]], axis=0)

        row = lax.broadcasted_iota(jnp.int32, (CHUNK, 1), 0)
        y = _ssd_chunk(stage(xbc_t), stage(z_t), stage(dt_t), row >= pad, *prm, cbuf, hst)
        ymbuf[...] = y[pad:, :]

    def block(j):
        rows = slice(j * CHUNK, (j + 1) * CHUNK)
        y = _ssd_chunk(xbc_m[rows, :], z_m[rows, :], dt_m[rows, :], None, *prm, cbuf, hst)
        y_o[rows, :] = y.astype(BF16)

    def every():
        ym_o[...] = ymbuf[...].astype(BF16)
        hf_o[...] = hst[...].reshape(hf_o.shape)
        ct_o[...] = cbuf[...]

    return meta, block, every


def _chunks_per_step(nc):
    return 4 if nc % 4 == 0 else (2 if nc % 2 == 0 else 1)


def _expand_heads(a):
    hh = lax.broadcasted_iota(jnp.int32, (LANES, D_SSD), 0)
    cc = lax.broadcasted_iota(jnp.int32, (LANES, D_SSD), 1)
    sel = jnp.where(jnp.right_shift(cc, SSD_HEAD_DIM.bit_length() - 1) == hh, 1.0, 0.0).astype(BF16)
    p1, p2, p3 = _split3(a)
    return _dot(p1, sel) + _dot(p2, sel) + _dot(p3, sel)


def _pad_rows_bf16(x, rows):
    return jnp.concatenate([x, jnp.zeros((rows - x.shape[0], x.shape[1]), F32)], axis=0).astype(BF16)


def _ssd_sample_kernel(xbc_s, z_s, dt_s, cprev, h0, cw, cb, dtb, an, dsk, nw,
                       y_o, hf_o, ubuf, ybuf):
    nsq = cprev.shape[0]
    t = SUBLANES
    rows = nsq * t
    x = xbc_s[...]
    for s in range(nsq):
        ubuf[2 * t * s:2 * t * s + t, :] = cprev[s]
        ubuf[2 * t * s + t:2 * t * (s + 1), :] = x[s * t:(s + 1) * t, :]
    parts = []
    for s in range(nsq):
        acc = cb[...]
        for k in range(CONV_W):
            off = 2 * t * s + t - (CONV_W - 1) + k
            acc = acc + ubuf[off:off + t, :] * cw[k:k + 1, :]
        parts.append(acc)
    xc = _silu(jnp.concatenate(parts, axis=0))
    xs = xc[:, :D_SSD]
    bm = xc[:, D_SSD:D_SSD + N_SSD_GROUPS * D_STATE]
    cm = xc[:, D_SSD + N_SSD_GROUPS * D_STATE:]

    dt = jax.nn.softplus(dt_s[...] + dtb[...])
    da = dt * an[...]
    ri = lax.broadcasted_iota(jnp.int32, (rows, rows), 0)
    cj = lax.broadcasted_iota(jnp.int32, (rows, rows), 1)
    mask = jnp.logical_and(ri >= cj, jnp.right_shift(ri, 3) == jnp.right_shift(cj, 3))
    cs = _dot_exact(jnp.where(mask, 1.0, 0.0).astype(BF16), da)
    last = _dot_exact(jnp.where(cj == jnp.bitwise_or(ri, t - 1), 1.0, 0.0).astype(BF16), cs)
    cs_t = cs.T
    ecl = jnp.exp(last)
    xdt = xs * _expand_heads(dt)
    xw = xdt * _expand_heads(jnp.exp(last - cs))
    ecs_x = _expand_heads(jnp.exp(cs))
    skip = xs * dsk[...]

    gw = SSD_HEADS_PER_GROUP * SSD_HEAD_DIM
    for g in range(N_SSD_GROUPS):
        bg_f = bm[:, g * D_STATE:(g + 1) * D_STATE]
        cg_f = cm[:, g * D_STATE:(g + 1) * D_STATE]
        cbm = _dot_nt(cg_f.astype(BF16), bg_f.astype(BF16))
        for r in range(SSD_HEADS_PER_GROUP):
            h = g * SSD_HEADS_PER_GROUP + r
            hc = slice(h * SSD_HEAD_DIM, (h + 1) * SSD_HEAD_DIM)
            seg = cs[:, h:h + 1] - cs_t[h:h + 1, :]
            decay = jnp.where(mask, jnp.exp(jnp.where(mask, seg, 0.0)), 0.0)
            ybuf[:, hc] = _dot((decay * cbm).astype(BF16), xdt[:, hc].astype(BF16)) + skip[:, hc]
        gc = slice(g * gw, (g + 1) * gw)
        heads = slice(g * SSD_HEADS_PER_GROUP, (g + 1) * SSD_HEADS_PER_GROUP)
        for s in range(nsq):
            rs = slice(s * t, (s + 1) * t)
            hg = h0[s, heads].reshape(gw, D_STATE)
            y_off = _dot_nt(_pad_rows_bf16(cg_f[rs, :], 2 * t), hg.astype(BF16))[0:t, :]
            ybuf[rs, gc] = ybuf[rs, gc] + y_off * ecs_x[rs, gc]
            ecl_col = jnp.concatenate(
                [jnp.broadcast_to(ecl[s * t:s * t + 1, h:h + 1], (SSD_HEAD_DIM, 1))
                 for h in range(heads.start, heads.stop)], axis=0)
            h_new = hg * ecl_col + _dot_tn(_pad_rows_bf16(xw[rs, gc], 2 * t), _pad_rows_bf16(bg_f[rs, :], 2 * t))
            hf_o[s, heads] = h_new.reshape(SSD_HEADS_PER_GROUP, SSD_HEAD_DIM, D_STATE)

    y_o[...] = _ssd_gate_norm(ybuf[...], z_s[...], nw).astype(BF16)


def _ssd_sample(proj_tail, conv_prev, h0, layer, params, nseq, t, nsq):
    blk = nsq * t
    in_specs = [
        pl.BlockSpec((blk, CONV_DIM), lambda b: (b, COL_XBC // CONV_DIM)),
        pl.BlockSpec((blk, D_SSD), lambda b: (b, COL_Z // D_SSD)),
        pl.BlockSpec((blk, LANES), lambda b: (b, COL_DT // LANES)),
        pl.BlockSpec((nsq, SUBLANES, CONV_DIM), lambda b: (b, 0, 0)),
        pl.BlockSpec((None, nsq, N_SSD_HEADS, SSD_HEAD_DIM, D_STATE), lambda b: (layer, b, 0, 0, 0)),
    ] + _param_specs(params)
    return pl.pallas_call(
        _ssd_sample_kernel,
        grid=(nseq // nsq,),
        in_specs=in_specs,
        out_specs=[
            pl.BlockSpec((blk, D_SSD), lambda b: (b, 0)),
            pl.BlockSpec((nsq, N_SSD_HEADS, SSD_HEAD_DIM, D_STATE), lambda b: (b, 0, 0, 0)),
        ],
        out_shape=[
            jax.ShapeDtypeStruct((nseq * t, D_SSD), BF16),
            jax.ShapeDtypeStruct((nseq, N_SSD_HEADS, SSD_HEAD_DIM, D_STATE), F32),
        ],
        scratch_shapes=[pltpu.VMEM((2 * blk, CONV_DIM), F32), pltpu.VMEM((blk, D_SSD), F32)],
        compiler_params=_cparams(1),
        name="ssd_sample",
    )(proj_tail, proj_tail, proj_tail, conv_prev, h0, *params)


def _qk_prep(x, w, cos, sin):
    lane = lax.broadcasted_iota(jnp.int32, x.shape, 1)
    lo_head = lane < HEAD_DIM
    sq = x * x
    s_lo = jnp.sum(jnp.where(lo_head, sq, 0.0), axis=-1, keepdims=True)
    s_all = jnp.sum(sq, axis=-1, keepdims=True)
    ms = jnp.where(lo_head, s_lo, s_all - s_lo) * (1.0 / HEAD_DIM)
    xn = x * lax.rsqrt(ms + EPS) * w
    half = HEAD_DIM // 2
    first_half = (lane % HEAD_DIM) < half
    partner = jnp.where(first_half, pltpu.roll(xn, LANES - half, 1), pltpu.roll(xn, half, 1))
    return xn * cos + partner * sin


def _q_groups(q, qnw_ref, cos, sin):
    return [_qk_prep(q[:, g * LANES:(g + 1) * LANES], qnw_ref[:, g * LANES:(g + 1) * LANES], cos, sin)
            * ATTN_SCALE for g in range(D_ATTN // LANES)]


def _sink_column(snk_ref, rows):
    return jnp.concatenate([jnp.broadcast_to(snk_ref[:, h:h + 1], (rows, 1)) for h in range(N_Q_HEADS)],
                           axis=0)


def _softmax_weights(pieces, sk):
    top = pieces[0]
    for p in pieces[1:]:
        top = jnp.maximum(top, p)
    m = jnp.maximum(jnp.max(top, axis=-1, keepdims=True), sk)
    e = [jnp.exp(p - m) for p in pieces]
    tot = e[0]
    for p in e[1:]:
        tot = tot + p
    return e, 1.0 / (jnp.sum(tot, axis=-1, keepdims=True) + jnp.exp(sk - m))


def _kv_prep_t(k, v, knw_ref, cos, sin):
    kp = _qk_prep(k, knw_ref[...], cos, sin)
    lane = lax.broadcasted_iota(jnp.int32, kp.shape, 1)
    return (kp, jnp.where(lane < HEAD_DIM, kp, 0.0).astype(BF16),
            jnp.where(lane < HEAD_DIM, 0.0, kp).astype(BF16), v.T.astype(BF16))


def _scores_t(qs, key_lo, key_hi):
    return jnp.concatenate([_dot_nt(key_lo, qs), _dot_nt(key_hi, qs)], axis=1)


def _softmax_weights_t(pieces, sk):
    m = sk
    for p in pieces:
        m = jnp.maximum(m, jnp.max(p, axis=0, keepdims=True))
    e = [jnp.exp(p - m) for p in pieces]
    den = jnp.exp(sk - m)
    for p in e:
        den = den + jnp.sum(p, axis=0, keepdims=True)
    return e, 1.0 / den


def _meta_weights(e_m):
    return jnp.concatenate([e_m.astype(BF16), jnp.zeros((CHUNK - N_META, e_m.shape[1]), BF16)], axis=0)


def _attn_out_t(o_t, anw_ref):
    nst = Q_PER_KV * CHUNK
    d = lax.broadcasted_iota(jnp.int32, (LANES, CHUNK), 0)
    groups = [jnp.where(d < HEAD_DIM, o_t[:, g * CHUNK:(g + 1) * CHUNK],
                        o_t[:, nst + g * CHUNK:nst + (g + 1) * CHUNK]).T for g in range(Q_PER_KV)]
    return _rms(jnp.concatenate(groups, axis=1), anw_ref[...])


N_ATTN_IN, N_ATTN_OUT, N_ATTN_SCRATCH = 14, 5, 8


def _attn_prompt_phases(q_m, k_m, v_m, q_t, k_t, v_t, cos_m, sin_m, cos_t, sin_t, qnw, knw, snk, anw,
                        o_o, om_o, kpm_o, kl_o, vl_o,
                        km_lo, km_hi, vmt, kp_lo, kp_hi, vpt, kmf, ombuf):
    c = pl.program_id(1)
    pad = CHUNK - N_META
    cols = N_Q_HEADS * CHUNK
    nblk = q_m.shape[0] // CHUNK
    carry = {}

    def sink_row():
        return jnp.concatenate([jnp.broadcast_to(snk[:, h:h + 1], (1, CHUNK)) for h in range(N_Q_HEADS)],
                               axis=1)

    def meta():
        sk = sink_row()

        def stage(meta_ref):
            return jnp.concatenate([jnp.zeros((pad, meta_ref.shape[1]), F32), meta_ref[...]], axis=0)

        cs, sn = cos_t[...], sin_t[...]
        kp = _qk_prep(stage(k_t), knw[...], cs, sn)
        lane = lax.broadcasted_iota(jnp.int32, (N_META, LANES), 1)
        km_lo[...] = jnp.where(lane < HEAD_DIM, kp[pad:, :], 0.0).astype(BF16)
        km_hi[...] = jnp.where(lane < HEAD_DIM, 0.0, kp[pad:, :]).astype(BF16)
        kmf[...] = kp[pad:, :]
        vmt[...] = jnp.concatenate([v_t[...], jnp.zeros((pad, LANES), F32)], axis=0).T.astype(BF16)
        kp_lo[...] = jnp.zeros(kp_lo.shape, BF16)
        kp_hi[...] = jnp.zeros(kp_hi.shape, BF16)
        vpt[...] = jnp.zeros(vpt.shape, BF16)
        qs = jnp.concatenate(_q_groups(stage(q_t), qnw, cs, sn), axis=0).astype(BF16)
        r = lax.broadcasted_iota(jnp.int32, (N_META, cols), 0)
        qi = lax.broadcasted_iota(jnp.int32, (N_META, cols), 1) & (CHUNK - 1)
        s_m = jnp.where(r <= qi - pad, _scores_t(qs, km_lo[...], km_hi[...]), NEG)
        (e_m,), inv = _softmax_weights_t([s_m], sk)
        ombuf[...] = _attn_out_t(_dot(vmt[...], _meta_weights(e_m)) * inv, anw)[pad:, :]

    def begin():
        carry['prev'] = (kp_lo[...], kp_hi[...], vpt[...])
        carry['sk'] = sink_row()
        r = lax.broadcasted_iota(jnp.int32, (CHUNK, cols), 0)
        qi = lax.broadcasted_iota(jnp.int32, (CHUNK, cols), 1) & (CHUNK - 1)
        carry['tri'] = r <= qi

    def block(j):
        prev, tri = carry['prev'], carry['tri']
        rows = slice(j * CHUNK, (j + 1) * CHUNK)
        cs, sn = cos_m[rows, :], sin_m[rows, :]
        v = v_m[rows, :]
        kp, k_lo, k_hi, v_tb = _kv_prep_t(k_m[rows, :], v, knw, cs, sn)
        qs = jnp.concatenate(_q_groups(q_m[rows, :], qnw, cs, sn), axis=0).astype(BF16)
        band = jnp.where(tri, _scores_t(qs, k_lo, k_hi), _scores_t(qs, prev[0], prev[1]))
        if j == 0:
            band = jnp.where(jnp.logical_or(tri, c > 1), band, NEG)
        (e_b, e_m), inv = _softmax_weights_t([band, _scores_t(qs, km_lo[...], km_hi[...])], carry['sk'])
        o_t = (_dot(v_tb, jnp.where(tri, e_b, 0.0).astype(BF16))
               + _dot(prev[2], jnp.where(tri, 0.0, e_b).astype(BF16))
               + _dot(vmt[...], _meta_weights(e_m))) * inv
        o_o[rows, :] = _attn_out_t(o_t, anw).astype(BF16)
        carry['prev'] = (k_lo, k_hi, v_tb)
        if j == nblk - 1:
            kl_o[...] = kp
            vl_o[...] = v

    def end():
        kp_lo[...], kp_hi[...], vpt[...] = carry['prev']

    def every():
        om_o[...] = ombuf[...].astype(BF16)
        kpm_o[...] = kmf[...]

    return meta, begin, block, end, every


def _mixer_prompt_kernel(*refs):
    n_in, n_out = N_SSD_IN + N_ATTN_IN, N_SSD_OUT + N_ATTN_OUT
    ins, outs, scratch = refs[:n_in], refs[n_in:n_in + n_out], refs[n_in + n_out:]
    s_meta, s_block, s_every = _ssd_prompt_phases(*ins[:N_SSD_IN], *outs[:N_SSD_OUT], *scratch[:N_SSD_SCRATCH])
    a_meta, a_begin, a_block, a_end, a_every = _attn_prompt_phases(
        *ins[N_SSD_IN:], *outs[N_SSD_OUT:], *scratch[N_SSD_SCRATCH:])
    c = pl.program_id(1)

    @pl.when(c == 0)
    def _():
        s_meta()
        a_meta()

    @pl.when(c > 0)
    def _():
        a_begin()
        for j in range(ins[0].shape[0] // CHUNK):
            s_block(j)
            a_block(j)
        a_end()

    s_every()
    a_every()


def _attn_sample_kernel(q_s, k_s, v_s, mk, mv, wk, wv, cos, sin, qnw, knw, snk, anw,
                        o_o, wk_o, wv_o, obuf):
    nsq = mk.shape[0]
    t = SUBLANES
    nst = Q_PER_KV * t
    cs, sn = cos[...], sin[...]
    kp = _qk_prep(k_s[...], knw[...], cs, sn)
    v = v_s[...]
    qg = _q_groups(q_s[...], qnw, cs, sn)

    lo = lax.broadcasted_iota(jnp.int32, (nst, LANES), 1) < HEAD_DIM
    i_q = lax.broadcasted_iota(jnp.int32, (2 * nst, LANES), 0) & (t - 1)
    cj = lax.broadcasted_iota(jnp.int32, (2 * nst, LANES), 1)
    mask_a = cj > i_q
    mask_b = jnp.logical_or(cj <= i_q, jnp.logical_and(cj >= t, cj < t + N_META))
    sk = _sink_column(snk, t)
    zpad = jnp.zeros((WINDOW - t - N_META, LANES), F32)
    for s in range(nsq):
        rows = slice(s * t, (s + 1) * t)
        q_st = jnp.concatenate([g[rows, :] for g in qg], axis=0)
        q2 = jnp.concatenate([jnp.where(lo, q_st, 0.0), jnp.where(lo, 0.0, q_st)], axis=0).astype(BF16)
        k_b = jnp.concatenate([kp[rows, :], mk[s], zpad], axis=0).astype(BF16)
        v_b = jnp.concatenate([v[rows, :], mv[s], zpad], axis=0).astype(BF16)
        (e_a, e_b), inv = _softmax_weights(
            [jnp.where(mask_a, _dot_nt(q2, wk[s].astype(BF16)), NEG),
             jnp.where(mask_b, _dot_nt(q2, k_b), NEG)], sk)
        o2 = (_dot(e_a.astype(BF16), wv[s].astype(BF16)) + _dot(e_b.astype(BF16), v_b)) * inv
        o_st = jnp.where(lo, o2[0:nst, :], o2[nst:2 * nst, :])
        for g in range(Q_PER_KV):
            obuf[rows, g * LANES:(g + 1) * LANES] = o_st[g * t:(g + 1) * t, :]
        wk_o[s] = jnp.concatenate([wk[s, t:, :], kp[rows, :]], axis=0)
        wv_o[s] = jnp.concatenate([wv[s, t:, :], v[rows, :]], axis=0)
    o_o[...] = _rms(obuf[...], anw[...]).astype(BF16)


def _mixer_prompt(proj_main, proj_tail, ssd_params, tabs_main, tabs_meta, attn_params, bsz, seq, meta_row):
    blk = _chunks_per_step(seq // CHUNK) * CHUNK
    nb = seq // blk
    mb = meta_row // N_META

    def main(col):
        return lambda b, c: (b * nb + jnp.maximum(c - 1, 0), col)

    def meta(col):
        return lambda b, c: (mb, col)

    per_batch = lambda b, c: (b, 0, 0)
    in_specs = [
        pl.BlockSpec((blk, CONV_DIM), main(COL_XBC // CONV_DIM)),
        pl.BlockSpec((blk, D_SSD), main(COL_Z // D_SSD)),
        pl.BlockSpec((blk, LANES), main(COL_DT // LANES)),
        pl.BlockSpec((N_META, CONV_DIM), meta(COL_XBC // CONV_DIM)),
        pl.BlockSpec((N_META, D_SSD), meta(COL_Z // D_SSD)),
        pl.BlockSpec((N_META, LANES), meta(COL_DT // LANES)),
    ] + _param_specs(ssd_params) + [
        pl.BlockSpec((blk, D_ATTN), main(COL_Q // D_ATTN)),
        pl.BlockSpec((blk, KV_DIM), main(COL_K // KV_DIM)),
        pl.BlockSpec((blk, KV_DIM), main(COL_V // KV_DIM)),
        pl.BlockSpec((N_META, D_ATTN), meta(COL_Q // D_ATTN)),
        pl.BlockSpec((N_META, KV_DIM), meta(COL_K // KV_DIM)),
        pl.BlockSpec((N_META, KV_DIM), meta(COL_V // KV_DIM)),
        pl.BlockSpec((blk, LANES), lambda b, c: (jnp.maximum(c - 1, 0), 0)),
        pl.BlockSpec((blk, LANES), lambda b, c: (jnp.maximum(c - 1, 0), 0)),
        pl.BlockSpec((CHUNK, LANES), lambda b, c: (0, 0)),
        pl.BlockSpec((CHUNK, LANES), lambda b, c: (0, 0)),
    ] + _param_specs(attn_params)
    assert len(in_specs) == N_SSD_IN + N_ATTN_IN
    return pl.pallas_call(
        _mixer_prompt_kernel,
        grid=(bsz, nb + 1),
        in_specs=in_specs,
        out_specs=[
            pl.BlockSpec((blk, D_SSD), main(0)),
            pl.BlockSpec((None, N_META, D_SSD), per_batch),
            pl.BlockSpec((None, N_SSD_HEADS, SSD_HEAD_DIM, D_STATE), lambda b, c: (b, 0, 0, 0)),
            pl.BlockSpec((None, SUBLANES, CONV_DIM), per_batch),
            pl.BlockSpec((blk, D_ATTN), main(0)),
            pl.BlockSpec((None, N_META, D_ATTN), per_batch),
            pl.BlockSpec((None, N_META, KV_DIM), per_batch),
            pl.BlockSpec((None, CHUNK, KV_DIM), per_batch),
            pl.BlockSpec((None, CHUNK, KV_DIM), per_batch),
        ],
        out_shape=[
            jax.ShapeDtypeStruct((bsz * seq, D_SSD), BF16),
            jax.ShapeDtypeStruct((bsz, N_META, D_SSD), BF16),
            jax.ShapeDtypeStruct((bsz, N_SSD_HEADS, SSD_HEAD_DIM, D_STATE), F32),
            jax.ShapeDtypeStruct((bsz, SUBLANES, CONV_DIM), F32),
            jax.ShapeDtypeStruct((bsz * seq, D_ATTN), BF16),
            jax.ShapeDtypeStruct((bsz, N_META, D_ATTN), BF16),
            jax.ShapeDtypeStruct((bsz, N_META, KV_DIM), F32),
            jax.ShapeDtypeStruct((bsz, CHUNK, KV_DIM), F32),
            jax.ShapeDtypeStruct((bsz, CHUNK, KV_DIM), F32),
        ],
        scratch_shapes=[
            pltpu.VMEM((SUBLANES, CONV_DIM), F32), pltpu.VMEM((N_SSD_HEADS * SSD_HEAD_DIM, D_STATE), F32),
            pltpu.VMEM((N_META, D_SSD), F32)] + [
            pltpu.VMEM((N_META, KV_DIM), BF16) for _ in range(2)] + [
            pltpu.VMEM((CHUNK, KV_DIM), BF16) for _ in range(4)] + [
            pltpu.VMEM((N_META, KV_DIM), F32), pltpu.VMEM((N_META, D_ATTN), F32),
        ],
        compiler_params=_cparams(2),
        name="mixer_prompt",
    )(proj_main, proj_main, proj_main, proj_tail, proj_tail, proj_tail, *ssd_params,
      proj_main, proj_main, proj_main, proj_tail, proj_tail, proj_tail, *tabs_main, *tabs_meta, *attn_params)


def _attn_sample(proj_tail, mk, mv, wk, wv, layer, cos, sin, params, nseq, t, nsq):
    blk = nsq * t
    cache = lambda b: (layer, b, 0, 0)
    in_specs = [
        pl.BlockSpec((blk, D_ATTN), lambda b: (b, COL_Q // D_ATTN)),
        pl.BlockSpec((blk, KV_DIM), lambda b: (b, COL_K // KV_DIM)),
        pl.BlockSpec((blk, KV_DIM), lambda b: (b, COL_V // KV_DIM)),
        pl.BlockSpec((None, nsq, N_META, KV_DIM), cache),
        pl.BlockSpec((None, nsq, N_META, KV_DIM), cache),
        pl.BlockSpec((None, nsq, WINDOW, KV_DIM), cache),
        pl.BlockSpec((None, nsq, WINDOW, KV_DIM), cache),
        pl.BlockSpec((blk, LANES), lambda b: (0, 0)),
        pl.BlockSpec((blk, LANES), lambda b: (0, 0)),
    ] + _param_specs(params)
    return pl.pallas_call(
        _attn_sample_kernel,
        grid=(nseq // nsq,),
        in_specs=in_specs,
        out_specs=[
            pl.BlockSpec((blk, D_ATTN), lambda b: (b, 0)),
            pl.BlockSpec((nsq, WINDOW, KV_DIM), lambda b: (b, 0, 0)),
            pl.BlockSpec((nsq, WINDOW, KV_DIM), lambda b: (b, 0, 0)),
        ],
        out_shape=[
            jax.ShapeDtypeStruct((nseq * t, D_ATTN), BF16),
            jax.ShapeDtypeStruct((nseq, WINDOW, KV_DIM), F32),
            jax.ShapeDtypeStruct((nseq, WINDOW, KV_DIM), F32),
        ],
        scratch_shapes=[pltpu.VMEM((blk, D_ATTN), F32)],
        compiler_params=_cparams(1),
        name="attn_sample",
    )(proj_tail, proj_tail, proj_tail, mk, mv, wk, wv, cos, sin, *params)


def _mix_out(n_main, x_refs, ys_refs, ya_refs, wo_ref):
    return (_pick(n_main, *x_refs) + _dot(_pick(n_main, *ys_refs), wo_ref[0:D_SSD, :])
            + _dot(_pick(n_main, *ya_refs), wo_ref[D_SSD:D_SSD + D_ATTN, :]))


def _swiglu_acc(hn, wg_ref, wu_ref, wd_ref, acc_ref):
    for j in range(D_FF // FF_CHUNK):
        cols = slice(j * FF_CHUNK, (j + 1) * FF_CHUNK)
        a = (_silu(_dot(hn, wg_ref[:, cols])) * _dot(hn, wu_ref[:, cols])).astype(BF16)
        acc_ref[...] += _dot(a, wd_ref[cols, :])


def _out_ffn_kernel(n_main, xa, xb, ysa, ysb, yaa, yab, wo_ref, nw_ref, wg_ref, wu_ref, wd_ref,
                    om_ref, ot_ref):
    xm = _mix_out(n_main, (xa, xb), (ysa, ysb), (yaa, yab), wo_ref)
    om_ref[...] = xm
    _swiglu_acc(_rms(xm, nw_ref[...]).astype(BF16), wg_ref, wu_ref, wd_ref, om_ref)
    ot_ref[...] = om_ref[...]


def _out_ffn(x, ys, ya, wo, nw, wg, wu, wd, n_main):
    n_tail = x[1].shape[0] // ROW_TILE
    return pl.pallas_call(
        functools.partial(_out_ffn_kernel, n_main),
        grid=(n_main + n_tail,),
        in_specs=_src_specs(D_MODEL, n_main) + _src_specs(D_SSD, n_main) + _src_specs(D_ATTN, n_main) + [
            _resident(wo.shape), _resident(nw.shape),
            _resident(wg.shape), _resident(wu.shape), _resident(wd.shape),
        ],
        out_specs=_dst_specs(D_MODEL, n_main),
        out_shape=_dst_shapes(D_MODEL, n_main, n_tail, F32),
        compiler_params=_cparams(1),
        name="out_ffn",
    )(*x, *ys, *ya, wo, nw, wg, wu, wd)


def _out_router_kernel(n_main, n_tok, xa, xb, ysa, ysb, yaa, yab, wo_ref, nw_ref, wr_hi_ref, wr_lo_ref,
                       before_ref, xm_o, rt_o, cnt_o):
    xm = _mix_out(n_main, (xa, xb), (ysa, ysb), (yaa, yab), wo_ref)
    xm_o[...] = xm
    hn = _rms(xm, nw_ref[...])
    hi = hn.astype(BF16)
    lo = (hn - hi.astype(F32)).astype(BF16)
    logits = _dot(hi, wr_hi_ref[...]) + _dot(lo, wr_hi_ref[...]) + _dot(hi, wr_lo_ref[...])
    lane = lax.broadcasted_iota(jnp.int32, logits.shape, 1)
    logits = jnp.where(lane < N_EXPERTS, logits, -jnp.inf)
    v1 = jnp.max(logits, axis=-1, keepdims=True)
    i1 = jnp.min(jnp.where(logits == v1, lane, LANES), axis=-1, keepdims=True)
    rest = jnp.where(lane == i1, -jnp.inf, logits)
    v2 = jnp.max(rest, axis=-1, keepdims=True)
    i2 = jnp.min(jnp.where(rest == v2, lane, LANES), axis=-1, keepdims=True)
    e2 = jnp.exp(v2 - v1)
    g1 = 1.0 / (1.0 + e2)
    g2 = e2 / (1.0 + e2)

    row = pl.program_id(0) * ROW_TILE + lax.broadcasted_iota(jnp.int32, (ROW_TILE, 1), 0)
    valid = row < n_tok
    oh1 = jnp.where(jnp.logical_and(lane == i1, valid), 1.0, 0.0)
    oh2 = jnp.where(jnp.logical_and(lane == i2, valid), 1.0, 0.0)
    before = before_ref[...]
    c1 = _dot(before, oh1.astype(BF16))
    c2 = _dot(before, oh2.astype(BF16))
    tot1 = jnp.sum(oh1, axis=0, keepdims=True)
    tot2 = jnp.sum(oh2, axis=0, keepdims=True)
    rank1 = jnp.sum(jnp.where(lane == i1, c1, 0.0), axis=-1, keepdims=True)
    rank2 = jnp.sum(jnp.where(lane == i2, c2 + tot1, 0.0), axis=-1, keepdims=True)
    cnt_o[...] = jnp.broadcast_to(tot1 + tot2, cnt_o.shape)
    route = jnp.where(lane == 0, i1.astype(F32), 0.0)
    for k, val in enumerate((i2.astype(F32), g1, g2, rank1, rank2)):
        route = jnp.where(lane == k + 1, val, route)
    rt_o[...] = route


def _out_router(x, ys, ya, wo, nw, wr_hi, wr_lo, n_main, n_tok):
    n_tiles = n_main + x[1].shape[0] // ROW_TILE
    rows = n_tiles * ROW_TILE
    before = jnp.tril(jnp.ones((ROW_TILE, ROW_TILE), BF16), -1)
    return pl.pallas_call(
        functools.partial(_out_router_kernel, n_main, n_tok),
        grid=(n_tiles,),
        in_specs=_src_specs(D_MODEL, n_main) + _src_specs(D_SSD, n_main) + _src_specs(D_ATTN, n_main) + [
            _resident(wo.shape), _resident(nw.shape), _resident(wr_hi.shape), _resident(wr_lo.shape),
            _resident(before.shape),
        ],
        out_specs=[
            pl.BlockSpec((ROW_TILE, D_MODEL), lambda i: (i, 0)),
            pl.BlockSpec((ROW_TILE, LANES), lambda i: (i, 0)),
            pl.BlockSpec((None, SUBLANES, LANES), lambda i: (i, 0, 0)),
        ],
        out_shape=[
            jax.ShapeDtypeStruct((rows, D_MODEL), F32),
            jax.ShapeDtypeStruct((rows, LANES), F32),
            jax.ShapeDtypeStruct((n_tiles, SUBLANES, LANES), F32),
        ],
        compiler_params=_cparams(1),
        name="out_router",
    )(*x, *ys, *ya, wo, nw, wr_hi, wr_lo, before)


def _tile_rows(idx, n_tiles, tail, fn):
    if tail == ROW_TILE:
        fn(ROW_TILE)
    else:
        pl.when(idx < n_tiles - 1)(lambda: fn(ROW_TILE))
        pl.when(idx == n_tiles - 1)(lambda: fn(tail))


def _dispatch_kernel(n_tiles, tail, zt_ref, dest_ref, x_ref, xs_ref, zbuf, sem, zsem):
    @pl.when(pl.program_id(0) == 0)
    def _():
        zbuf[...] = jnp.zeros(zbuf.shape, F32)

        def zero_copy(j):
            start = pl.multiple_of(zt_ref[j] * ROW_TILE, ROW_TILE)
            return pltpu.make_async_copy(zbuf, xs_ref.at[pl.ds(start, ROW_TILE)], zsem)

        for j in range(zt_ref.shape[0]):
            pl.when(zt_ref[j] >= 0)(lambda j=j: zero_copy(j).start())
        for j in range(zt_ref.shape[0]):
            pl.when(zt_ref[j] >= 0)(lambda j=j: zero_copy(j).wait())

    def run(nrows):
        def body(r, carry):
            for k in range(TOP_K):
                d = dest_ref[0, 0, TOP_K * r + k]
                pltpu.make_async_copy(x_ref.at[pl.ds(r, 1)], xs_ref.at[pl.ds(d, 1)], sem).start()
            return carry

        lax.fori_loop(0, nrows, body, 0, unroll=DMA_UNROLL)
        for k in range(TOP_K):
            pltpu.make_async_copy(x_ref.at[pl.ds(0, nrows)], xs_ref.at[pl.ds(0, nrows)], sem).wait()

    _tile_rows(pl.program_id(0), n_tiles, tail, run)


def _dispatch(last_tile, dest, xm, n_tok, m_rows):
    n_tiles = xm.shape[0] // ROW_TILE
    tail = n_tok - (n_tiles - 1) * ROW_TILE
    grid_spec = pltpu.PrefetchScalarGridSpec(
        num_scalar_prefetch=1,
        grid=(n_tiles,),
        in_specs=[
            pl.BlockSpec((1, 1, TOP_K * ROW_TILE), lambda i, lt: (i, 0, 0), memory_space=pltpu.SMEM),
            pl.BlockSpec((ROW_TILE, D_MODEL), lambda i, lt: (i, 0)),
        ],
        out_specs=pl.BlockSpec(memory_space=pl.ANY),
        scratch_shapes=[pltpu.VMEM((ROW_TILE, D_MODEL), F32), pltpu.SemaphoreType.DMA(()),
                        pltpu.SemaphoreType.DMA(())],
    )
    return pl.pallas_call(
        functools.partial(_dispatch_kernel, n_tiles, tail),
        grid_spec=grid_spec,
        out_shape=jax.ShapeDtypeStruct((m_rows, D_MODEL), F32),
        compiler_params=_cparams(1, has_side_effects=True),
        name="moe_dispatch",
    )(last_tile, dest, xm)


def _moe_kernel(te_ref, nu_ref, x_ref, nw_ref, wg_ref, wu_ref, wd_ref, o_ref):
    i = pl.program_id(0)
    o_ref[...] = jnp.zeros(o_ref.shape, F32)

    @pl.when(i < nu_ref[0])
    def _():
        _swiglu_acc(_rms(x_ref[...], nw_ref[...]).astype(BF16), wg_ref, wu_ref, wd_ref, o_ref)


def _moe_experts(tile_e, n_used, xs, nw, wg, wu, wd):
    rows = xs.shape[0]
    grid_spec = pltpu.PrefetchScalarGridSpec(
        num_scalar_prefetch=2,
        grid=(rows // ROW_TILE,),
        in_specs=[
            pl.BlockSpec((ROW_TILE, D_MODEL), lambda i, te, nu: (jnp.maximum(jnp.minimum(i, nu[0] - 1), 0), 0)),
            pl.BlockSpec((1, D_MODEL), lambda i, te, nu: (0, 0)),
            pl.BlockSpec((None, D_MODEL, D_FF), lambda i, te, nu: (te[i], 0, 0)),
            pl.BlockSpec((None, D_MODEL, D_FF), lambda i, te, nu: (te[i], 0, 0)),
            pl.BlockSpec((None, D_FF, D_MODEL), lambda i, te, nu: (te[i], 0, 0)),
        ],
        out_specs=pl.BlockSpec((ROW_TILE, D_MODEL), lambda i, te, nu: (i, 0)),
    )
    return pl.pallas_call(
        _moe_kernel,
        grid_spec=grid_spec,
        out_shape=jax.ShapeDtypeStruct((rows, D_MODEL), F32),
        compiler_params=_cparams(1),
        name="moe_experts",
    )(tile_e, n_used, xs, nw, wg, wu, wd)


def _combine_kernel(n_main, n_tiles, tail, dcur_ref, dnext_ref, xm_ref, rt_ref, yb_ref,
                    om_ref, ot_ref, gbuf, sem):
    i = pl.program_id(0)

    def issue(dref, slot, nrows):
        def body(r, carry):
            for k in range(TOP_K):
                d = dref[0, 0, TOP_K * r + k]
                pltpu.make_async_copy(yb_ref.at[pl.ds(d, 1)], gbuf.at[slot, k, pl.ds(r, 1)],
                                      sem.at[slot]).start()
            return carry

        lax.fori_loop(0, nrows, body, 0, unroll=DMA_UNROLL)

    def wait(slot, nrows):
        for k in range(TOP_K):
            pltpu.make_async_copy(yb_ref.at[pl.ds(0, nrows)], gbuf.at[slot, k, pl.ds(0, nrows)],
                                  sem.at[slot]).wait()

    @pl.when(i == 0)
    def _():
        gbuf[...] = jnp.zeros(gbuf.shape, F32)
        _tile_rows(i, n_tiles, tail, lambda n: issue(dcur_ref, 0, n))

    for slot in range(2):
        @pl.when(jnp.logical_and(i + 1 < n_tiles, (i + 1) % 2 == slot))
        def _(slot=slot):
            _tile_rows(i + 1, n_tiles, tail, lambda n: issue(dnext_ref, slot, n))

    for slot in range(2):
        @pl.when(i % 2 == slot)
        def _(slot=slot):
            _tile_rows(i, n_tiles, tail, lambda n: wait(slot, n))
            gates = rt_ref[...]
            val = (xm_ref[...] + gates[:, TOP_K:TOP_K + 1] * gbuf[slot, 0]
                   + gates[:, TOP_K + 1:TOP_K + 2] * gbuf[slot, 1])

            @pl.when(i < n_main)
            def _():
                om_ref[...] = val

            @pl.when(i >= n_main)
            def _():
                ot_ref[...] = val


def _combine(dest, xm, route, yb, n_main, n_tok):
    n_tiles = xm.shape[0] // ROW_TILE
    tail = n_tok - (n_tiles - 1) * ROW_TILE
    dspec = lambda f: pl.BlockSpec((1, 1, TOP_K * ROW_TILE), f, memory_space=pltpu.SMEM)
    return pl.pallas_call(
        functools.partial(_combine_kernel, n_main, n_tiles, tail),
        grid=(n_tiles,),
        in_specs=[
            dspec(lambda i: (i, 0, 0)),
            dspec(lambda i: (jnp.minimum(i + 1, n_tiles - 1), 0, 0)),
            pl.BlockSpec((ROW_TILE, D_MODEL), lambda i: (i, 0)),
            pl.BlockSpec((ROW_TILE, LANES), lambda i: (i, 0)),
            pl.BlockSpec(memory_space=pl.ANY),
        ],
        out_specs=_src_specs(D_MODEL, n_main),
        out_shape=[jax.ShapeDtypeStruct((n_main * ROW_TILE, D_MODEL), F32),
                   jax.ShapeDtypeStruct(((n_tiles - n_main) * ROW_TILE, D_MODEL), F32)],
        scratch_shapes=[pltpu.VMEM((2, TOP_K, ROW_TILE, D_MODEL), F32), pltpu.SemaphoreType.DMA((2,))],
        compiler_params=_cparams(1),
        name="moe_combine",
    )(dest, dest, xm, route, yb)


def _moe_layer(xm, route, counts, n_main, n_tok, nw, wg, wu, wd):
    n_tiles = xm.shape[0] // ROW_TILE
    m_tiles = -(-(n_tok * TOP_K + N_EXPERTS * (ROW_TILE - 1)) // ROW_TILE)
    cnt = counts[:, 0, :N_EXPERTS].astype(jnp.int32)
    total = jnp.sum(cnt, axis=0)
    padded = (total + ROW_TILE - 1) // ROW_TILE * ROW_TILE
    pad_end = jnp.cumsum(padded)
    base = (pad_end - padded)[None, :] + jnp.cumsum(cnt, axis=0) - cnt
    e = route[:, 0:TOP_K].astype(jnp.int32).reshape(n_tiles, ROW_TILE, TOP_K)
    rank = route[:, 2 * TOP_K:3 * TOP_K].astype(jnp.int32).reshape(n_tiles, ROW_TILE, TOP_K)
    onehot = e[..., None] == jnp.arange(N_EXPERTS, dtype=jnp.int32)
    dest = jnp.sum(jnp.where(onehot, base[:, None, None, :], 0), axis=-1) + rank
    dest = dest.reshape(n_tiles, 1, ROW_TILE * TOP_K)
    tile_start = jnp.arange(m_tiles, dtype=jnp.int32) * ROW_TILE
    tile_e = jnp.minimum(jnp.sum((pad_end[None, :] <= tile_start[:, None]).astype(jnp.int32), axis=1),
                         N_EXPERTS - 1)
    n_used = (pad_end[-1:] // ROW_TILE).astype(jnp.int32)
    last_tile = jnp.where(padded > 0, pad_end // ROW_TILE - 1, -1)
    spare = n_used[0] + jnp.arange(m_tiles - (n_tok * TOP_K) // ROW_TILE, dtype=jnp.int32)
    zero_tiles = jnp.concatenate([last_tile, jnp.where(spare < m_tiles, spare, -1)]).astype(jnp.int32)
    xs = _dispatch(zero_tiles, dest, xm, n_tok, m_tiles * ROW_TILE)
    yb = _moe_experts(tile_e, n_used, xs, nw, wg, wu, wd)
    return _combine(dest, xm, route, yb, n_main, n_tok)


def _rope_tables(pos):
    half = HEAD_DIM // 2
    inv_freq = ROPE_THETA ** (-jnp.arange(half, dtype=F32) / half)
    ang = pos.astype(F32)[:, None] * inv_freq[None, :]
    cos = jnp.cos(ang)
    sin = jnp.sin(ang)
    reps = LANES // HEAD_DIM
    return (jnp.tile(jnp.concatenate([cos, cos], axis=-1), (1, reps)),
            jnp.tile(jnp.concatenate([-sin, sin], axis=-1), (1, reps)))


def _pad_lanes(v, width=LANES):
    v = v.astype(F32).reshape(1, -1)
    return jnp.pad(v, ((0, 0), (0, width - v.shape[1])))


def kernel(x_prompt, x_sample, state_ssm, state_conv, cache_meta_k, cache_meta_v, cache_win_k, cache_win_v, meta_tokens, norm_mix_w, w_in, conv_w, conv_b, dt_bias, a_log, d_skip, ssd_norm_w, q_norm_w, k_norm_w, sinks, attn_norm_w, w_out, norm_ffn_w, w_gate, w_up, w_down, w_router, moe_w_gate, moe_w_up, moe_w_down):
    bsz, seq, _ = x_prompt.shape
    nseq, t_s, _ = x_sample.shape
    depth = w_in.shape[0]
    r_main = bsz * seq
    r_samp = nseq * t_s
    assert seq % CHUNK == 0 and t_s == SUBLANES and r_main % ROW_TILE == 0 and r_samp % N_META == 0
    n_main = r_main // ROW_TILE
    n_tok = r_main + r_samp + N_META
    r_tail = -(-(r_samp + N_META) // ROW_TILE) * ROW_TILE
    tail_pad = r_tail - r_samp - N_META

    x = (x_prompt.reshape(r_main, D_MODEL),
         jnp.concatenate([x_sample.reshape(r_samp, D_MODEL), meta_tokens.astype(F32),
                          jnp.zeros((tail_pad, D_MODEL), F32)], axis=0))

    tabs_main = _rope_tables(N_META + jnp.arange(seq, dtype=jnp.int32))
    tabs_meta = _rope_tables(jnp.arange(CHUNK, dtype=jnp.int32) - (CHUNK - N_META))
    nsq = 16 if nseq % 16 == 0 else nseq
    cos_s, sin_s = (jnp.tile(tab, (nsq, 1)) for tab in
                    _rope_tables(PAST_LEN + jnp.arange(t_s, dtype=jnp.int32)))

    o_z, o_xbc, o_dt, o_q, o_k, o_v = 0, 512, 1536, 1544, 2056, 2184
    col = jnp.arange(D_ATTN, dtype=jnp.int32)
    grp, lane = col // LANES, col % LANES
    head_perm = (grp + Q_PER_KV * (lane // HEAD_DIM)) * HEAD_DIM + lane % HEAD_DIM

    def heads4(a):
        return a.reshape(a.shape[0], a.shape[1], N_KV_HEADS, HEAD_DIM)

    outs = {k: [] for k in ('p_ssm', 'p_conv', 'p_mk', 'p_mv', 'p_wk', 'p_wv', 's_ssm', 's_conv', 's_wk', 's_wv')}
    for l in range(depth):
        wl = w_in[l]
        w_re = jnp.concatenate([
            wl[:, o_xbc:o_xbc + CONV_DIM], wl[:, o_z:o_z + D_SSD], wl[:, o_q:o_q + D_ATTN][:, head_perm],
            wl[:, o_k:o_k + KV_DIM], wl[:, o_v:o_v + KV_DIM], wl[:, o_dt:o_dt + N_SSD_HEADS],
            jnp.zeros((D_MODEL, PROJ_W - COL_DT - N_SSD_HEADS), wl.dtype)], axis=1).astype(BF16)
        proj_main, proj_tail = _in_proj(x[0], x[1], norm_mix_w[l].reshape(1, D_MODEL).astype(F32), w_re, n_main)

        ssd_params = (conv_w[l].astype(F32), conv_b[l].reshape(1, CONV_DIM).astype(F32),
                      _pad_lanes(dt_bias[l]), _pad_lanes(-jnp.exp(a_log[l].astype(F32))),
                      jnp.repeat(d_skip[l].astype(F32), SSD_HEAD_DIM).reshape(1, D_SSD),
                      ssd_norm_w[l].reshape(1, D_SSD).astype(F32))
        attn_params = (jnp.tile(q_norm_w[l].astype(F32), N_Q_HEADS).reshape(1, D_ATTN),
                       jnp.tile(k_norm_w[l].astype(F32), N_KV_HEADS).reshape(1, KV_DIM),
                       _pad_lanes(sinks[l]), attn_norm_w[l][head_perm].reshape(1, D_ATTN).astype(F32))
        ys_p, ys_m, ssm_p, conv_p, ya_p, ya_m, kp_m, k_last, v_last = _mixer_prompt(
            proj_main, proj_tail, ssd_params, tabs_main, tabs_meta, attn_params, bsz, seq, r_samp)
        conv_prev = jnp.pad(state_conv[l].astype(F32), ((0, 0), (SUBLANES - (CONV_W - 1), 0), (0, 0)))
        ys_s, ssm_s = _ssd_sample(proj_tail, conv_prev, state_ssm.astype(F32), l, ssd_params, nseq, t_s, nsq)
        ya_s, wk_s, wv_s = _attn_sample(
            proj_tail, cache_meta_k.reshape(depth, nseq, N_META, KV_DIM).astype(F32),
            cache_meta_v.reshape(depth, nseq, N_META, KV_DIM).astype(F32),
            cache_win_k.reshape(depth, nseq, WINDOW, KV_DIM).astype(F32),
            cache_win_v.reshape(depth, nseq, WINDOW, KV_DIM).astype(F32),
            l, cos_s, sin_s, attn_params, nseq, t_s, nsq)

        ys = (ys_p, jnp.concatenate([ys_s, ys_m[0], jnp.zeros((tail_pad, D_SSD), BF16)], axis=0))
        ya = (ya_p, jnp.concatenate([ya_s, ya_m[0], jnp.zeros((tail_pad, D_ATTN), BF16)], axis=0))

        wo = jnp.concatenate([w_out[l][:D_SSD], w_out[l][D_SSD:][head_perm]], axis=0).astype(BF16)
        nfw = norm_ffn_w[l].reshape(1, D_MODEL).astype(F32)
        i = l // 2
        if l % 2 == 0:
            x = _out_ffn(x, ys, ya, wo, nfw, w_gate[i].astype(BF16), w_up[i].astype(BF16),
                         w_down[i].astype(BF16), n_main)
        else:
            wr = jnp.pad(w_router[i].astype(F32), ((0, 0), (0, LANES - N_EXPERTS)))
            wr_hi = wr.astype(BF16)
            wr_lo = (wr - wr_hi.astype(F32)).astype(BF16)
            xm, route, counts = _out_router(x, ys, ya, wo, nfw, wr_hi, wr_lo, n_main, n_tok)
            x = _moe_layer(xm, route, counts, n_main, n_tok, nfw, moe_w_gate[i].astype(BF16),
                           moe_w_up[i].astype(BF16), moe_w_down[i].astype(BF16))

        samp = proj_tail[:r_samp]
        xbc_s = samp[:, COL_XBC:COL_XBC + CONV_DIM].reshape(nseq, t_s, CONV_DIM)
        v_meta = proj_tail[r_samp:r_samp + N_META, COL_V:COL_V + KV_DIM]
        meta_shape = (bsz, N_META, N_KV_HEADS, HEAD_DIM)
        outs['p_ssm'].append(ssm_p)
        outs['p_conv'].append(conv_p[:, SUBLANES - (CONV_W - 1):])
        outs['p_mk'].append(jnp.broadcast_to(heads4(kp_m[0:1]), meta_shape))
        outs['p_mv'].append(jnp.broadcast_to(heads4(v_meta[None]), meta_shape))
        outs['p_wk'].append(heads4(k_last))
        outs['p_wv'].append(heads4(v_last))
        outs['s_ssm'].append(ssm_s)
        outs['s_conv'].append(jnp.concatenate([state_conv[l].astype(F32), xbc_s], axis=1)[:, t_s:])
        outs['s_wk'].append(heads4(wk_s))
        outs['s_wv'].append(heads4(wv_s))

    y_prompt = x[0][:r_main].reshape(bsz, seq, D_MODEL)
    y_sample = x[1][:r_samp].reshape(nseq, t_s, D_MODEL)
    st = lambda k: jnp.stack(outs[k])
    return (y_prompt, y_sample, st('p_ssm'), st('p_conv'), st('p_mk'), st('p_mv'), st('p_wk'), st('p_wv'),
            st('s_ssm'), st('s_conv'), st('s_wk'), st('s_wv'))
```

```python
import functools

import jax
import jax.numpy as jnp
from jax import lax
from jax.experimental import pallas as pl
from jax.experimental.pallas import tpu as pltpu

F32 = jnp.float32
BF16 = jnp.bfloat16

D_MODEL = 1024
D_SSD = 512
SSD_HEAD_DIM = 64
N_SSD_HEADS = 8
SSD_HEADS_PER_GROUP = 4
N_SSD_GROUPS = 2
D_STATE = 128
CONV_W = 4
CONV_DIM = 1024
D_ATTN = 512
HEAD_DIM = 64
N_Q_HEADS = 8
N_KV_HEADS = 2
Q_PER_KV = 4
KV_DIM = 128
WINDOW = 128
N_META = 16
D_FF = 2816
N_EXPERTS = 8
TOP_K = 2
EPS = 1e-6
NEG = -1e30
ATTN_SCALE = HEAD_DIM ** -0.5
PAST_LEN = 16384
ROPE_THETA = 10000.0

LANES = 128
SUBLANES = 8
CHUNK = 128
ROW_TILE = 512
FF_CHUNK = 256
DMA_UNROLL = 8
VMEM_LIMIT = 60 * 1024 * 1024

COL_XBC = 0
COL_Z = 1024
COL_Q = 1536
COL_K = 2048
COL_V = 2176
COL_DT = 2304
PROJ_W = 2432
PROJ_STEP = 512


def _dot(a, b):
    return jnp.dot(a, b, preferred_element_type=F32)


def _dot_nt(a, b):
    return lax.dot_general(a, b, (((1,), (1,)), ((), ())), preferred_element_type=F32)


def _dot_tn(a, b):
    return lax.dot_general(a, b, (((0,), (0,)), ((), ())), preferred_element_type=F32)


def _rms(x, w):
    return x * lax.rsqrt(jnp.mean(x * x, axis=-1, keepdims=True) + EPS) * w


def _silu(x):
    return x * jax.nn.sigmoid(x)


def _split3(x):
    p1 = x.astype(BF16)
    r1 = x - p1.astype(F32)
    p2 = r1.astype(BF16)
    p3 = (r1 - p2.astype(F32)).astype(BF16)
    return p1, p2, p3


def _dot_exact(sel, x):
    p1, p2, p3 = _split3(x)
    return _dot(sel, p1) + _dot(sel, p2) + _dot(sel, p3)


def _cparams(ndim, **kw):
    return pltpu.CompilerParams(dimension_semantics=("arbitrary",) * ndim, vmem_limit_bytes=VMEM_LIMIT, **kw)


def _src_specs(width, n_main):
    return [pl.BlockSpec((ROW_TILE, width), lambda i: (jnp.minimum(i, n_main - 1), 0)),
            pl.BlockSpec((ROW_TILE, width), lambda i: (jnp.maximum(i - n_main, 0), 0))]


def _pick(n_main, main_ref, tail_ref):
    dtype = main_ref.dtype
    picked = jnp.where(pl.program_id(0) < n_main, main_ref[...].astype(F32), tail_ref[...].astype(F32))
    return picked.astype(dtype)


def _dst_specs(width, n_main):
    return [pl.BlockSpec((ROW_TILE, width), lambda i: (jnp.minimum(i, n_main), 0)),
            pl.BlockSpec((ROW_TILE, width), lambda i: (jnp.maximum(i - n_main, 0), 0))]


def _dst_shapes(width, n_main, n_tail, dtype):
    return [jax.ShapeDtypeStruct(((n_main + 1) * ROW_TILE, width), dtype),
            jax.ShapeDtypeStruct((n_tail * ROW_TILE, width), dtype)]


def _resident(shape):
    nd = len(shape)
    return pl.BlockSpec(shape, lambda *a: (0,) * nd, pipeline_mode=pl.Buffered(1))


def _param_specs(params):
    return [pl.BlockSpec(p.shape, lambda *a: (0, 0)) for p in params]


def _in_proj_kernel(n_main, xa_ref, xb_ref, nw_ref, w_ref, om_ref, ot_ref):
    xn = _rms(_pick(n_main, xa_ref, xb_ref), nw_ref[...]).astype(BF16)
    for c0 in range(0, PROJ_W, PROJ_STEP):
        cols = slice(c0, min(c0 + PROJ_STEP, PROJ_W))
        r = _dot(xn, w_ref[:, cols])
        om_ref[:, cols] = r
        ot_ref[:, cols] = r


def _in_proj(x_main, x_tail, nw, w, n_main):
    n_tail = x_tail.shape[0] // ROW_TILE
    return pl.pallas_call(
        functools.partial(_in_proj_kernel, n_main),
        grid=(n_main + n_tail,),
        in_specs=_src_specs(D_MODEL, n_main) + [
            pl.BlockSpec((1, D_MODEL), lambda i: (0, 0)),
            pl.BlockSpec((D_MODEL, PROJ_W), lambda i: (0, 0)),
        ],
        out_specs=_dst_specs(PROJ_W, n_main),
        out_shape=_dst_shapes(PROJ_W, n_main, n_tail, F32),
        compiler_params=_cparams(1),
        name="in_proj",
    )(x_main, x_tail, nw, w)


def _ssd_chunk(xbc, z, dt_raw, valid, cw_ref, cb_ref, dtb_ref, an_ref, dsk_ref, nw_ref, cbuf, hst):
    q = CHUNK
    ext = jnp.concatenate([cbuf[...], xbc], axis=0)
    cbuf[...] = xbc[q - SUBLANES:q, :]
    acc = cb_ref[...] + xbc * cw_ref[CONV_W - 1:CONV_W, :]
    for s in range(1, CONV_W):
        acc = acc + pltpu.roll(ext, s, 0)[SUBLANES:, :] * cw_ref[CONV_W - 1 - s:CONV_W - s, :]
    xc = _silu(acc)
    xs = xc[:, :D_SSD]
    bm = xc[:, D_SSD:D_SSD + N_SSD_GROUPS * D_STATE]
    cm = xc[:, D_SSD + N_SSD_GROUPS * D_STATE:]

    dt = jax.nn.softplus(dt_raw + dtb_ref[...])
    if valid is not None:
        dt = jnp.where(valid, dt, 0.0)
    da = dt * an_ref[...]
    row_i = lax.broadcasted_iota(jnp.int32, (q, q), 0)
    col_j = lax.broadcasted_iota(jnp.int32, (q, q), 1)
    tril = row_i >= col_j
    cs = _dot_exact(jnp.where(tril, 1.0, 0.0).astype(BF16), da)
    cs_t = cs.T
    last = cs[q - 1:q, :]
    ecl = jnp.exp(last)
    xdt = xs * _expand_heads(dt)
    xw = xdt * _expand_heads(jnp.exp(last - cs))
    ecs_x = _expand_heads(jnp.exp(cs))
    skip = xs * dsk_ref[...]

    gw = SSD_HEADS_PER_GROUP * SSD_HEAD_DIM
    head_of_col = jnp.right_shift(lax.broadcasted_iota(jnp.int32, (q, gw), 1), SSD_HEAD_DIM.bit_length() - 1)
    ys = []
    for g in range(N_SSD_GROUPS):
        bg = bm[:, g * D_STATE:(g + 1) * D_STATE].astype(BF16)
        cg = cm[:, g * D_STATE:(g + 1) * D_STATE].astype(BF16)
        cb = _dot_nt(cg, bg)
        gc = slice(g * gw, (g + 1) * gw)
        h_prev = hst[gc, :]
        y_g = _dot_nt(cg, h_prev.astype(BF16)) * ecs_x[:, gc] + skip[:, gc]
        xdt_g = xdt[:, gc]
        for r in range(SSD_HEADS_PER_GROUP):
            h = g * SSD_HEADS_PER_GROUP + r
            seg = cs[:, h:h + 1] - cs_t[h:h + 1, :]
            decay = jnp.where(tril, jnp.exp(jnp.where(tril, seg, 0.0)), 0.0)
            y_g = y_g + _dot((decay * cb).astype(BF16), jnp.where(head_of_col == r, xdt_g, 0.0).astype(BF16))
        ys.append(y_g)
        ecl_col = jnp.concatenate(
            [jnp.broadcast_to(ecl[:, h:h + 1], (SSD_HEAD_DIM, 1))
             for h in range(g * SSD_HEADS_PER_GROUP, (g + 1) * SSD_HEADS_PER_GROUP)], axis=0)
        hst[gc, :] = h_prev * ecl_col + _dot_tn(xw[:, gc].astype(BF16), bg)

    return _ssd_gate_norm(jnp.concatenate(ys, axis=1), z, nw_ref)


def _ssd_gate_norm(y, z, nw_ref):
    y = y * _silu(z)
    gs = D_SSD // N_SSD_GROUPS
    return jnp.concatenate([_rms(y[:, g * gs:(g + 1) * gs], nw_ref[:, g * gs:(g + 1) * gs])
                            for g in range(N_SSD_GROUPS)], axis=-1)


N_SSD_IN, N_SSD_OUT, N_SSD_SCRATCH = 12, 4, 5


def _ssd_prompt_phases(xbc_m, z_m, dt_m, xbc_t, z_t, dt_t, cw, cb, dtb, an, dsk, nw,
                       y_o, ym_o, hf_o, ct_o, cbuf, hst, ymbuf, cbuf_meta, hst_meta):
    pad = CHUNK - N_META
    prm = (cw, cb, dtb, an, dsk, nw)

    def meta():
        cbuf[...] = jnp.zeros((SUBLANES, CONV_DIM), F32)
        hst[...] = jnp.zeros(hst.shape, F32)

        def stage(meta_ref):
            return jnp.concatenate([jnp.zeros((pad, meta_ref.shape[1]), F32), meta_ref[...]], axis=0)

        row = lax.broadcasted_iota(jnp.int32, (CHUNK, 1), 0)
        y = _ssd_chunk(stage(xbc_t), stage(z_t), stage(dt_t), row >= pad, *prm, cbuf, hst)
        ymbuf[...] = y[pad:, :]

    def block(j):
        rows = slice(j * CHUNK, (j + 1) * CHUNK)
        y = _ssd_chunk(xbc_m[rows, :], z_m[rows, :], dt_m[rows, :], None, *prm, cbuf, hst)
        y_o[rows, :] = y.astype(BF16)

    def every():
        ym_o[...] = ymbuf[...].astype(BF16)
        hf_o[...] = hst[...].reshape(hf_o.shape)
        ct_o[...] = cbuf[...]

    def save():
        cbuf_meta[...] = cbuf[...]
        hst_meta[...] = hst[...]

    def restore():
        cbuf[...] = cbuf_meta[...]
        hst[...] = hst_meta[...]

    return meta, block, every, save, restore


def _chunks_per_step(nc):
    return 4 if nc % 4 == 0 else (2 if nc % 2 == 0 else 1)


def _expand_heads(a):
    hh = lax.broadcasted_iota(jnp.int32, (LANES, D_SSD), 0)
    cc = lax.broadcasted_iota(jnp.int32, (LANES, D_SSD), 1)
    sel = jnp.where(jnp.right_shift(cc, SSD_HEAD_DIM.bit_length() - 1) == hh, 1.0, 0.0).astype(BF16)
    p1, p2, p3 = _split3(a)
    return _dot(p1, sel) + _dot(p2, sel) + _dot(p3, sel)


def _pad_rows_bf16(x, rows):
    return jnp.concatenate([x, jnp.zeros((rows - x.shape[0], x.shape[1]), F32)], axis=0).astype(BF16)


def _ssd_sample_kernel(xbc_s, z_s, dt_s, cprev, h0, cw, cb, dtb, an, dsk, nw,
                       y_o, hf_o, ubuf, ybuf):
    nsq = cprev.shape[0]
    t = SUBLANES
    rows = nsq * t
    x = xbc_s[...]
    for s in range(nsq):
        ubuf[2 * t * s:2 * t * s + t, :] = cprev[s]
        ubuf[2 * t * s + t:2 * t * (s + 1), :] = x[s * t:(s + 1) * t, :]
    parts = []
    for s in range(nsq):
        acc = cb[...]
        for k in range(CONV_W):
            off = 2 * t * s + t - (CONV_W - 1) + k
            acc = acc + ubuf[off:off + t, :] * cw[k:k + 1, :]
        parts.append(acc)
    xc = _silu(jnp.concatenate(parts, axis=0))
    xs = xc[:, :D_SSD]
    bm = xc[:, D_SSD:D_SSD + N_SSD_GROUPS * D_STATE]
    cm = xc[:, D_SSD + N_SSD_GROUPS * D_STATE:]

    dt = jax.nn.softplus(dt_s[...] + dtb[...])
    da = dt * an[...]
    ri = lax.broadcasted_iota(jnp.int32, (rows, rows), 0)
    cj = lax.broadcasted_iota(jnp.int32, (rows, rows), 1)
    mask = jnp.logical_and(ri >= cj, jnp.right_shift(ri, 3) == jnp.right_shift(cj, 3))
    cs = _dot_exact(jnp.where(mask, 1.0, 0.0).astype(BF16), da)
    last = _dot_exact(jnp.where(cj == jnp.bitwise_or(ri, t - 1), 1.0, 0.0).astype(BF16), cs)
    cs_t = cs.T
    ecl = jnp.exp(last)
    xdt = xs * _expand_heads(dt)
    xw = xdt * _expand_heads(jnp.exp(last - cs))
    ecs_x = _expand_heads(jnp.exp(cs))
    skip = xs * dsk[...]

    gw = SSD_HEADS_PER_GROUP * SSD_HEAD_DIM
    for g in range(N_SSD_GROUPS):
        bg_f = bm[:, g * D_STATE:(g + 1) * D_STATE]
        cg_f = cm[:, g * D_STATE:(g + 1) * D_STATE]
        cbm = _dot_nt(cg_f.astype(BF16), bg_f.astype(BF16))
        for r in range(SSD_HEADS_PER_GROUP):
            h = g * SSD_HEADS_PER_GROUP + r
            hc = slice(h * SSD_HEAD_DIM, (h + 1) * SSD_HEAD_DIM)
            seg = cs[:, h:h + 1] - cs_t[h:h + 1, :]
            decay = jnp.where(mask, jnp.exp(jnp.where(mask, seg, 0.0)), 0.0)
            ybuf[:, hc] = _dot((decay * cbm).astype(BF16), xdt[:, hc].astype(BF16)) + skip[:, hc]
        gc = slice(g * gw, (g + 1) * gw)
        heads = slice(g * SSD_HEADS_PER_GROUP, (g + 1) * SSD_HEADS_PER_GROUP)
        for s in range(nsq):
            rs = slice(s * t, (s + 1) * t)
            hg = h0[s, heads].reshape(gw, D_STATE)
            y_off = _dot_nt(_pad_rows_bf16(cg_f[rs, :], 2 * t), hg.astype(BF16))[0:t, :]
            ybuf[rs, gc] = ybuf[rs, gc] + y_off * ecs_x[rs, gc]
            ecl_col = jnp.concatenate(
                [jnp.broadcast_to(ecl[s * t:s * t + 1, h:h + 1], (SSD_HEAD_DIM, 1))
                 for h in range(heads.start, heads.stop)], axis=0)
            h_new = hg * ecl_col + _dot_tn(_pad_rows_bf16(xw[rs, gc], 2 * t), _pad_rows_bf16(bg_f[rs, :], 2 * t))
            hf_o[s, heads] = h_new.reshape(SSD_HEADS_PER_GROUP, SSD_HEAD_DIM, D_STATE)

    y_o[...] = _ssd_gate_norm(ybuf[...], z_s[...], nw).astype(BF16)


def _ssd_sample(proj_tail, conv_prev, h0, layer, params, nseq, t, nsq):
    blk = nsq * t
    in_specs = [
        pl.BlockSpec((blk, CONV_DIM), lambda b: (b, COL_XBC // CONV_DIM)),
        pl.BlockSpec((blk, D_SSD), lambda b: (b, COL_Z // D_SSD)),
        pl.BlockSpec((blk, LANES), lambda b: (b, COL_DT // LANES)),
        pl.BlockSpec((nsq, SUBLANES, CONV_DIM), lambda b: (b, 0, 0)),
        pl.BlockSpec((None, nsq, N_SSD_HEADS, SSD_HEAD_DIM, D_STATE), lambda b: (layer, b, 0, 0, 0)),
    ] + _param_specs(params)
    return pl.pallas_call(
        _ssd_sample_kernel,
        grid=(nseq // nsq,),
        in_specs=in_specs,
        out_specs=[
            pl.BlockSpec((blk, D_SSD), lambda b: (b, 0)),
            pl.BlockSpec((nsq, N_SSD_HEADS, SSD_HEAD_DIM, D_STATE), lambda b: (b, 0, 0, 0)),
        ],
        out_shape=[
            jax.ShapeDtypeStruct((nseq * t, D_SSD), BF16),
            jax.ShapeDtypeStruct((nseq, N_SSD_HEADS, SSD_HEAD_DIM, D_STATE), F32),
        ],
        scratch_shapes=[pltpu.VMEM((2 * blk, CONV_DIM), F32), pltpu.VMEM((blk, D_SSD), F32)],
        compiler_params=_cparams(1),
        name="ssd_sample",
    )(proj_tail, proj_tail, proj_tail, conv_prev, h0, *params)


def _qk_prep(x, w, cos, sin):
    lane = lax.broadcasted_iota(jnp.int32, x.shape, 1)
    lo_head = lane < HEAD_DIM
    sq = x * x
    s_lo = jnp.sum(jnp.where(lo_head, sq, 0.0), axis=-1, keepdims=True)
    s_all = jnp.sum(sq, axis=-1, keepdims=True)
    ms = jnp.where(lo_head, s_lo, s_all - s_lo) * (1.0 / HEAD_DIM)
    xn = x * lax.rsqrt(ms + EPS) * w
    half = HEAD_DIM // 2
    first_half = (lane % HEAD_DIM) < half
    partner = jnp.where(first_half, pltpu.roll(xn, LANES - half, 1), pltpu.roll(xn, half, 1))
    return xn * cos + partner * sin


def _q_groups(q, qnw_ref, cos, sin):
    return [_qk_prep(q[:, g * LANES:(g + 1) * LANES], qnw_ref[:, g * LANES:(g + 1) * LANES], cos, sin)
            * ATTN_SCALE for g in range(D_ATTN // LANES)]


def _sink_column(snk_ref, rows):
    return jnp.concatenate([jnp.broadcast_to(snk_ref[:, h:h + 1], (rows, 1)) for h in range(N_Q_HEADS)],
                           axis=0)


def _softmax_weights(pieces, sk):
    top = pieces[0]
    for p in pieces[1:]:
        top = jnp.maximum(top, p)
    m = jnp.maximum(jnp.max(top, axis=-1, keepdims=True), sk)
    e = [jnp.exp(p - m) for p in pieces]
    tot = e[0]
    for p in e[1:]:
        tot = tot + p
    return e, 1.0 / (jnp.sum(tot, axis=-1, keepdims=True) + jnp.exp(sk - m))


def _kv_prep_t(k, v, knw_ref, cos, sin):
    kp = _qk_prep(k, knw_ref[...], cos, sin)
    lane = lax.broadcasted_iota(jnp.int32, kp.shape, 1)
    return (kp, jnp.where(lane < HEAD_DIM, kp, 0.0).astype(BF16),
            jnp.where(lane < HEAD_DIM, 0.0, kp).astype(BF16), v.T.astype(BF16))


def _scores_t(qs, key_lo, key_hi):
    return jnp.concatenate([_dot_nt(key_lo, qs), _dot_nt(key_hi, qs)], axis=1)


def _softmax_weights_t(pieces, sk):
    m = sk
    for p in pieces:
        m = jnp.maximum(m, jnp.max(p, axis=0, keepdims=True))
    e = [jnp.exp(p - m) for p in pieces]
    den = jnp.exp(sk - m)
    for p in e:
        den = den + jnp.sum(p, axis=0, keepdims=True)
    return e, 1.0 / den


def _meta_weights(e_m):
    return jnp.concatenate([e_m.astype(BF16), jnp.zeros((CHUNK - N_META, e_m.shape[1]), BF16)], axis=0)


def _attn_out_t(o_t, anw_ref):
    nst = Q_PER_KV * CHUNK
    d = lax.broadcasted_iota(jnp.int32, (LANES, CHUNK), 0)
    groups = [jnp.where(d < HEAD_DIM, o_t[:, g * CHUNK:(g + 1) * CHUNK],
                        o_t[:, nst + g * CHUNK:nst + (g + 1) * CHUNK]).T for g in range(Q_PER_KV)]
    return _rms(jnp.concatenate(groups, axis=1), anw_ref[...])


N_ATTN_IN, N_ATTN_OUT, N_ATTN_SCRATCH = 14, 5, 8


def _attn_prompt_phases(q_m, k_m, v_m, q_t, k_t, v_t, cos_m, sin_m, cos_t, sin_t, qnw, knw, snk, anw,
                        o_o, om_o, kpm_o, kl_o, vl_o,
                        km_lo, km_hi, vmt, kp_lo, kp_hi, vpt, kmf, ombuf):
    c = pl.program_id(1)
    pad = CHUNK - N_META
    cols = N_Q_HEADS * CHUNK
    nblk = q_m.shape[0] // CHUNK
    carry = {}

    def sink_row():
        return jnp.concatenate([jnp.broadcast_to(snk[:, h:h + 1], (1, CHUNK)) for h in range(N_Q_HEADS)],
                               axis=1)

    def meta():
        sk = sink_row()

        def stage(meta_ref):
            return jnp.concatenate([jnp.zeros((pad, meta_ref.shape[1]), F32), meta_ref[...]], axis=0)

        cs, sn = cos_t[...], sin_t[...]
        kp = _qk_prep(stage(k_t), knw[...], cs, sn)
        lane = lax.broadcasted_iota(jnp.int32, (N_META, LANES), 1)
        km_lo[...] = jnp.where(lane < HEAD_DIM, kp[pad:, :], 0.0).astype(BF16)
        km_hi[...] = jnp.where(lane < HEAD_DIM, 0.0, kp[pad:, :]).astype(BF16)
        kmf[...] = kp[pad:, :]
        vmt[...] = jnp.concatenate([v_t[...], jnp.zeros((pad, LANES), F32)], axis=0).T.astype(BF16)
        kp_lo[...] = jnp.zeros(kp_lo.shape, BF16)
        kp_hi[...] = jnp.zeros(kp_hi.shape, BF16)
        vpt[...] = jnp.zeros(vpt.shape, BF16)
        qs = jnp.concatenate(_q_groups(stage(q_t), qnw, cs, sn), axis=0).astype(BF16)
        r = lax.broadcasted_iota(jnp.int32, (N_META, cols), 0)
        qi = lax.broadcasted_iota(jnp.int32, (N_META, cols), 1) & (CHUNK - 1)
        s_m = jnp.where(r <= qi - pad, _scores_t(qs, km_lo[...], km_hi[...]), NEG)
        (e_m,), inv = _softmax_weights_t([s_m], sk)
        ombuf[...] = _attn_out_t(_dot(vmt[...], _meta_weights(e_m)) * inv, anw)[pad:, :]

    def begin():
        carry['prev'] = (kp_lo[...], kp_hi[...], vpt[...])
        carry['sk'] = sink_row()
        r = lax.broadcasted_iota(jnp.int32, (CHUNK, cols), 0)
        qi = lax.broadcasted_iota(jnp.int32, (CHUNK, cols), 1) & (CHUNK - 1)
        carry['tri'] = r <= qi

    def block(j):
        prev, tri = carry['prev'], carry['tri']
        rows = slice(j * CHUNK, (j + 1) * CHUNK)
        cs, sn = cos_m[rows, :], sin_m[rows, :]
        v = v_m[rows, :]
        kp, k_lo, k_hi, v_tb = _kv_prep_t(k_m[rows, :], v, knw, cs, sn)
        qs = jnp.concatenate(_q_groups(q_m[rows, :], qnw, cs, sn), axis=0).astype(BF16)
        band = jnp.where(tri, _scores_t(qs, k_lo, k_hi), _scores_t(qs, prev[0], prev[1]))
        if j == 0:
            band = jnp.where(jnp.logical_or(tri, c > 1), band, NEG)
        (e_b, e_m), inv = _softmax_weights_t([band, _scores_t(qs, km_lo[...], km_hi[...])], carry['sk'])
        o_t = (_dot(v_tb, jnp.where(tri, e_b, 0.0).astype(BF16))
               + _dot(prev[2], jnp.where(tri, 0.0, e_b).astype(BF16))
               + _dot(vmt[...], _meta_weights(e_m))) * inv
        o_o[rows, :] = _attn_out_t(o_t, anw).astype(BF16)
        carry['prev'] = (k_lo, k_hi, v_tb)
        if j == nblk - 1:
            kl_o[...] = kp
            vl_o[...] = v

    def reset():
        kp_lo[...] = jnp.zeros(kp_lo.shape, BF16)
        kp_hi[...] = jnp.zeros(kp_hi.shape, BF16)
        vpt[...] = jnp.zeros(vpt.shape, BF16)

    def end():
        kp_lo[...], kp_hi[...], vpt[...] = carry['prev']

    def every():
        om_o[...] = ombuf[...].astype(BF16)
        kpm_o[...] = kmf[...]

    return meta, begin, block, end, every, reset


def _mixer_prompt_kernel(*refs):
    n_in, n_out = N_SSD_IN + N_ATTN_IN, N_SSD_OUT + N_ATTN_OUT
    ins, outs, scratch = refs[:n_in], refs[n_in:n_in + n_out], refs[n_in + n_out:]
    s_meta, s_block, s_every, s_save, s_restore = _ssd_prompt_phases(
        *ins[:N_SSD_IN], *outs[:N_SSD_OUT], *scratch[:N_SSD_SCRATCH])
    a_meta, a_begin, a_block, a_end, a_every, a_reset = _attn_prompt_phases(
        *ins[N_SSD_IN:], *outs[N_SSD_OUT:], *scratch[N_SSD_SCRATCH:])
    b = pl.program_id(0)
    c = pl.program_id(1)

    @pl.when(jnp.logical_and(c == 0, b == 0))
    def _():
        s_meta()
        a_meta()
        s_save()

    @pl.when(jnp.logical_and(c == 0, b > 0))
    def _():
        s_restore()
        a_reset()

    @pl.when(c > 0)
    def _():
        a_begin()
        for j in range(ins[0].shape[0] // CHUNK):
            s_block(j)
            a_block(j)
        a_end()

    s_every()
    a_every()


def _attn_sample_kernel(q_s, k_s, v_s, mk, mv, wk, wv, cos, sin, qnw, knw, snk, anw,
                        o_o, wk_o, wv_o, obuf):
    nsq = mk.shape[0]
    t = SUBLANES
    nst = Q_PER_KV * t
    cs, sn = cos[...], sin[...]
    kp = _qk_prep(k_s[...], knw[...], cs, sn)
    v = v_s[...]
    qg = _q_groups(q_s[...], qnw, cs, sn)

    lo = lax.broadcasted_iota(jnp.int32, (nst, LANES), 1) < HEAD_DIM
    i_q = lax.broadcasted_iota(jnp.int32, (2 * nst, LANES), 0) & (t - 1)
    cj = lax.broadcasted_iota(jnp.int32, (2 * nst, LANES), 1)
    mask_a = cj > i_q
    mask_b = jnp.logical_or(cj <= i_q, jnp.logical_and(cj >= t, cj < t + N_META))
    sk = _sink_column(snk, t)
    zpad = jnp.zeros((WINDOW - t - N_META, LANES), F32)
    for s in range(nsq):
        rows = slice(s * t, (s + 1) * t)
        q_st = jnp.concatenate([g[rows, :] for g in qg], axis=0)
        q2 = jnp.concatenate([jnp.where(lo, q_st, 0.0), jnp.where(lo, 0.0, q_st)], axis=0).astype(BF16)
        k_b = jnp.concatenate([kp[rows, :], mk[s], zpad], axis=0).astype(BF16)
        v_b = jnp.concatenate([v[rows, :], mv[s], zpad], axis=0).astype(BF16)
        (e_a, e_b), inv = _softmax_weights(
            [jnp.where(mask_a, _dot_nt(q2, wk[s].astype(BF16)), NEG),
             jnp.where(mask_b, _dot_nt(q2, k_b), NEG)], sk)
        o2 = (_dot(e_a.astype(BF16), wv[s].astype(BF16)) + _dot(e_b.astype(BF16), v_b)) * inv
        o_st = jnp.where(lo, o2[0:nst, :], o2[nst:2 * nst, :])
        for g in range(Q_PER_KV):
            obuf[rows, g * LANES:(g + 1) * LANES] = o_st[g * t:(g + 1) * t, :]
        wk_o[s] = jnp.concatenate([wk[s, t:, :], kp[rows, :]], axis=0)
        wv_o[s] = jnp.concatenate([wv[s, t:, :], v[rows, :]], axis=0)
    o_o[...] = _rms(obuf[...], anw[...]).astype(BF16)


def _mixer_prompt(proj_main, proj_tail, ssd_params, tabs_main, tabs_meta, attn_params, bsz, seq, meta_row):
    blk = _chunks_per_step(seq // CHUNK) * CHUNK
    nb = seq // blk
    mb = meta_row // N_META

    def main(col):
        return lambda b, c: (b * nb + jnp.maximum(c - 1, 0), col)

    def meta(col):
        return lambda b, c: (mb, col)

    per_batch = lambda b, c: (b, 0, 0)
    in_specs = [
        pl.BlockSpec((blk, CONV_DIM), main(COL_XBC // CONV_DIM)),
        pl.BlockSpec((blk, D_SSD), main(COL_Z // D_SSD)),
        pl.BlockSpec((blk, LANES), main(COL_DT // LANES)),
        pl.BlockSpec((N_META, CONV_DIM), meta(COL_XBC // CONV_DIM)),
        pl.BlockSpec((N_META, D_SSD), meta(COL_Z // D_SSD)),
        pl.BlockSpec((N_META, LANES), meta(COL_DT // LANES)),
    ] + _param_specs(ssd_params) + [
        pl.BlockSpec((blk, D_ATTN), main(COL_Q // D_ATTN)),
        pl.BlockSpec((blk, KV_DIM), main(COL_K // KV_DIM)),
        pl.BlockSpec((blk, KV_DIM), main(COL_V // KV_DIM)),
        pl.BlockSpec((N_META, D_ATTN), meta(COL_Q // D_ATTN)),
        pl.BlockSpec((N_META, KV_DIM), meta(COL_K // KV_DIM)),
        pl.BlockSpec((N_META, KV_DIM), meta(COL_V // KV_DIM)),
        pl.BlockSpec((blk, LANES), lambda b, c: (jnp.maximum(c - 1, 0), 0)),
        pl.BlockSpec((blk, LANES), lambda b, c: (jnp.maximum(c - 1, 0), 0)),
        pl.BlockSpec((CHUNK, LANES), lambda b, c: (0, 0)),
        pl.BlockSpec((CHUNK, LANES), lambda b, c: (0, 0)),
    ] + _param_specs(attn_params)
    assert len(in_specs) == N_SSD_IN + N_ATTN_IN
    return pl.pallas_call(
        _mixer_prompt_kernel,
        grid=(bsz, nb + 1),
        in_specs=in_specs,
        out_specs=[
            pl.BlockSpec((blk, D_SSD), main(0)),
            pl.BlockSpec((None, N_META, D_SSD), per_batch),
            pl.BlockSpec((None, N_SSD_HEADS, SSD_HEAD_DIM, D_STATE), lambda b, c: (b, 0, 0, 0)),
            pl.BlockSpec((None, SUBLANES, CONV_DIM), per_batch),
            pl.BlockSpec((blk, D_ATTN), main(0)),
            pl.BlockSpec((None, N_META, D_ATTN), per_batch),
            pl.BlockSpec((None, N_META, KV_DIM), per_batch),
            pl.BlockSpec((None, CHUNK, KV_DIM), per_batch),
            pl.BlockSpec((None, CHUNK, KV_DIM), per_batch),
        ],
        out_shape=[
            jax.ShapeDtypeStruct((bsz * seq, D_SSD), BF16),
            jax.ShapeDtypeStruct((bsz, N_META, D_SSD), BF16),
            jax.ShapeDtypeStruct((bsz, N_SSD_HEADS, SSD_HEAD_DIM, D_STATE), F32),
            jax.ShapeDtypeStruct((bsz, SUBLANES, CONV_DIM), F32),
            jax.ShapeDtypeStruct((bsz * seq, D_ATTN), BF16),
            jax.ShapeDtypeStruct((bsz, N_META, D_ATTN), BF16),
            jax.ShapeDtypeStruct((bsz, N_META, KV_DIM), F32),
            jax.ShapeDtypeStruct((bsz, CHUNK, KV_DIM), F32),
            jax.ShapeDtypeStruct((bsz, CHUNK, KV_DIM), F32),
        ],
        scratch_shapes=[
            pltpu.VMEM((SUBLANES, CONV_DIM), F32), pltpu.VMEM((N_SSD_HEADS * SSD_HEAD_DIM, D_STATE), F32),
            pltpu.VMEM((N_META, D_SSD), F32),
            pltpu.VMEM((SUBLANES, CONV_DIM), F32), pltpu.VMEM((N_SSD_HEADS * SSD_HEAD_DIM, D_STATE), F32)] + [
            pltpu.VMEM((N_META, KV_DIM), BF16) for _ in range(2)] + [
            pltpu.VMEM((CHUNK, KV_DIM), BF16) for _ in range(4)] + [
            pltpu.VMEM((N_META, KV_DIM), F32), pltpu.VMEM((N_META, D_ATTN), F32),
        ],
        compiler_params=_cparams(2),
        name="mixer_prompt",
    )(proj_main, proj_main, proj_main, proj_tail, proj_tail, proj_tail, *ssd_params,
      proj_main, proj_main, proj_main, proj_tail, proj_tail, proj_tail, *tabs_main, *tabs_meta, *attn_params)


def _attn_sample(proj_tail, mk, mv, wk, wv, layer, cos, sin, params, nseq, t, nsq):
    blk = nsq * t
    cache = lambda b: (layer, b, 0, 0)
    in_specs = [
        pl.BlockSpec((blk, D_ATTN), lambda b: (b, COL_Q // D_ATTN)),
        pl.BlockSpec((blk, KV_DIM), lambda b: (b, COL_K // KV_DIM)),
        pl.BlockSpec((blk, KV_DIM), lambda b: (b, COL_V // KV_DIM)),
        pl.BlockSpec((None, nsq, N_META, KV_DIM), cache),
        pl.BlockSpec((None, nsq, N_META, KV_DIM), cache),
        pl.BlockSpec((None, nsq, WINDOW, KV_DIM), cache),
        pl.BlockSpec((None, nsq, WINDOW, KV_DIM), cache),
        pl.BlockSpec((blk, LANES), lambda b: (0, 0)),
        pl.BlockSpec((blk, LANES), lambda b: (0, 0)),
    ] + _param_specs(params)
    return pl.pallas_call(
        _attn_sample_kernel,
        grid=(nseq // nsq,),
        in_specs=in_specs,
        out_specs=[
            pl.BlockSpec((blk, D_ATTN), lambda b: (b, 0)),
            pl.BlockSpec((nsq, WINDOW, KV_DIM), lambda b: (b, 0, 0)),
            pl.BlockSpec((nsq, WINDOW, KV_DIM), lambda b: (b, 0, 0)),
        ],
        out_shape=[
            jax.ShapeDtypeStruct((nseq * t, D_ATTN), BF16),
            jax.ShapeDtypeStruct((nseq, WINDOW, KV_DIM), F32),
            jax.ShapeDtypeStruct((nseq, WINDOW, KV_DIM), F32),
        ],
        scratch_shapes=[pltpu.VMEM((blk, D_ATTN), F32)],
        compiler_params=_cparams(1),
        name="attn_sample",
    )(proj_tail, proj_tail, proj_tail, mk, mv, wk, wv, cos, sin, *params)


def _mix_out(n_main, x_refs, ys_refs, ya_refs, wo_ref):
    return (_pick(n_main, *x_refs) + _dot(_pick(n_main, *ys_refs), wo_ref[0:D_SSD, :])
            + _dot(_pick(n_main, *ya_refs), wo_ref[D_SSD:D_SSD + D_ATTN, :]))


def _swiglu_acc(hn, wg_ref, wu_ref, wd_ref, acc_ref):
    for j in range(D_FF // FF_CHUNK):
        cols = slice(j * FF_CHUNK, (j + 1) * FF_CHUNK)
        a = (_silu(_dot(hn, wg_ref[:, cols])) * _dot(hn, wu_ref[:, cols])).astype(BF16)
        acc_ref[...] += _dot(a, wd_ref[cols, :])


def _out_ffn_kernel(n_main, xa, xb, ysa, ysb, yaa, yab, wo_ref, nw_ref, wg_ref, wu_ref, wd_ref,
                    om_ref, ot_ref):
    xm = _mix_out(n_main, (xa, xb), (ysa, ysb), (yaa, yab), wo_ref)
    om_ref[...] = xm
    _swiglu_acc(_rms(xm, nw_ref[...]).astype(BF16), wg_ref, wu_ref, wd_ref, om_ref)
    ot_ref[...] = om_ref[...]


def _out_ffn(x, ys, ya, wo, nw, wg, wu, wd, n_main):
    n_tail = x[1].shape[0] // ROW_TILE
    return pl.pallas_call(
        functools.partial(_out_ffn_kernel, n_main),
        grid=(n_main + n_tail,),
        in_specs=_src_specs(D_MODEL, n_main) + _src_specs(D_SSD, n_main) + _src_specs(D_ATTN, n_main) + [
            _resident(wo.shape), _resident(nw.shape),
            _resident(wg.shape), _resident(wu.shape), _resident(wd.shape),
        ],
        out_specs=_dst_specs(D_MODEL, n_main),
        out_shape=_dst_shapes(D_MODEL, n_main, n_tail, F32),
        compiler_params=_cparams(1),
        name="out_ffn",
    )(*x, *ys, *ya, wo, nw, wg, wu, wd)


def _out_router_kernel(n_main, n_tok, xa, xb, ysa, ysb, yaa, yab, wo_ref, nw_ref, wr_hi_ref, wr_lo_ref,
                       before_ref, xm_o, rt_o, cnt_o):
    xm = _mix_out(n_main, (xa, xb), (ysa, ysb), (yaa, yab), wo_ref)
    xm_o[...] = xm
    hn = _rms(xm, nw_ref[...])
    hi = hn.astype(BF16)
    lo = (hn - hi.astype(F32)).astype(BF16)
    logits = _dot(hi, wr_hi_ref[...]) + _dot(lo, wr_hi_ref[...]) + _dot(hi, wr_lo_ref[...])
    lane = lax.broadcasted_iota(jnp.int32, logits.shape, 1)
    logits = jnp.where(lane < N_EXPERTS, logits, -jnp.inf)
    v1 = jnp.max(logits, axis=-1, keepdims=True)
    i1 = jnp.min(jnp.where(logits == v1, lane, LANES), axis=-1, keepdims=True)
    rest = jnp.where(lane == i1, -jnp.inf, logits)
    v2 = jnp.max(rest, axis=-1, keepdims=True)
    i2 = jnp.min(jnp.where(rest == v2, lane, LANES), axis=-1, keepdims=True)
    e2 = jnp.exp(v2 - v1)
    g1 = 1.0 / (1.0 + e2)
    g2 = e2 / (1.0 + e2)

    row = pl.program_id(0) * ROW_TILE + lax.broadcasted_iota(jnp.int32, (ROW_TILE, 1), 0)
    valid = row < n_tok
    oh1 = jnp.where(jnp.logical_and(lane == i1, valid), 1.0, 0.0)
    oh2 = jnp.where(jnp.logical_and(lane == i2, valid), 1.0, 0.0)
    before = before_ref[...]
    c1 = _dot(before, oh1.astype(BF16))
    c2 = _dot(before, oh2.astype(BF16))
    tot1 = jnp.sum(oh1, axis=0, keepdims=True)
    tot2 = jnp.sum(oh2, axis=0, keepdims=True)
    rank1 = jnp.sum(jnp.where(lane == i1, c1, 0.0), axis=-1, keepdims=True)
    rank2 = jnp.sum(jnp.where(lane == i2, c2 + tot1, 0.0), axis=-1, keepdims=True)
    cnt_o[...] = jnp.broadcast_to(tot1 + tot2, cnt_o.shape)
    route = jnp.where(lane == 0, i1.astype(F32), 0.0)
    for k, val in enumerate((i2.astype(F32), g1, g2, rank1, rank2)):
        route = jnp.where(lane == k + 1, val, route)
    rt_o[...] = route


def _out_router(x, ys, ya, wo, nw, wr_hi, wr_lo, n_main, n_tok):
    n_tiles = n_main + x[1].shape[0] // ROW_TILE
    rows = n_tiles * ROW_TILE
    before = jnp.tril(jnp.ones((ROW_TILE, ROW_TILE), BF16), -1)
    return pl.pallas_call(
        functools.partial(_out_router_kernel, n_main, n_tok),
        grid=(n_tiles,),
        in_specs=_src_specs(D_MODEL, n_main) + _src_specs(D_SSD, n_main) + _src_specs(D_ATTN, n_main) + [
            _resident(wo.shape), _resident(nw.shape), _resident(wr_hi.shape), _resident(wr_lo.shape),
            _resident(before.shape),
        ],
        out_specs=[
            pl.BlockSpec((ROW_TILE, D_MODEL), lambda i: (i, 0)),
            pl.BlockSpec((ROW_TILE, LANES), lambda i: (i, 0)),
            pl.BlockSpec((None, SUBLANES, LANES), lambda i: (i, 0, 0)),
        ],
        out_shape=[
            jax.ShapeDtypeStruct((rows, D_MODEL), F32),
            jax.ShapeDtypeStruct((rows, LANES), F32),
            jax.ShapeDtypeStruct((n_tiles, SUBLANES, LANES), F32),
        ],
        compiler_params=_cparams(1),
        name="out_router",
    )(*x, *ys, *ya, wo, nw, wr_hi, wr_lo, before)


def _tile_rows(idx, n_tiles, tail, fn):
    if tail == ROW_TILE:
        fn(ROW_TILE)
    else:
        pl.when(idx < n_tiles - 1)(lambda: fn(ROW_TILE))
        pl.when(idx == n_tiles - 1)(lambda: fn(tail))


def _dispatch_kernel(n_tiles, tail, zt_ref, dest_ref, x_ref, xs_ref, zbuf, sem, zsem):
    @pl.when(pl.program_id(0) == 0)
    def _():
        zbuf[...] = jnp.zeros(zbuf.shape, F32)

        def zero_copy(j):
            start = pl.multiple_of(zt_ref[j] * ROW_TILE, ROW_TILE)
            return pltpu.make_async_copy(zbuf, xs_ref.at[pl.ds(start, ROW_TILE)], zsem)

        for j in range(zt_ref.shape[0]):
            pl.when(zt_ref[j] >= 0)(lambda j=j: zero_copy(j).start())
        for j in range(zt_ref.shape[0]):
            pl.when(zt_ref[j] >= 0)(lambda j=j: zero_copy(j).wait())

    def run(nrows):
        def body(r, carry):
            for k in range(TOP_K):
                d = dest_ref[0, 0, TOP_K * r + k]
                pltpu.make_async_copy(x_ref.at[pl.ds(r, 1)], xs_ref.at[pl.ds(d, 1)], sem).start()
            return carry

        lax.fori_loop(0, nrows, body, 0, unroll=DMA_UNROLL)
        for k in range(TOP_K):
            pltpu.make_async_copy(x_ref.at[pl.ds(0, nrows)], xs_ref.at[pl.ds(0, nrows)], sem).wait()

    _tile_rows(pl.program_id(0), n_tiles, tail, run)


def _dispatch(last_tile, dest, xm, n_tok, m_rows):
    n_tiles = xm.shape[0] // ROW_TILE
    tail = n_tok - (n_tiles - 1) * ROW_TILE
    grid_spec = pltpu.PrefetchScalarGridSpec(
        num_scalar_prefetch=1,
        grid=(n_tiles,),
        in_specs=[
            pl.BlockSpec((1, 1, TOP_K * ROW_TILE), lambda i, lt: (i, 0, 0), memory_space=pltpu.SMEM),
            pl.BlockSpec((ROW_TILE, D_MODEL), lambda i, lt: (i, 0)),
        ],
        out_specs=pl.BlockSpec(memory_space=pl.ANY),
        scratch_shapes=[pltpu.VMEM((ROW_TILE, D_MODEL), F32), pltpu.SemaphoreType.DMA(()),
                        pltpu.SemaphoreType.DMA(())],
    )
    return pl.pallas_call(
        functools.partial(_dispatch_kernel, n_tiles, tail),
        grid_spec=grid_spec,
        out_shape=jax.ShapeDtypeStruct((m_rows, D_MODEL), F32),
        compiler_params=_cparams(1, has_side_effects=True),
        name="moe_dispatch",
    )(last_tile, dest, xm)


def _moe_kernel(te_ref, nu_ref, x_ref, nw_ref, wg_ref, wu_ref, wd_ref, o_ref):
    i = pl.program_id(0)
    o_ref[...] = jnp.zeros(o_ref.shape, F32)

    @pl.when(i < nu_ref[0])
    def _():
        _swiglu_acc(_rms(x_ref[...], nw_ref[...]).astype(BF16), wg_ref, wu_ref, wd_ref, o_ref)


def _moe_experts(tile_e, n_used, xs, nw, wg, wu, wd):
    rows = xs.shape[0]
    grid_spec = pltpu.PrefetchScalarGridSpec(
        num_scalar_prefetch=2,
        grid=(rows // ROW_TILE,),
        in_specs=[
            pl.BlockSpec((ROW_TILE, D_MODEL), lambda i, te, nu: (jnp.maximum(jnp.minimum(i, nu[0] - 1), 0), 0)),
            pl.BlockSpec((1, D_MODEL), lambda i, te, nu: (0, 0)),
            pl.BlockSpec((None, D_MODEL, D_FF), lambda i, te, nu: (te[i], 0, 0)),
            pl.BlockSpec((None, D_MODEL, D_FF), lambda i, te, nu: (te[i], 0, 0)),
            pl.BlockSpec((None, D_FF, D_MODEL), lambda i, te, nu: (te[i], 0, 0)),
        ],
        out_specs=pl.BlockSpec((ROW_TILE, D_MODEL), lambda i, te, nu: (i, 0)),
    )
    return pl.pallas_call(
        _moe_kernel,
        grid_spec=grid_spec,
        out_shape=jax.ShapeDtypeStruct((rows, D_MODEL), F32),
        compiler_params=_cparams(1),
        name="moe_experts",
    )(tile_e, n_used, xs, nw, wg, wu, wd)


def _combine_kernel(n_main, n_tiles, tail, dcur_ref, dnext_ref, xm_ref, rt_ref, yb_ref,
                    om_ref, ot_ref, gbuf, sem):
    i = pl.program_id(0)

    def issue(dref, slot, nrows):
        def body(r, carry):
            for k in range(TOP_K):
                d = dref[0, 0, TOP_K * r + k]
                pltpu.make_async_copy(yb_ref.at[pl.ds(d, 1)], gbuf.at[slot, k, pl.ds(r, 1)],
                                      sem.at[slot]).start()
            return carry

        lax.fori_loop(0, nrows, body, 0, unroll=DMA_UNROLL)

    def wait(slot, nrows):
        for k in range(TOP_K):
            pltpu.make_async_copy(yb_ref.at[pl.ds(0, nrows)], gbuf.at[slot, k, pl.ds(0, nrows)],
                                  sem.at[slot]).wait()

    @pl.when(i == 0)
    def _():
        gbuf[...] = jnp.zeros(gbuf.shape, F32)
        _tile_rows(i, n_tiles, tail, lambda n: issue(dcur_ref, 0, n))

    for slot in range(2):
        @pl.when(jnp.logical_and(i + 1 < n_tiles, (i + 1) % 2 == slot))
        def _(slot=slot):
            _tile_rows(i + 1, n_tiles, tail, lambda n: issue(dnext_ref, slot, n))

    for slot in range(2):
        @pl.when(i % 2 == slot)
        def _(slot=slot):
            _tile_rows(i, n_tiles, tail, lambda n: wait(slot, n))
            gates = rt_ref[...]
            val = (xm_ref[...] + gates[:, TOP_K:TOP_K + 1] * gbuf[slot, 0]
                   + gates[:, TOP_K + 1:TOP_K + 2] * gbuf[slot, 1])

            @pl.when(i < n_main)
            def _():
                om_ref[...] = val

            @pl.when(i >= n_main)
            def _():
                ot_ref[...] = val


def _combine(dest, xm, route, yb, n_main, n_tok):
    n_tiles = xm.shape[0] // ROW_TILE
    tail = n_tok - (n_tiles - 1) * ROW_TILE
    dspec = lambda f: pl.BlockSpec((1, 1, TOP_K * ROW_TILE), f, memory_space=pltpu.SMEM)
    return pl.pallas_call(
        functools.partial(_combine_kernel, n_main, n_tiles, tail),
        grid=(n_tiles,),
        in_specs=[
            dspec(lambda i: (i, 0, 0)),
            dspec(lambda i: (jnp.minimum(i + 1, n_tiles - 1), 0, 0)),
            pl.BlockSpec((ROW_TILE, D_MODEL), lambda i: (i, 0)),
            pl.BlockSpec((ROW_TILE, LANES), lambda i: (i, 0)),
            pl.BlockSpec(memory_space=pl.ANY),
        ],
        out_specs=_src_specs(D_MODEL, n_main),
        out_shape=[jax.ShapeDtypeStruct((n_main * ROW_TILE, D_MODEL), F32),
                   jax.ShapeDtypeStruct(((n_tiles - n_main) * ROW_TILE, D_MODEL), F32)],
        scratch_shapes=[pltpu.VMEM((2, TOP_K, ROW_TILE, D_MODEL), F32), pltpu.SemaphoreType.DMA((2,))],
        compiler_params=_cparams(1),
        name="moe_combine",
    )(dest, dest, xm, route, yb)


def _moe_layer(xm, route, counts, n_main, n_tok, nw, wg, wu, wd):
    n_tiles = xm.shape[0] // ROW_TILE
    m_tiles = -(-(n_tok * TOP_K + N_EXPERTS * (ROW_TILE - 1)) // ROW_TILE)
    cnt = counts[:, 0, :N_EXPERTS].astype(jnp.int32)
    total = jnp.sum(cnt, axis=0)
    padded = (total + ROW_TILE - 1) // ROW_TILE * ROW_TILE
    pad_end = jnp.cumsum(padded)
    base = (pad_end - padded)[None, :] + jnp.cumsum(cnt, axis=0) - cnt
    e = route[:, 0:TOP_K].astype(jnp.int32).reshape(n_tiles, ROW_TILE, TOP_K)
    rank = route[:, 2 * TOP_K:3 * TOP_K].astype(jnp.int32).reshape(n_tiles, ROW_TILE, TOP_K)
    onehot = e[..., None] == jnp.arange(N_EXPERTS, dtype=jnp.int32)
    dest = jnp.sum(jnp.where(onehot, base[:, None, None, :], 0), axis=-1) + rank
    dest = dest.reshape(n_tiles, 1, ROW_TILE * TOP_K)
    tile_start = jnp.arange(m_tiles, dtype=jnp.int32) * ROW_TILE
    tile_e = jnp.minimum(jnp.sum((pad_end[None, :] <= tile_start[:, None]).astype(jnp.int32), axis=1),
                         N_EXPERTS - 1)
    n_used = (pad_end[-1:] // ROW_TILE).astype(jnp.int32)
    last_tile = jnp.where(padded > 0, pad_end // ROW_TILE - 1, -1)
    spare = n_used[0] + jnp.arange(m_tiles - (n_tok * TOP_K) // ROW_TILE, dtype=jnp.int32)
    zero_tiles = jnp.concatenate([last_tile, jnp.where(spare < m_tiles, spare, -1)]).astype(jnp.int32)
    xs = _dispatch(zero_tiles, dest, xm, n_tok, m_tiles * ROW_TILE)
    yb = _moe_experts(tile_e, n_used, xs, nw, wg, wu, wd)
    return _combine(dest, xm, route, yb, n_main, n_tok)


def _rope_tables(pos):
    half = HEAD_DIM // 2
    inv_freq = ROPE_THETA ** (-jnp.arange(half, dtype=F32) / half)
    ang = pos.astype(F32)[:, None] * inv_freq[None, :]
    cos = jnp.cos(ang)
    sin = jnp.sin(ang)
    reps = LANES // HEAD_DIM
    return (jnp.tile(jnp.concatenate([cos, cos], axis=-1), (1, reps)),
            jnp.tile(jnp.concatenate([-sin, sin], axis=-1), (1, reps)))


def _pad_lanes(v, width=LANES):
    v = v.astype(F32).reshape(1, -1)
    return jnp.pad(v, ((0, 0), (0, width - v.shape[1])))


def kernel(x_prompt, x_sample, state_ssm, state_conv, cache_meta_k, cache_meta_v, cache_win_k, cache_win_v, meta_tokens, norm_mix_w, w_in, conv_w, conv_b, dt_bias, a_log, d_skip, ssd_norm_w, q_norm_w, k_norm_w, sinks, attn_norm_w, w_out, norm_ffn_w, w_gate, w_up, w_down, w_router, moe_w_gate, moe_w_up, moe_w_down):
    bsz, seq, _ = x_prompt.shape
    nseq, t_s, _ = x_sample.shape
    depth = w_in.shape[0]
    r_main = bsz * seq
    r_samp = nseq * t_s
    assert seq % CHUNK == 0 and t_s == SUBLANES and r_main % ROW_TILE == 0 and r_samp % N_META == 0
    n_main = r_main // ROW_TILE
    n_tok = r_main + r_samp + N_META
    r_tail = -(-(r_samp + N_META) // ROW_TILE) * ROW_TILE
    tail_pad = r_tail - r_samp - N_META

    x = (x_prompt.reshape(r_main, D_MODEL),
         jnp.concatenate([x_sample.reshape(r_samp, D_MODEL), meta_tokens.astype(F32),
                          jnp.zeros((tail_pad, D_MODEL), F32)], axis=0))

    tabs_main = _rope_tables(N_META + jnp.arange(seq, dtype=jnp.int32))
    tabs_meta = _rope_tables(jnp.arange(CHUNK, dtype=jnp.int32) - (CHUNK - N_META))
    nsq = 16 if nseq % 16 == 0 else nseq
    cos_s, sin_s = (jnp.tile(tab, (nsq, 1)) for tab in
                    _rope_tables(PAST_LEN + jnp.arange(t_s, dtype=jnp.int32)))

    o_z, o_xbc, o_dt, o_q, o_k, o_v = 0, 512, 1536, 1544, 2056, 2184
    col = jnp.arange(D_ATTN, dtype=jnp.int32)
    grp, lane = col // LANES, col % LANES
    head_perm = (grp + Q_PER_KV * (lane // HEAD_DIM)) * HEAD_DIM + lane % HEAD_DIM

    def heads4(a):
        return a.reshape(a.shape[0], a.shape[1], N_KV_HEADS, HEAD_DIM)

    outs = {k: [] for k in ('p_ssm', 'p_conv', 'p_mk', 'p_mv', 'p_wk', 'p_wv', 's_ssm', 's_conv', 's_wk', 's_wv')}
    for l in range(depth):
        wl = w_in[l]
        w_re = jnp.concatenate([
            wl[:, o_xbc:o_xbc + CONV_DIM], wl[:, o_z:o_z + D_SSD], wl[:, o_q:o_q + D_ATTN][:, head_perm],
            wl[:, o_k:o_k + KV_DIM], wl[:, o_v:o_v + KV_DIM], wl[:, o_dt:o_dt + N_SSD_HEADS],
            jnp.zeros((D_MODEL, PROJ_W - COL_DT - N_SSD_HEADS), wl.dtype)], axis=1).astype(BF16)
        proj_main, proj_tail = _in_proj(x[0], x[1], norm_mix_w[l].reshape(1, D_MODEL).astype(F32), w_re, n_main)

        ssd_params = (conv_w[l].astype(F32), conv_b[l].reshape(1, CONV_DIM).astype(F32),
                      _pad_lanes(dt_bias[l]), _pad_lanes(-jnp.exp(a_log[l].astype(F32))),
                      jnp.repeat(d_skip[l].astype(F32), SSD_HEAD_DIM).reshape(1, D_SSD),
                      ssd_norm_w[l].reshape(1, D_SSD).astype(F32))
        attn_params = (jnp.tile(q_norm_w[l].astype(F32), N_Q_HEADS).reshape(1, D_ATTN),
                       jnp.tile(k_norm_w[l].astype(F32), N_KV_HEADS).reshape(1, KV_DIM),
                       _pad_lanes(sinks[l]), attn_norm_w[l][head_perm].reshape(1, D_ATTN).astype(F32))
        ys_p, ys_m, ssm_p, conv_p, ya_p, ya_m, kp_m, k_last, v_last = _mixer_prompt(
            proj_main, proj_tail, ssd_params, tabs_main, tabs_meta, attn_params, bsz, seq, r_samp)
        conv_prev = jnp.pad(state_conv[l].astype(F32), ((0, 0), (SUBLANES - (CONV_W - 1), 0), (0, 0)))
        ys_s, ssm_s = _ssd_sample(proj_tail, conv_prev, state_ssm.astype(F32), l, ssd_params, nseq, t_s, nsq)
        ya_s, wk_s, wv_s = _attn_sample(
            proj_tail, cache_meta_k.reshape(depth, nseq, N_META, KV_DIM).astype(F32),
            cache_meta_v.reshape(depth, nseq, N_META, KV_DIM).astype(F32),
            cache_win_k.reshape(depth, nseq, WINDOW, KV_DIM).astype(F32),
            cache_win_v.reshape(depth, nseq, WINDOW, KV_DIM).astype(F32),
            l, cos_s, sin_s, attn_params, nseq, t_s, nsq)

        ys = (ys_p, jnp.concatenate([ys_s, ys_m[0], jnp.zeros((tail_pad, D_SSD), BF16)], axis=0))
        ya = (ya_p, jnp.concatenate([ya_s, ya_m[0], jnp.zeros((tail_pad, D_ATTN), BF16)], axis=0))

        wo = jnp.concatenate([w_out[l][:D_SSD], w_out[l][D_SSD:][head_perm]], axis=0).astype(BF16)
        nfw = norm_ffn_w[l].reshape(1, D_MODEL).astype(F32)
        i = l // 2
        if l % 2 == 0:
            x = _out_ffn(x, ys, ya, wo, nfw, w_gate[i].astype(BF16), w_up[i].astype(BF16),
                         w_down[i].astype(BF16), n_main)
        else:
            wr = jnp.pad(w_router[i].astype(F32), ((0, 0), (0, LANES - N_EXPERTS)))
            wr_hi = wr.astype(BF16)
            wr_lo = (wr - wr_hi.astype(F32)).astype(BF16)
            xm, route, counts = _out_router(x, ys, ya, wo, nfw, wr_hi, wr_lo, n_main, n_tok)
            x = _moe_layer(xm, route, counts, n_main, n_tok, nfw, moe_w_gate[i].astype(BF16),
                           moe_w_up[i].astype(BF16), moe_w_down[i].astype(BF16))

        samp = proj_tail[:r_samp]
        xbc_s = samp[:, COL_XBC:COL_XBC + CONV_DIM].reshape(nseq, t_s, CONV_DIM)
        v_meta = proj_tail[r_samp:r_samp + N_META, COL_V:COL_V + KV_DIM]
        meta_shape = (bsz, N_META, N_KV_HEADS, HEAD_DIM)
        outs['p_ssm'].append(ssm_p)
        outs['p_conv'].append(conv_p[:, SUBLANES - (CONV_W - 1):])
        outs['p_mk'].append(jnp.broadcast_to(heads4(kp_m[0:1]), meta_shape))
        outs['p_mv'].append(jnp.broadcast_to(heads4(v_meta[None]), meta_shape))
        outs['p_wk'].append(heads4(k_last))
        outs['p_wv'].append(heads4(v_last))
        outs['s_ssm'].append(ssm_s)
        outs['s_conv'].append(jnp.concatenate([state_conv[l].astype(F32), xbc_s], axis=1)[:, t_s:])
        outs['s_wk'].append(heads4(wk_s))
        outs['s_wv'].append(heads4(wv_s))

    y_prompt = x[0][:r_main].reshape(bsz, seq, D_MODEL)
    y_sample = x[1][:r_samp].reshape(nseq, t_s, D_MODEL)
    st = lambda k: jnp.stack(outs[k])
    return (y_prompt, y_sample, st('p_ssm'), st('p_conv'), st('p_mk'), st('p_mv'), st('p_wk'), st('p_wv'),
            st('s_ssm'), st('s_conv'), st('s_wk'), st('s_wv'))
```
